```python
import jax, jax.numpy as jnp
from jax import lax
import numpy as np

D_MODEL = 1024
BATCH = 4
SEQ = 4096
DEPTH = 2
DEC_BATCH = 32
DEC_SEQ = 4
PAST_LEN = 8192
PAGE_SIZE = 128

CONV_DIM = D_MODEL
CONV_W = 3
N_HEADS = 16
HEAD_DIM = 64
N_KV = 4
GROUP = N_HEADS // N_KV
ATT_DIM = N_HEADS * HEAD_DIM
KV_DIM = N_KV * HEAD_DIM
CMP_STRIDE = 16
CMP_BLOCK = 2 * CMP_STRIDE
CMP_HID = 4 * HEAD_DIM
SLC_BLOCK = 64
TOP_N = 16
WINDOW = 512
PLE_DIM = 256
ROPE_THETA = 10000.0
RMS_EPS = 1e-6
Q_BLOCK = 64
N_KV_SLOTS = 4
IN_COLS = 4 * CONV_DIM + 2 * ATT_DIM + 6 * KV_DIM + 3 * N_HEADS + 2 * D_MODEL
NEG_INF = -1e30
FORCE_SCORE = 1e6

kernel_name = "hybrid_shortconv_nsa_decoder_step"


def _rms_norm(x, g):
    xf = x.astype(jnp.float32)
    y = xf * lax.rsqrt(jnp.mean(xf * xf, axis=-1, keepdims=True) + RMS_EPS)
    return (y * g.astype(jnp.float32)).astype(x.dtype)


def _rope(x, pos):
    half = HEAD_DIM // 2
    inv = jnp.power(ROPE_THETA, -jnp.arange(half, dtype=jnp.float32) / half)
    ang = pos.astype(jnp.float32)[:, None] * inv[None, :]
    cos = jnp.cos(ang)[:, None, :]
    sin = jnp.sin(ang)[:, None, :]
    xf = x.astype(jnp.float32)
    x1, x2 = xf[..., :half], xf[..., half:]
    return jnp.concatenate([x1 * cos - x2 * sin, x2 * cos + x1 * sin], axis=-1).astype(x.dtype)


def _masked_probs(s, mask):
    s = jnp.where(mask, s.astype(jnp.float32), NEG_INF)
    m = jnp.max(s, axis=-1, keepdims=True)
    e = jnp.where(mask, jnp.exp(s - m), 0.0)
    return e / jnp.maximum(jnp.sum(e, axis=-1, keepdims=True), 1e-30)


def _project(x, pos, ln_g, w_in):
    B, T, _ = x.shape
    h = _rms_norm(x, ln_g)
    z = jnp.einsum('btd,dc->btc', h, w_in)
    sizes = [CONV_DIM] * 4 + [ATT_DIM] + [KV_DIM] * 6 + [3 * N_HEADS, ATT_DIM, 2 * D_MODEL]
    cuts = np.cumsum(sizes)[:-1].tolist()
    (ba, ca, xa, za, q, kc, vc, ks, vs, kw, vw, g_br, zb, g_mix) = jnp.split(z, cuts, axis=-1)
    q = _rope(q.reshape(B, T, N_HEADS, HEAD_DIM), pos).reshape(B, T, N_KV, GROUP, HEAD_DIM)
    kc = _rope(kc.reshape(B, T, N_KV, HEAD_DIM), pos)
    ks = _rope(ks.reshape(B, T, N_KV, HEAD_DIM), pos)
    kw = _rope(kw.reshape(B, T, N_KV, HEAD_DIM), pos)
    vc = vc.reshape(B, T, N_KV, HEAD_DIM)
    vs = vs.reshape(B, T, N_KV, HEAD_DIM)
    vw = vw.reshape(B, T, N_KV, HEAD_DIM)
    g_br = jax.nn.sigmoid(g_br).reshape(B, T, N_KV, GROUP, 3)
    return ba, ca, xa, za, q, kc, vc, ks, vs, kw, vw, g_br, zb, g_mix


def _short_conv(u, prefix, w):
    T = u.shape[1]
    up = jnp.concatenate([prefix, u], axis=1)
    y = w[0] * up[:, 0:T]
    for j in range(1, CONV_W):
        y = y + w[j] * up[:, j:j + T]
    return y, up[:, T:]


def _compress(k, pos_emb, w1, w2):
    B, L = k.shape[:2]
    n_chunk = L // CMP_STRIDE
    ch = k[:, :n_chunk * CMP_STRIDE].reshape(B, n_chunk, CMP_STRIDE, N_KV, HEAD_DIM)
    first = jnp.einsum('bcsgd,sdh->bcgh', ch + pos_emb[None, None, :CMP_STRIDE, None, :], w1[:CMP_STRIDE])
    second = jnp.einsum('bcsgd,sdh->bcgh', ch + pos_emb[None, None, CMP_STRIDE:, None, :], w1[CMP_STRIDE:])
    hid = jax.nn.silu(first[:, :-1] + second[:, 1:])
    return jnp.einsum('bngh,hd->bngd', hid, w2)


def _to_blocks(k):
    B, L = k.shape[:2]
    n_blk = -(-L // SLC_BLOCK)
    kp = jnp.pad(k, ((0, 0), (0, n_blk * SLC_BLOCK - L), (0, 0), (0, 0)))
    return kp.reshape(B, n_blk, SLC_BLOCK, N_KV, HEAD_DIM).transpose(0, 3, 1, 2, 4)


def _overlap(n_cmp, n_slc):
    start = jnp.arange(n_cmp) * CMP_STRIDE
    sj = jnp.arange(n_slc) * SLC_BLOCK
    ov = (start[:, None] < sj[None, :] + SLC_BLOCK) & (start[:, None] + CMP_BLOCK > sj[None, :])
    return ov.astype(jnp.float32)


def _nsa_core(q, qpos, k_cmp, v_cmp, k_blk, v_blk, overlap, k_win, v_win, kwpos, gates):
    B, Tq = q.shape[:2]
    scale = HEAD_DIM ** -0.5
    n_cmp = k_cmp.shape[1]
    n_slc = k_blk.shape[2]
    n_sel = min(TOP_N, n_slc)
    s_c = jnp.einsum('bqgrd,bngd->bqgrn', q, k_cmp) * scale
    cmp_end = jnp.arange(n_cmp) * CMP_STRIDE + CMP_BLOCK - 1
    mask_c = (cmp_end[None, :] <= qpos[:, None])[None, :, None, None, :]
    p_c = _masked_probs(s_c, mask_c)
    o_c = jnp.einsum('bqgrn,bngd->bqgrd', p_c.astype(v_cmp.dtype), v_cmp)
    imp = jnp.einsum('bqgrn,nj->bqgj', p_c, overlap)
    blk = jnp.arange(n_slc)[None, :]
    cur = (qpos // SLC_BLOCK)[:, None]
    valid = blk <= cur
    forced = (blk == 0) | (blk == cur) | (blk == cur - 1)
    score = jnp.where(forced[None, :, None, :], FORCE_SCORE, imp)
    score = jnp.where(valid[None, :, None, :], score, -1.0)
    top_val, top_idx = lax.top_k(score, n_sel)
    sel_ok = top_val >= 0.0
    idx = top_idx.transpose(0, 2, 1, 3)
    bi = jnp.arange(B)[:, None, None, None]
    gi = jnp.arange(N_KV)[None, :, None, None]
    kg = k_blk[bi, gi, idx].reshape(B, N_KV, Tq, n_sel * SLC_BLOCK, HEAD_DIM)
    vg = v_blk[bi, gi, idx].reshape(B, N_KV, Tq, n_sel * SLC_BLOCK, HEAD_DIM)
    s_s = jnp.einsum('bqgrd,bgqkd->bqgrk', q, kg) * scale
    kpos = top_idx[..., None] * SLC_BLOCK + jnp.arange(SLC_BLOCK)
    mask_s = ((kpos <= qpos[None, :, None, None, None]) & sel_ok[..., None]).reshape(B, Tq, N_KV, n_sel * SLC_BLOCK)
    p_s = _masked_probs(s_s, mask_s[:, :, :, None, :])
    o_s = jnp.einsum('bqgrk,bgqkd->bqgrd', p_s.astype(vg.dtype), vg)
    s_w = jnp.einsum('bqgrd,bkgd->bqgrk', q, k_win) * scale
    dist = qpos[:, None] - kwpos[None, :]
    mask_w = ((dist >= 0) & (dist < WINDOW) & (kwpos[None, :] >= 0))[None, :, None, None, :]
    p_w = _masked_probs(s_w, mask_w)
    o_w = jnp.einsum('bqgrk,bkgd->bqgrd', p_w.astype(v_win.dtype), v_win)
    return gates[..., 0:1] * o_c + gates[..., 1:2] * o_s + gates[..., 2:3] * o_w


def _nsa_prompt(q, kc, vc, ks, vs, kw, vw, gates, cmp_k, cmp_v):
    B, T = q.shape[:2]
    k_cmp = _compress(kc, *cmp_k)
    v_cmp = _compress(vc, *cmp_v)
    k_blk = _to_blocks(ks)
    v_blk = _to_blocks(vs)
    overlap = _overlap(k_cmp.shape[1], k_blk.shape[2])
    pad = ((0, 0), (WINDOW, 0), (0, 0), (0, 0))
    kw_pad = jnp.pad(kw, pad)
    vw_pad = jnp.pad(vw, pad)
    span = WINDOW + Q_BLOCK

    def block(s):
        qpos = s + jnp.arange(Q_BLOCK, dtype=jnp.int32)
        kwpos = s - WINDOW + jnp.arange(span, dtype=jnp.int32)
        return _nsa_core(lax.dynamic_slice_in_dim(q, s, Q_BLOCK, 1), qpos, k_cmp, v_cmp, k_blk, v_blk, overlap,
                         lax.dynamic_slice_in_dim(kw_pad, s, span, 1), lax.dynamic_slice_in_dim(vw_pad, s, span, 1),
                         kwpos, lax.dynamic_slice_in_dim(gates, s, Q_BLOCK, 1))

    starts = jnp.arange(T // Q_BLOCK, dtype=jnp.int32) * Q_BLOCK
    out = lax.map(block, starts)
    return jnp.moveaxis(out, 0, 1).reshape(B, T, ATT_DIM)


def _merge(x, ya, za, yb, zb, g_mix, p_i, w_out_a, w_out_b, w_o, w_ple, w_ple_gate):
    br_a = jnp.einsum('btc,cd->btd', jax.nn.silu(za) * ya, w_out_a)
    br_b = jnp.einsum('btc,cd->btd', jax.nn.silu(zb) * yb, w_out_b)
    g_a, g_b = jnp.split(jax.nn.sigmoid(g_mix), 2, axis=-1)
    x = x + jnp.einsum('btd,de->bte', g_a * br_a + g_b * br_b, w_o)
    gate = jax.nn.sigmoid(jnp.einsum('btd,de->bte', x, w_ple_gate))
    return x + gate * jnp.einsum('btp,pd->btd', p_i, w_ple)


def setup_inputs(seed: int = 0) -> dict:
    key = jax.random.key(seed)
    ks = jax.random.split(key, 24)
    n_pages = PAST_LEN // PAGE_SIZE
    n_pool = (DEC_BATCH * n_pages * 5) // 4
    w_buf = min(WINDOW, PAST_LEN)

    def nrm(k, shape, scale):
        return jax.random.normal(k, shape, jnp.float32) * scale

    page_table = jax.random.permutation(ks[5], n_pool)[:DEC_BATCH * n_pages].reshape(DEC_BATCH, n_pages).astype(jnp.int32)
    return {
        "x_prompt": nrm(ks[0], (BATCH, SEQ, D_MODEL), 1.0),
        "x_sample": nrm(ks[1], (DEC_BATCH, DEC_SEQ, D_MODEL), 1.0),
        "cache_kv": nrm(ks[2], (DEPTH, n_pool, PAGE_SIZE, N_KV_SLOTS, N_KV, HEAD_DIM), 1.0),
        "state_win": nrm(ks[3], (DEPTH, DEC_BATCH, w_buf, 2, N_KV, HEAD_DIM), 1.0),
        "state_conv": nrm(ks[4], (DEPTH, DEC_BATCH, CONV_W - 1, CONV_DIM), 1.0),
        "page_table": page_table,
        "p_prompt": nrm(ks[6], (DEPTH, BATCH, SEQ, PLE_DIM), 1.0),
        "p_sample": nrm(ks[7], (DEPTH, DEC_BATCH, DEC_SEQ, PLE_DIM), 1.0),
        "ln_g": 1.0 + nrm(ks[8], (DEPTH, D_MODEL), 0.05),
        "w_in": nrm(ks[9], (DEPTH, D_MODEL, IN_COLS), D_MODEL ** -0.5),
        "conv_w": nrm(ks[10], (DEPTH, CONV_W, CONV_DIM), CONV_W ** -0.5),
        "cmp_pos_k": nrm(ks[11], (DEPTH, CMP_BLOCK, HEAD_DIM), 0.1),
        "cmp_w1_k": nrm(ks[12], (DEPTH, CMP_BLOCK, HEAD_DIM, CMP_HID), (CMP_BLOCK * HEAD_DIM) ** -0.5),
        "cmp_w2_k": nrm(ks[13], (DEPTH, CMP_HID, HEAD_DIM), CMP_HID ** -0.5),
        "cmp_pos_v": nrm(ks[14], (DEPTH, CMP_BLOCK, HEAD_DIM), 0.1),
        "cmp_w1_v": nrm(ks[15], (DEPTH, CMP_BLOCK, HEAD_DIM, CMP_HID), (CMP_BLOCK * HEAD_DIM) ** -0.5),
        "cmp_w2_v": nrm(ks[16], (DEPTH, CMP_HID, HEAD_DIM), CMP_HID ** -0.5),
        "w_out_a": nrm(ks[17], (DEPTH, CONV_DIM, D_MODEL), CONV_DIM ** -0.5),
        "w_out_b": nrm(ks[18], (DEPTH, ATT_DIM, D_MODEL), ATT_DIM ** -0.5),
        "w_o": nrm(ks[19], (DEPTH, D_MODEL, D_MODEL), D_MODEL ** -0.5),
        "w_ple": nrm(ks[20], (DEPTH, PLE_DIM, D_MODEL), PLE_DIM ** -0.5),
        "w_ple_gate": nrm(ks[21], (DEPTH, D_MODEL, D_MODEL), D_MODEL ** -0.5),
        "final_g": 1.0 + nrm(ks[22], (D_MODEL,), 0.05),
    }


def reference(x_prompt, x_sample, cache_kv, state_win, state_conv, page_table, p_prompt, p_sample,
              ln_g, w_in, conv_w, cmp_pos_k, cmp_w1_k, cmp_w2_k, cmp_pos_v, cmp_w1_v, cmp_w2_v,
              w_out_a, w_out_b, w_o, w_ple, w_ple_gate, final_g):
    n_dec, dec_len = x_sample.shape[0], x_sample.shape[1]
    past_len = page_table.shape[1] * PAGE_SIZE
    w_buf = state_win.shape[2]
    pos_p = jnp.arange(x_prompt.shape[1], dtype=jnp.int32)
    pos_s = past_len + jnp.arange(dec_len, dtype=jnp.int32)
    kwpos_s = past_len - w_buf + jnp.arange(w_buf + dec_len, dtype=jnp.int32)
    w_keep_p = min(WINDOW, x_prompt.shape[1])
    xp, xs = x_prompt, x_sample
    kv_p, kv_s, win_p, win_s, conv_p, conv_s = [], [], [], [], [], []
    for i in range(DEPTH):
        cmp_k = (cmp_pos_k[i], cmp_w1_k[i], cmp_w2_k[i])
        cmp_v = (cmp_pos_v[i], cmp_w1_v[i], cmp_w2_v[i])
        merge_w = (w_out_a[i], w_out_b[i], w_o[i], w_ple[i], w_ple_gate[i])
        ba, ca, xa, za, q, kc, vc, ks, vs, kw, vw, gbr, zb, gmix = _project(xp, pos_p, ln_g[i], w_in[i])
        u = ca * xa
        conv_y, tail = _short_conv(u, jnp.zeros_like(u[:, :CONV_W - 1]), conv_w[i])
        yb = _nsa_prompt(q, kc, vc, ks, vs, kw, vw, gbr, cmp_k, cmp_v)
        xp = _merge(xp, ba * conv_y, za, yb, zb, gmix, p_prompt[i], *merge_w)
        kv_p.append(jnp.stack([kc, vc, ks, vs], axis=2))
        win_p.append(jnp.stack([kw, vw], axis=2)[:, -w_keep_p:])
        conv_p.append(tail)
        ba, ca, xa, za, q, kc, vc, ks, vs, kw, vw, gbr, zb, gmix = _project(xs, pos_s, ln_g[i], w_in[i])
        u = ca * xa
        conv_y, tail = _short_conv(u, state_conv[i], conv_w[i])
        new_kv = jnp.stack([kc, vc, ks, vs], axis=2)
        past = cache_kv[i][page_table].reshape(n_dec, past_len, N_KV_SLOTS, N_KV, HEAD_DIM)
        full = jnp.concatenate([past, new_kv], axis=1)
        win = jnp.concatenate([state_win[i], jnp.stack([kw, vw], axis=2)], axis=1)
        k_cmp = _compress(full[:, :, 0], *cmp_k)
        v_cmp = _compress(full[:, :, 1], *cmp_v)
        k_blk = _to_blocks(full[:, :, 2])
        v_blk = _to_blocks(full[:, :, 3])
        overlap = _overlap(k_cmp.shape[1], k_blk.shape[2])
        yb = _nsa_core(q, pos_s, k_cmp, v_cmp, k_blk, v_blk, overlap, win[:, :, 0], win[:, :, 1], kwpos_s, gbr)
        yb = yb.reshape(n_dec, dec_len, ATT_DIM)
        xs = _merge(xs, ba * conv_y, za, yb, zb, gmix, p_sample[i], *merge_w)
        kv_s.append(new_kv)
        win_s.append(win[:, -w_buf:])
        conv_s.append(tail)
    y_prompt = _rms_norm(xp, final_g)
    y_sample = _rms_norm(xs, final_g)
    return (y_prompt, y_sample, jnp.stack(kv_p), jnp.stack(kv_s), jnp.stack(win_p), jnp.stack(win_s), jnp.stack(conv_p), jnp.stack(conv_s))
```

```python
import functools

import numpy as np
import jax
import jax.numpy as jnp
from jax import lax
from jax.experimental import pallas as pl
from jax.experimental.pallas import tpu as pltpu

D_MODEL = 1024
CONV_DIM = D_MODEL
CONV_W = 3
N_HEADS = 16
HEAD_DIM = 64
N_KV = 4
GROUP = N_HEADS // N_KV
ATT_DIM = N_HEADS * HEAD_DIM
KV_DIM = N_KV * HEAD_DIM
PAGE_SIZE = 128
CMP_STRIDE = 16
CMP_BLOCK = 2 * CMP_STRIDE
CMP_HID = 4 * HEAD_DIM
SLC_BLOCK = 64
TOP_N = 16
WINDOW = 512
PLE_DIM = 256
ROPE_THETA = 10000.0
RMS_EPS = 1e-6
NEG_INF = -1e30
FORCE_SCORE = 1e6

LANE = 128
SUBLANE = 8
VMEM_LIMIT = 56 * 1024 * 1024

OFF_A = 0
OFF_Q = 4 * CONV_DIM
OFF_KV = OFF_Q + N_HEADS * LANE
OFF_WIN = OFF_KV + 4 * KV_DIM
OFF_ZB = OFF_WIN + 2 * KV_DIM
OFF_GM = OFF_ZB + ATT_DIM
OFF_GB = OFF_GM + 2 * D_MODEL
N_PROJ = OFF_GB + N_KV * LANE
MAX_BLOCKS = LANE - HEAD_DIM

F32 = jnp.float32
BF16 = jnp.bfloat16
NT_DIMS = (((1,), (1,)), ((), ()))


def _params(n_axes):
    return pltpu.CompilerParams(dimension_semantics=("arbitrary",) * n_axes, vmem_limit_bytes=VMEM_LIMIT)


def _const_spec(shape):
    zeros = (0,) * len(shape)
    return pl.BlockSpec(shape, lambda *_: zeros, pipeline_mode=pl.Buffered(1))


def _sigmoid(x):
    return 1.0 / (1.0 + jnp.exp(-x))


def _div(x, d):
    assert d & (d - 1) == 0
    return jnp.right_shift(x, d.bit_length() - 1)


def _mod(x, d):
    assert d & (d - 1) == 0
    return jnp.bitwise_and(x, d - 1)


def _rope_slab(x, cos, sin, lane):
    swapped = jnp.where(_mod(lane, HEAD_DIM) < HEAD_DIM // 2, pltpu.roll(x, LANE - HEAD_DIM // 2, 1),
                        pltpu.roll(x, HEAD_DIM // 2, 1))
    return x * cos + swapped * sin


def _proj_kernel(x_ref, g_ref, w_ref, cos_ref, sin_ref, pre_ref, cw_ref,
                 a_ref, q_ref, kv_ref, win_ref, zb_ref, gm_ref, gb_ref, tail_ref, *rest,
                 tm, stride, carry, emit_ext):
    if emit_ext:
        ks_ref, vs_ref, kw_ref, vw_ref, u_scr = rest
    else:
        (u_scr,) = rest
    t = pl.program_id(1)
    x = x_ref[...]
    ms = jnp.mean(x * x, axis=-1, keepdims=True)
    h = (x * lax.rsqrt(ms + RMS_EPS) * g_ref[...]).astype(BF16)

    def seg(off, width):
        return jnp.dot(h, w_ref[:, off:off + width], preferred_element_type=F32)

    @pl.when(t == 0)
    def _():
        u_scr[0:carry, :] = pre_ref[0]

    u = seg(OFF_A + CONV_DIM, CONV_DIM) * seg(OFF_A + 2 * CONV_DIM, CONV_DIM)
    u_scr[carry:carry + tm, :] = u
    cw = cw_ref[...]
    conv = cw[0:1] * u_scr[carry - 2 * stride:carry - 2 * stride + tm, :]
    conv = conv + cw[1:2] * u_scr[carry - stride:carry - stride + tm, :]
    conv = conv + cw[2:3] * u
    ba = seg(OFF_A, CONV_DIM)
    za = seg(OFF_A + 3 * CONV_DIM, CONV_DIM)
    a_ref[...] = ((za * _sigmoid(za)) * (ba * conv)).astype(BF16)
    tail = u_scr[tm:tm + carry, :]
    tail_ref[0] = tail
    u_scr[0:carry, :] = tail

    lane = lax.broadcasted_iota(jnp.int32, (tm, LANE), 1)
    cos = cos_ref[...]
    sin = sin_ref[...]
    scale = HEAD_DIM ** -0.5

    zq = seg(OFF_Q, N_HEADS * LANE)
    for s in range(N_HEADS):
        sl = slice(s * LANE, (s + 1) * LANE)
        q_ref[:, sl] = (_rope_slab(zq[:, sl], cos, sin, lane) * scale).astype(BF16)

    zkv = seg(OFF_KV, 4 * KV_DIM)
    zwin = seg(OFF_WIN, 2 * KV_DIM)
    n_slab = KV_DIM // LANE
    kv_slabs = []
    for s in range(4 * n_slab):
        v = zkv[:, s * LANE:(s + 1) * LANE]
        if (s // n_slab) % 2 == 0:
            v = _rope_slab(v, cos, sin, lane)
        kv_ref[:, s * LANE:(s + 1) * LANE] = v
        kv_slabs.append(v)
    win_slabs = []
    for s in range(2 * n_slab):
        v = zwin[:, s * LANE:(s + 1) * LANE]
        if s // n_slab == 0:
            v = _rope_slab(v, cos, sin, lane)
        win_ref[:, s * LANE:(s + 1) * LANE] = v
        win_slabs.append(v)

    zb = seg(OFF_ZB, ATT_DIM)
    zb_ref[...] = zb * _sigmoid(zb)
    gm_ref[...] = _sigmoid(seg(OFF_GM, 2 * D_MODEL))
    gb_ref[...] = _sigmoid(seg(OFF_GB, N_KV * LANE))

    if emit_ext:
        row = lax.broadcasted_iota(jnp.int32, (tm, LANE), 0)
        blk = _div(t * tm + row, SLC_BLOCK)
        onehot = jnp.where(lane - HEAD_DIM == blk, 1.0, 0.0)
        ones_col = jnp.where(lane == HEAD_DIM, 1.0, 0.0)
        low = lane < HEAD_DIM
        for g in range(N_KV):
            def half(v):
                return v if g % 2 == 0 else pltpu.roll(v, HEAD_DIM, 1)
            ks_ref[0, g] = jnp.where(low, half(kv_slabs[2 * n_slab + g // 2]), onehot).astype(BF16)
            vs_ref[0, g] = jnp.where(low, half(kv_slabs[3 * n_slab + g // 2]), ones_col).astype(BF16)
            kw_ref[0, g] = jnp.where(low, half(win_slabs[g // 2]), 0.0).astype(BF16)
            vw_ref[0, g] = jnp.where(low, half(win_slabs[n_slab + g // 2]), ones_col).astype(BF16)


def _project(x2d, ln_g, w_perm, cos, sin, prefix, conv_w8, *, n_seq, rows_per_seq, tm, stride, carry,
             emit_ext, pos_per_seq):
    nt = rows_per_seq // tm
    n_rows = n_seq * rows_per_seq
    row_map = lambda b, t: (b * nt + t, 0)
    pos_map = (lambda b, t: (t, 0)) if pos_per_seq else row_map
    in_specs = [
        pl.BlockSpec((tm, D_MODEL), row_map),
        _const_spec((1, D_MODEL)),
        _const_spec((D_MODEL, N_PROJ)),
        pl.BlockSpec((tm, LANE), pos_map),
        pl.BlockSpec((tm, LANE), pos_map),
        pl.BlockSpec((1, carry, CONV_DIM), lambda b, t: (b, 0, 0)),
        _const_spec((SUBLANE, CONV_DIM)),
    ]
    out_shape = [
        jax.ShapeDtypeStruct((n_rows, CONV_DIM), BF16),
        jax.ShapeDtypeStruct((n_rows, N_HEADS * LANE), BF16),
        jax.ShapeDtypeStruct((n_rows, 4 * KV_DIM), F32),
        jax.ShapeDtypeStruct((n_rows, 2 * KV_DIM), F32),
        jax.ShapeDtypeStruct((n_rows, ATT_DIM), F32),
        jax.ShapeDtypeStruct((n_rows, 2 * D_MODEL), F32),
        jax.ShapeDtypeStruct((n_rows, N_KV * LANE), F32),
        jax.ShapeDtypeStruct((n_seq, carry, CONV_DIM), F32),
    ]
    out_specs = [
        pl.BlockSpec((tm, CONV_DIM), row_map),
        pl.BlockSpec((tm, N_HEADS * LANE), row_map),
        pl.BlockSpec((tm, 4 * KV_DIM), row_map),
        pl.BlockSpec((tm, 2 * KV_DIM), row_map),
        pl.BlockSpec((tm, ATT_DIM), row_map),
        pl.BlockSpec((tm, 2 * D_MODEL), row_map),
        pl.BlockSpec((tm, N_KV * LANE), row_map),
        pl.BlockSpec((1, carry, CONV_DIM), lambda b, t: (b, 0, 0)),
    ]
    if emit_ext:
        for _ in range(4):
            out_shape.append(jax.ShapeDtypeStruct((n_seq, N_KV, rows_per_seq, LANE), BF16))
            out_specs.append(pl.BlockSpec((1, N_KV, tm, LANE), lambda b, t: (b, 0, t, 0)))
    return pl.pallas_call(
        functools.partial(_proj_kernel, tm=tm, stride=stride, carry=carry, emit_ext=emit_ext),
        grid=(n_seq, nt),
        in_specs=in_specs,
        out_specs=out_specs,
        out_shape=out_shape,
        scratch_shapes=[pltpu.VMEM((carry + tm, CONV_DIM), F32)],
        compiler_params=_params(2),
        name="project_prompt" if emit_ext else "project_sample",
    )(x2d, ln_g, w_perm, cos, sin, prefix, conv_w8)


def _merge_kernel(x_ref, a_ref, gm_ref, p_ref, woa_ref, wob_ref, wo_ref, wple_ref, wpg_ref, fg_ref, y_ref, *rest,
                  y_needs_gate, final):
    br_a = jnp.dot(a_ref[...], woa_ref[...], preferred_element_type=F32)
    if y_needs_gate:
        zb_ref, o_ref = rest
        y = (zb_ref[...] * y_ref[...]).astype(BF16)
    else:
        (o_ref,) = rest
        y = y_ref[...]
    br_b = jnp.dot(y, wob_ref[...], preferred_element_type=F32)
    gm = gm_ref[...]
    mix = gm[:, :D_MODEL] * br_a + gm[:, D_MODEL:] * br_b
    x1 = x_ref[...] + jnp.dot(mix.astype(BF16), wo_ref[...], preferred_element_type=F32)
    gate = _sigmoid(jnp.dot(x1.astype(BF16), wpg_ref[...], preferred_element_type=F32))
    x2 = x1 + gate * jnp.dot(p_ref[...].astype(BF16), wple_ref[...], preferred_element_type=F32)
    if final:
        ms = jnp.mean(x2 * x2, axis=-1, keepdims=True)
        x2 = x2 * lax.rsqrt(ms + RMS_EPS) * fg_ref[...]
    o_ref[...] = x2


def _merge(x2d, a_act, y, zb_act, gmix, p2d, w_out_a, w_out_b, w_o, w_ple, w_pg, final_g, *, tm, final):
    n_rows = x2d.shape[0]
    row = lambda width: pl.BlockSpec((tm, width), lambda t: (t, 0))
    y_args = (y,) if zb_act is None else (y, zb_act)
    return pl.pallas_call(
        functools.partial(_merge_kernel, y_needs_gate=zb_act is not None, final=final),
        grid=(n_rows // tm,),
        in_specs=[row(D_MODEL), row(CONV_DIM), row(2 * D_MODEL), row(PLE_DIM),
                  _const_spec((CONV_DIM, D_MODEL)), _const_spec((ATT_DIM, D_MODEL)),
                  _const_spec((D_MODEL, D_MODEL)), _const_spec((PLE_DIM, D_MODEL)),
                  _const_spec((D_MODEL, D_MODEL)), _const_spec((1, D_MODEL))] + [row(ATT_DIM)] * len(y_args),
        out_specs=row(D_MODEL),
        out_shape=jax.ShapeDtypeStruct((n_rows, D_MODEL), F32),
        compiler_params=_params(1),
        name="merge_prompt" if zb_act is None else "merge_sample",
    )(x2d, a_act, gmix, p2d, w_out_a, w_out_b, w_o, w_ple, w_pg, final_g, *y_args)


def _compress_kernel(pt_ref, *refs, n_pages, grouped):
    pages = refs[:n_pages]
    pos_ref, w1_ref, w2_ref, ko_ref, vo_ref, x_scr, carry_scr = refs[n_pages:]
    st = pl.program_id(1)
    n = n_pages * (PAGE_SIZE // CMP_STRIDE)
    slabs = pages[0].shape[0] // PAGE_SIZE
    lane = lax.broadcasted_iota(jnp.int32, (SUBLANE, LANE), 1)
    low = lane < HEAD_DIM

    @pl.when(st == 0)
    def _():
        carry_scr[...] = jnp.zeros_like(carry_scr)

    row0 = lax.broadcasted_iota(jnp.int32, (n, CMP_HID), 0) == 0
    for slot, out_ref in enumerate((ko_ref, vo_ref)):
        for j in range(n_pages):
            for pair in range(CMP_STRIDE // 2):
                for v in range(KV_DIM // LANE):
                    slab = slot * (KV_DIM // LANE) + v
                    a = pages[j][pl.ds(2 * pair * slabs + slab, SUBLANE, stride=CMP_STRIDE * slabs), :]
                    b = pages[j][pl.ds((2 * pair + 1) * slabs + slab, SUBLANE, stride=CMP_STRIDE * slabs), :]
                    first = jnp.where(low, a, pltpu.roll(b, HEAD_DIM, 1))
                    second = jnp.where(low, pltpu.roll(a, HEAD_DIM, 1), b)
                    for g, val in ((2 * v, first), (2 * v + 1, second)):
                        r0 = g * n + j * SUBLANE
                        x_scr[r0:r0 + SUBLANE, pair * LANE:(pair + 1) * LANE] = val
        xs = x_scr[...]
        pos = pos_ref[slot]
        f = jnp.dot((xs + pos[0:1]).astype(BF16), w1_ref[slot, 0], preferred_element_type=F32)
        s = jnp.dot((xs + pos[1:2]).astype(BF16), w1_ref[slot, 1], preferred_element_type=F32)
        acc = None
        for g in range(N_KV):
            fg = f[g * n:(g + 1) * n]
            prev = carry_scr[slot, g, SUBLANE - 1:SUBLANE, :]
            shifted = jnp.where(row0, prev, pltpu.roll(fg, 1, 0))
            carry_scr[slot, g] = fg[n - SUBLANE:n]
            pre = shifted + s[g * n:(g + 1) * n]
            hid = (pre * _sigmoid(pre)).astype(BF16)
            if grouped:
                out_ref[0, g] = jnp.dot(hid, w2_ref[slot, 0], preferred_element_type=F32).astype(BF16)
            else:
                part = jnp.dot(hid, w2_ref[slot, g], preferred_element_type=F32)
                acc = part if acc is None else acc + part
        if not grouped:
            out_ref[0] = acc.astype(BF16)


def _compress(pages_arr, page_table, pos2, w1, w2, *, n_pages, grouped):
    n_seq, pages_per_seq = page_table.shape
    n_steps = pages_per_seq // n_pages
    n = n_pages * (PAGE_SIZE // CMP_STRIDE)
    n_chunk = pages_per_seq * (PAGE_SIZE // CMP_STRIDE)
    flat = CMP_STRIDE * HEAD_DIM
    page_rows = PAGE_SIZE * (4 * KV_DIM // LANE)

    def page_spec(j):
        return pl.BlockSpec((page_rows, LANE), lambda b, s, pt: (pt[b, s * n_pages + j], 0))

    if grouped:
        out_sds = jax.ShapeDtypeStruct((n_seq, N_KV, n_chunk, LANE), BF16)
        out_spec = pl.BlockSpec((1, N_KV, n, LANE), lambda b, s, pt: (b, 0, s, 0))
    else:
        out_sds = jax.ShapeDtypeStruct((n_seq, n_chunk, KV_DIM), BF16)
        out_spec = pl.BlockSpec((1, n, KV_DIM), lambda b, s, pt: (b, s, 0))
    const = lambda shape: pl.BlockSpec(shape, lambda b, s, pt: (0,) * len(shape), pipeline_mode=pl.Buffered(1))
    grid_spec = pltpu.PrefetchScalarGridSpec(
        num_scalar_prefetch=1,
        grid=(n_seq, n_steps),
        in_specs=[page_spec(j) for j in range(n_pages)] + [const(pos2.shape), const(w1.shape), const(w2.shape)],
        out_specs=[out_spec, out_spec],
        scratch_shapes=[pltpu.VMEM((N_KV * n, flat), F32), pltpu.VMEM((2, N_KV, SUBLANE, CMP_HID), F32)],
    )
    return pl.pallas_call(
        functools.partial(_compress_kernel, n_pages=n_pages, grouped=grouped),
        grid_spec=grid_spec,
        out_shape=[out_sds, out_sds],
        compiler_params=_params(2),
        name="compress_prompt" if grouped else "compress_sample",
    )(page_table, *([pages_arr] * n_pages), pos2, w1, w2)


def _select_bias(score_ref, blk, n_real, n_sel):
    score = score_ref[...]

    def body(i, cnt):
        si = score_ref[pl.ds(i, 1), :]
        return cnt + jnp.where(blk > i, jnp.where(si >= score, 1.0, 0.0), jnp.where(si > score, 1.0, 0.0))

    cnt = lax.fori_loop(0, n_real, body, jnp.zeros(score.shape, F32))
    return jnp.where((cnt < n_sel) & (score >= 0.0), 0.0, NEG_INF)


def _attn_prompt_kernel(q_ref, gb_ref, zb_ref, kc_ref, vc_ref, ks_ref, vs_ref, kw_ref, vw_ref, o_ref,
                        score_scr, *, tq, kchunk, n_cmp, n_blk, seq_len):
    qi = pl.program_id(2)
    s0 = qi * tq
    rows = GROUP * tq
    q4 = q_ref[0]
    q = jnp.concatenate([q4[:, r * LANE:(r + 1) * LANE] for r in range(GROUP)], axis=0)
    qpos = s0 + _mod(lax.broadcasted_iota(jnp.int32, (rows, 1), 0), tq)

    s_c = lax.dot_general(q, kc_ref[0, 0], NT_DIMS, preferred_element_type=F32)
    m_idx = lax.broadcasted_iota(jnp.int32, (1, n_cmp), 1)
    mask_c = (m_idx >= 1) & (CMP_STRIDE * m_idx + CMP_STRIDE - 1 <= qpos)
    s_c = jnp.where(mask_c, s_c, NEG_INF)
    e_c = jnp.where(mask_c, jnp.exp(s_c - jnp.max(s_c, axis=-1, keepdims=True)), 0.0)
    p_c = e_c / jnp.maximum(jnp.sum(e_c, axis=-1, keepdims=True), 1e-30)
    o_c = jnp.dot(p_c.astype(BF16), vc_ref[0, 0], preferred_element_type=F32)

    p_hi = p_c.astype(BF16)
    p_lo = (p_c - p_hi.astype(F32)).astype(BF16)
    bj = lax.broadcasted_iota(jnp.int32, (MAX_BLOCKS, n_cmp), 0)
    bm = lax.broadcasted_iota(jnp.int32, (MAX_BLOCKS, n_cmp), 1)
    ratio = SLC_BLOCK // CMP_STRIDE
    ov_t = jnp.where((bm >= ratio * bj) & (bm <= ratio * bj + ratio) & (bm >= 1), 1.0, 0.0).astype(BF16)
    imp_t = (lax.dot_general(ov_t, p_hi, NT_DIMS, preferred_element_type=F32)
             + lax.dot_general(ov_t, p_lo, NT_DIMS, preferred_element_type=F32))
    imp = imp_t[:, 0:tq]
    for r in range(1, GROUP):
        imp = imp + imp_t[:, r * tq:(r + 1) * tq]
    blk = lax.broadcasted_iota(jnp.int32, (MAX_BLOCKS, tq), 0)
    cur = _div(s0 + lax.broadcasted_iota(jnp.int32, (MAX_BLOCKS, tq), 1), SLC_BLOCK)
    forced = (blk == 0) | (blk == cur) | (blk == cur - 1)
    score = jnp.where(forced, FORCE_SCORE, imp)
    score_scr[...] = jnp.where(blk <= cur, score, -1.0)
    bias_t = _select_bias(score_scr, blk, n_blk, min(TOP_N, n_blk))
    bias = jnp.concatenate([jnp.zeros((HEAD_DIM, tq), F32), bias_t], axis=0).T
    qx = q + jnp.concatenate([bias.astype(BF16)] * GROUP, axis=0)

    def chunk(k0, carry, causal):
        m_run, acc = carry
        s = lax.dot_general(qx, ks_ref[0, 0, pl.ds(k0, kchunk), :], NT_DIMS, preferred_element_type=F32)
        if causal:
            kpos = k0 + lax.broadcasted_iota(jnp.int32, (1, kchunk), 1)
            s = jnp.where(kpos <= qpos, s, NEG_INF)
        m_new = jnp.maximum(m_run, jnp.max(s, axis=-1, keepdims=True))
        p = jnp.exp(s - m_new)
        pv = jnp.dot(p.astype(BF16), vs_ref[0, 0, pl.ds(k0, kchunk), :], preferred_element_type=F32)
        return m_new, jnp.exp(m_run - m_new) * acc + pv

    n_full = s0 // kchunk
    init = (jnp.full((rows, 1), NEG_INF, F32), jnp.zeros((rows, LANE), F32))
    carry = lax.fori_loop(0, n_full, lambda i, c: chunk(pl.multiple_of(i * kchunk, kchunk), c, False), init)
    _, acc = chunk(pl.multiple_of(n_full * kchunk, kchunk), carry, True)
    o_s = acc / jnp.maximum(acc[:, HEAD_DIM:HEAD_DIM + 1], 1e-30)

    span = WINDOW + tq
    w0 = pl.multiple_of(jnp.maximum(s0 - WINDOW, 0), tq)
    s_w = lax.dot_general(q, kw_ref[0, 0, pl.ds(w0, span), :], NT_DIMS, preferred_element_type=F32)
    dist = qpos - (w0 + lax.broadcasted_iota(jnp.int32, (1, span), 1))
    mask_w = (dist >= 0) & (dist < WINDOW)
    s_w = jnp.where(mask_w, s_w, NEG_INF)
    p_w = jnp.where(mask_w, jnp.exp(s_w - jnp.max(s_w, axis=-1, keepdims=True)), 0.0)
    acc_w = jnp.dot(p_w.astype(BF16), vw_ref[0, 0, pl.ds(w0, span), :], preferred_element_type=F32)
    o_w = acc_w / jnp.maximum(acc_w[:, HEAD_DIM:HEAD_DIM + 1], 1e-30)

    gb = gb_ref[0]
    zb = zb_ref[0]
    lane = lax.broadcasted_iota(jnp.int32, (tq, LANE), 1)
    heads = []
    for r in range(GROUP):
        rs = slice(r * tq, (r + 1) * tq)
        heads.append(gb[:, 3 * r:3 * r + 1] * o_c[rs] + gb[:, 3 * r + 1:3 * r + 2] * o_s[rs]
                     + gb[:, 3 * r + 2:3 * r + 3] * o_w[rs])
    for pair in range(GROUP // 2):
        slab = jnp.where(lane < HEAD_DIM, heads[2 * pair], pltpu.roll(heads[2 * pair + 1], HEAD_DIM, 1))
        o_ref[0, :, pair * LANE:(pair + 1) * LANE] = (zb[:, pair * LANE:(pair + 1) * LANE] * slab).astype(BF16)
    del seq_len


def _attn_prompt(qp, gb, zb_act, kc_ext, vc_ext, ks_ext, vs_ext, kw_ext, vw_ext, *, tq, kchunk):
    n_b, seq_len = qp.shape[0], qp.shape[1]
    n_cmp = kc_ext.shape[2]
    n_blk = seq_len // SLC_BLOCK
    head_w = GROUP * HEAD_DIM
    tok = lambda width: pl.BlockSpec((1, tq, width), lambda b, g, i: (b, i, g))
    seq = lambda n: pl.BlockSpec((1, 1, n, LANE), lambda b, g, i: (b, g, 0, 0))
    return pl.pallas_call(
        functools.partial(_attn_prompt_kernel, tq=tq, kchunk=kchunk, n_cmp=n_cmp, n_blk=n_blk, seq_len=seq_len),
        grid=(n_b, N_KV, seq_len // tq),
        in_specs=[tok(GROUP * LANE), tok(LANE), tok(head_w), seq(n_cmp), seq(n_cmp), seq(seq_len), seq(seq_len),
                  seq(seq_len), seq(seq_len)],
        out_specs=tok(head_w),
        out_shape=jax.ShapeDtypeStruct((n_b, seq_len, ATT_DIM), BF16),
        scratch_shapes=[pltpu.VMEM((MAX_BLOCKS, tq), F32)],
        compiler_params=_params(3),
        name="attention_prompt",
    )(qp, gb, zb_act, kc_ext, vc_ext, ks_ext, vs_ext, kw_ext, vw_ext)


def _attn_sample_kernel(pt_ref, *refs, n_pages, past_len, dec_len, n_blk_pad):
    pages = refs[:n_pages]
    (qbd_ref, kc_ref, vc_ref, newkv_ref, win_ref, newwin_ref, gate_ref, o_ref,
     score_scr, bias_scr, m_scr, l_scr, acc_scr, oc_scr) = refs[n_pages:]
    st = pl.program_id(1)
    n_steps = pl.num_programs(1)
    cols = LANE
    tslots = cols // N_HEADS
    qbd = qbd_ref[0]
    tok = _mod(lax.broadcasted_iota(jnp.int32, (1, cols), 1), tslots)
    qpos = past_len + tok
    n_cmp = kc_ref.shape[1]
    n_past_blk = past_len // SLC_BLOCK

    @pl.when(st == 0)
    def _():
        s_c = jnp.dot(kc_ref[0], qbd, preferred_element_type=F32)
        m_idx = lax.broadcasted_iota(jnp.int32, (n_cmp, 1), 0)
        mask_c = (m_idx >= 1) & (CMP_STRIDE * m_idx + CMP_STRIDE - 1 <= qpos)
        s_c = jnp.where(mask_c, s_c, NEG_INF)
        e_c = jnp.where(mask_c, jnp.exp(s_c - jnp.max(s_c, axis=0, keepdims=True)), 0.0)
        p_c = e_c / jnp.maximum(jnp.sum(e_c, axis=0, keepdims=True), 1e-30)
        p_hi = p_c.astype(BF16)
        p_lo = (p_c - p_hi.astype(F32)).astype(BF16)
        vc_t = vc_ref[0].astype(F32).T.astype(BF16)
        oc_scr[...] = jnp.dot(vc_t, p_hi, preferred_element_type=F32)
        bj = lax.broadcasted_iota(jnp.int32, (n_blk_pad, n_cmp), 0)
        bm = lax.broadcasted_iota(jnp.int32, (n_blk_pad, n_cmp), 1)
        ratio = SLC_BLOCK // CMP_STRIDE
        ov_t = jnp.where((bm >= ratio * bj) & (bm <= ratio * bj + ratio) & (bm >= 1), 1.0, 0.0).astype(BF16)
        imp1 = (jnp.dot(ov_t, p_hi, preferred_element_type=F32)
                + jnp.dot(ov_t, p_lo, preferred_element_type=F32))
        ca = lax.broadcasted_iota(jnp.int32, (cols, cols), 0)
        cb = lax.broadcasted_iota(jnp.int32, (cols, cols), 1)
        per_group = GROUP * tslots
        same = jnp.where((_div(ca, per_group) == _div(cb, per_group)) & (_mod(ca, tslots) == _mod(cb, tslots)),
                         1.0, 0.0).astype(BF16)
        i_hi = imp1.astype(BF16)
        i_mid = (imp1 - i_hi.astype(F32)).astype(BF16)
        i_lo = (imp1 - i_hi.astype(F32) - i_mid.astype(F32)).astype(BF16)
        imp = (jnp.dot(i_hi, same, preferred_element_type=F32) + jnp.dot(i_mid, same, preferred_element_type=F32)
               + jnp.dot(i_lo, same, preferred_element_type=F32))
        blk = lax.broadcasted_iota(jnp.int32, (n_blk_pad, cols), 0)
        cur = _div(qpos, SLC_BLOCK)
        forced = (blk == 0) | (blk == cur) | (blk == cur - 1)
        score = jnp.where(forced, FORCE_SCORE, imp)
        score_scr[...] = jnp.where(blk <= cur, score, -1.0)
        n_blk = n_past_blk + 1
        bias_scr[...] = _select_bias(score_scr, blk, n_blk, min(TOP_N, n_blk))
        m_scr[...] = jnp.full(m_scr.shape, NEG_INF, F32)
        l_scr[...] = jnp.zeros(l_scr.shape, F32)
        acc_scr[...] = jnp.zeros(acc_scr.shape, F32)

    def update(s, v_t):
        m_old = m_scr[0:1, :]
        m_new = jnp.maximum(m_old, jnp.max(s, axis=0, keepdims=True))
        alpha = jnp.exp(m_old - m_new)
        p = jnp.exp(s - m_new)
        l_scr[0:1, :] = alpha * l_scr[0:1, :] + jnp.sum(p, axis=0, keepdims=True)
        acc_scr[...] = alpha * acc_scr[...] + jnp.dot(v_t, p.astype(BF16), preferred_element_type=F32)
        m_scr[0:1, :] = m_new

    half = lax.broadcasted_iota(jnp.int32, (PAGE_SIZE, 1), 0) < SLC_BLOCK
    per_page = PAGE_SIZE // SLC_BLOCK
    for j in range(n_pages):
        page = pages[j][0]
        b0 = (st * n_pages + j) * per_page
        s = jnp.dot(page[:, :KV_DIM].astype(BF16), qbd, preferred_element_type=F32)
        s = s + jnp.where(half, bias_scr[pl.ds(b0, 1), :], bias_scr[pl.ds(b0 + 1, 1), :])
        update(s, page[:, KV_DIM:].T.astype(BF16))

    @pl.when(st == n_steps - 1)
    def _():
        def pad_rows(a):
            return jnp.concatenate([a, jnp.zeros((LANE - a.shape[0], a.shape[1]), a.dtype)], axis=0)

        new = pad_rows(newkv_ref[0])
        rown = lax.broadcasted_iota(jnp.int32, (new.shape[0], 1), 0)
        s = jnp.dot(new[:, 2 * KV_DIM:3 * KV_DIM].astype(BF16), qbd, preferred_element_type=F32)
        s = s + bias_scr[n_past_blk:n_past_blk + 1, :]
        s = jnp.where((rown <= tok) & (rown < dec_len), s, NEG_INF)
        update(s, new[:, 3 * KV_DIM:].T.astype(BF16))
        o_s = acc_scr[...] / jnp.maximum(l_scr[0:1, :], 1e-30)

        w_buf = win_ref.shape[1]
        old = win_ref[0]
        neww = pad_rows(newwin_ref[0])
        keys = jnp.concatenate([old[:, :KV_DIM], neww[:, :KV_DIM]], axis=0).astype(BF16)
        vals = jnp.concatenate([old[:, KV_DIM:], neww[:, KV_DIM:]], axis=0)
        s_w = jnp.dot(keys, qbd, preferred_element_type=F32)
        idx = lax.broadcasted_iota(jnp.int32, (keys.shape[0], 1), 0)
        dist = qpos - (past_len - w_buf + idx)
        mask_w = (dist >= 0) & (dist < WINDOW) & (idx < w_buf + dec_len) & (past_len - w_buf + idx >= 0)
        s_w = jnp.where(mask_w, s_w, NEG_INF)
        e_w = jnp.where(mask_w, jnp.exp(s_w - jnp.max(s_w, axis=0, keepdims=True)), 0.0)
        p_w = e_w / jnp.maximum(jnp.sum(e_w, axis=0, keepdims=True), 1e-30)
        o_w = jnp.dot(vals.T.astype(BF16), p_w.astype(BF16), preferred_element_type=F32)
        gate = gate_ref[0]
        o_ref[0] = gate[0:1] * oc_scr[...] + gate[1:2] * o_s + gate[2:3] * o_w


def _attn_sample(cache_i, page_table, qbd, kc_flat, vc_flat, newkv8, win_old, newwin8, gates_t, *, n_pages,
                 past_len, dec_len):
    n_seq, pages_per_seq = page_table.shape
    n_steps = pages_per_seq // n_pages
    n_cmp = kc_flat.shape[1]
    n_blk_pad = -(-(past_len // SLC_BLOCK + 1) // SUBLANE) * SUBLANE
    w_buf = win_old.shape[1]

    def page_spec(j):
        return pl.BlockSpec((1, PAGE_SIZE, 2 * KV_DIM), lambda b, s, pt: (pt[b, s * n_pages + j], 0, 1))

    per_seq = lambda shape: pl.BlockSpec((1,) + shape, lambda b, s, pt: (b,) + (0,) * len(shape))
    grid_spec = pltpu.PrefetchScalarGridSpec(
        num_scalar_prefetch=1,
        grid=(n_seq, n_steps),
        in_specs=[page_spec(j) for j in range(n_pages)] + [
            per_seq((KV_DIM, LANE)), per_seq((n_cmp, KV_DIM)), per_seq((n_cmp, KV_DIM)),
            per_seq((SUBLANE, 4 * KV_DIM)), per_seq((w_buf, 2 * KV_DIM)), per_seq((SUBLANE, 2 * KV_DIM)),
            per_seq((SUBLANE, LANE))],
        out_specs=per_seq((KV_DIM, LANE)),
        scratch_shapes=[pltpu.VMEM((n_blk_pad, LANE), F32), pltpu.VMEM((n_blk_pad, LANE), F32),
                        pltpu.VMEM((SUBLANE, LANE), F32), pltpu.VMEM((SUBLANE, LANE), F32),
                        pltpu.VMEM((KV_DIM, LANE), F32), pltpu.VMEM((KV_DIM, LANE), F32)],
    )
    return pl.pallas_call(
        functools.partial(_attn_sample_kernel, n_pages=n_pages, past_len=past_len, dec_len=dec_len,
                          n_blk_pad=n_blk_pad),
        grid_spec=grid_spec,
        out_shape=jax.ShapeDtypeStruct((n_seq, KV_DIM, LANE), F32),
        compiler_params=_params(2),
        name="attention_sample",
    )(page_table, *([cache_i] * n_pages), qbd, kc_flat, vc_flat, newkv8, win_old, newwin8, gates_t)


def _permute_w_in(w):
    d = w.shape[0]
    c = np.cumsum([0, 4 * CONV_DIM, ATT_DIM, 4 * KV_DIM, 2 * KV_DIM, 3 * N_HEADS, ATT_DIM, 2 * D_MODEL])
    wa, wq, wkv, wwin, wgb, wzb, wgm = (w[:, c[i]:c[i + 1]] for i in range(7))
    wq = jnp.pad(wq.reshape(d, N_HEADS, HEAD_DIM), ((0, 0), (0, 0), (0, LANE - HEAD_DIM))).reshape(d, N_HEADS * LANE)
    wgb = jnp.pad(wgb.reshape(d, N_KV, 3 * GROUP), ((0, 0), (0, 0), (0, LANE - 3 * GROUP))).reshape(d, N_KV * LANE)
    return jnp.concatenate([wa, wq, wkv, wwin, wzb, wgm, wgb], axis=1).astype(BF16)


def _rope_tables(pos):
    half = HEAD_DIM // 2
    inv = jnp.power(ROPE_THETA, -jnp.arange(half, dtype=F32) / half)
    ang = pos.astype(F32)[:, None] * inv[None, :]
    cos, sin = jnp.cos(ang), jnp.sin(ang)
    reps = LANE // HEAD_DIM
    return jnp.tile(jnp.concatenate([cos, cos], axis=1), (1, reps)), jnp.tile(jnp.concatenate([-sin, sin], axis=1), (1, reps))


def _compress_weights(pos_k, w1_k, w2_k, pos_v, w1_v, w2_v, grouped):
    flat = CMP_STRIDE * HEAD_DIM
    pos2 = jnp.stack([jnp.pad(p.reshape(2, flat), ((0, SUBLANE - 2), (0, 0))) for p in (pos_k, pos_v)])
    w1 = jnp.stack([w.reshape(2, flat, CMP_HID) for w in (w1_k, w1_v)]).astype(BF16)
    if grouped:
        w2 = jnp.stack([jnp.pad(w, ((0, 0), (0, LANE - HEAD_DIM)))[None] for w in (w2_k, w2_v)])
    else:
        w2 = jnp.stack([jnp.stack([jnp.pad(w, ((0, 0), (g * HEAD_DIM, KV_DIM - (g + 1) * HEAD_DIM)))
                                   for g in range(N_KV)]) for w in (w2_k, w2_v)])
    return pos2, w1, w2.astype(BF16)


def kernel(x_prompt, x_sample, cache_kv, state_win, state_conv, page_table, p_prompt, p_sample, ln_g, w_in, conv_w,
           cmp_pos_k, cmp_w1_k, cmp_w2_k, cmp_pos_v, cmp_w1_v, cmp_w2_v, w_out_a, w_out_b, w_o, w_ple, w_ple_gate,
           final_g):
    depth = ln_g.shape[0]
    n_b, seq_len, _ = x_prompt.shape
    n_dec, dec_len, _ = x_sample.shape
    pages_per_seq = page_table.shape[1]
    past_len = pages_per_seq * PAGE_SIZE
    n_pool = cache_kv.shape[1]
    w_buf = state_win.shape[2]
    assert seq_len % PAGE_SIZE == 0 and seq_len // SLC_BLOCK <= MAX_BLOCKS and seq_len >= WINDOW + LANE
    assert CONV_W - 1 <= dec_len <= SUBLANE and (n_dec * dec_len) % SUBLANE == 0
    assert w_buf == min(WINDOW, past_len) and N_HEADS * SUBLANE == LANE

    tm_p = min(256, seq_len)
    tq = LANE
    kchunk = min(512, seq_len)
    tm_merge = min(512, seq_len)
    pages_p = min(16, seq_len // PAGE_SIZE)
    pages_s = min(16, pages_per_seq)
    rows_s = n_dec * dec_len
    carry_p = SUBLANE
    carry_s = (CONV_W - 1) * n_dec

    cos_p, sin_p = _rope_tables(jnp.arange(seq_len, dtype=jnp.int32))
    pos_s = past_len + jnp.repeat(jnp.arange(dec_len, dtype=jnp.int32), n_dec)
    cos_s, sin_s = _rope_tables(pos_s)
    prompt_pages = jnp.arange(n_b * (seq_len // PAGE_SIZE), dtype=jnp.int32).reshape(n_b, seq_len // PAGE_SIZE)
    eye_g = jnp.eye(N_KV, dtype=F32)
    tpad = SUBLANE - dec_len

    xp = x_prompt.reshape(n_b * seq_len, D_MODEL)
    xs = x_sample.transpose(1, 0, 2).reshape(rows_s, D_MODEL)
    kv_p, kv_s, win_p, win_s, conv_p, conv_s = [], [], [], [], [], []
    for i in range(depth):
        last = i == depth - 1
        w_perm = _permute_w_in(w_in[i])
        g_row = ln_g[i].reshape(1, D_MODEL)
        cw8 = jnp.pad(conv_w[i], ((0, SUBLANE - CONV_W), (0, 0)))
        merge_w = (w_out_a[i].astype(BF16), w_out_b[i].astype(BF16), w_o[i].astype(BF16), w_ple[i].astype(BF16),
                   w_ple_gate[i].astype(BF16), final_g.reshape(1, D_MODEL))
        cmp_args = (cmp_pos_k[i], cmp_w1_k[i], cmp_w2_k[i], cmp_pos_v[i], cmp_w1_v[i], cmp_w2_v[i])

        (a_act, qp, newkv, win, zb_act, gmix, gb, tail, ks_ext, vs_ext, kw_ext, vw_ext) = _project(
            xp, g_row, w_perm, cos_p, sin_p, jnp.zeros((n_b, carry_p, CONV_DIM), F32), cw8,
            n_seq=n_b, rows_per_seq=seq_len, tm=tm_p, stride=1, carry=carry_p, emit_ext=True, pos_per_seq=True)
        kc_ext, vc_ext = _compress(newkv.reshape(n_b * seq_len * (4 * KV_DIM // LANE), LANE), prompt_pages,
                                   *_compress_weights(*cmp_args, True), n_pages=pages_p, grouped=True)
        y_act = _attn_prompt(qp.reshape(n_b, seq_len, N_HEADS * LANE), gb.reshape(n_b, seq_len, N_KV * LANE),
                             zb_act.reshape(n_b, seq_len, ATT_DIM), kc_ext, vc_ext, ks_ext, vs_ext, kw_ext, vw_ext,
                             tq=tq, kchunk=kchunk)
        xp = _merge(xp, a_act, y_act.reshape(n_b * seq_len, ATT_DIM), None, gmix,
                    p_prompt[i].reshape(n_b * seq_len, PLE_DIM), *merge_w, tm=tm_merge, final=last)
        kv_p.append(newkv.reshape(n_b, seq_len, 4, N_KV, HEAD_DIM))
        win_p.append(win.reshape(n_b, seq_len, 2, N_KV, HEAD_DIM)[:, seq_len - min(WINDOW, seq_len):])
        conv_p.append(tail[:, carry_p - (CONV_W - 1):])

        prefix_s = state_conv[i].transpose(1, 0, 2).reshape(1, carry_s, CONV_DIM)
        (a_act, qp, newkv, win, zb_act, gmix, gb, tail) = _project(
            xs, g_row, w_perm, cos_s, sin_s, prefix_s, cw8,
            n_seq=1, rows_per_seq=rows_s, tm=rows_s, stride=n_dec, carry=carry_s, emit_ext=False, pos_per_seq=False)
        cache_i = cache_kv[i].reshape(n_pool, PAGE_SIZE, 4 * KV_DIM)
        kc_flat, vc_flat = _compress(cache_i.reshape(n_pool * PAGE_SIZE * (4 * KV_DIM // LANE), LANE), page_table,
                                     *_compress_weights(*cmp_args, False), n_pages=pages_s, grouped=False)
        q5 = qp.reshape(dec_len, n_dec, N_KV, GROUP, LANE)[..., :HEAD_DIM].astype(F32)
        q5 = jnp.pad(q5, ((0, tpad), (0, 0), (0, 0), (0, 0), (0, 0)))
        qbd = q5.transpose(1, 4, 2, 3, 0)[:, None] * eye_g[None, :, None, :, None, None]
        qbd = qbd.reshape(n_dec, KV_DIM, LANE).astype(BF16)
        gates_t = gb.reshape(dec_len, n_dec, N_KV, LANE)[..., :3 * GROUP].reshape(dec_len, n_dec, N_KV, GROUP, 3)
        gates_t = jnp.pad(gates_t.transpose(1, 4, 2, 3, 0), ((0, 0), (0, SUBLANE - 3), (0, 0), (0, 0), (0, tpad)))
        gates_t = gates_t.reshape(n_dec, SUBLANE, LANE)
        newkv_b = newkv.reshape(dec_len, n_dec, 4 * KV_DIM).transpose(1, 0, 2)
        newwin_b = win.reshape(dec_len, n_dec, 2 * KV_DIM).transpose(1, 0, 2)
        pad_rows = lambda a: jnp.pad(a, ((0, 0), (0, tpad), (0, 0)))
        win_old = state_win[i].reshape(n_dec, w_buf, 2 * KV_DIM)
        o_t = _attn_sample(cache_i, page_table, qbd, kc_flat, vc_flat, pad_rows(newkv_b), win_old, pad_rows(newwin_b),
                           gates_t, n_pages=pages_s, past_len=past_len, dec_len=dec_len)
        o6 = o_t.reshape(n_dec, N_KV, HEAD_DIM, N_KV, GROUP, SUBLANE)
        yb = jnp.stack([o6[:, g, :, g] for g in range(N_KV)], axis=1)
        yb = yb.transpose(4, 0, 1, 3, 2)[:dec_len].reshape(rows_s, ATT_DIM)
        xs = _merge(xs, a_act, yb, zb_act, gmix, p_sample[i].transpose(1, 0, 2).reshape(rows_s, PLE_DIM), *merge_w,
                    tm=rows_s, final=last)
        kv_s.append(newkv_b.reshape(n_dec, dec_len, 4, N_KV, HEAD_DIM))
        win_s.append(jnp.concatenate([win_old, newwin_b], axis=1)[:, -w_buf:].reshape(n_dec, w_buf, 2, N_KV, HEAD_DIM))
        conv_s.append(tail.reshape(CONV_W - 1, n_dec, CONV_DIM).transpose(1, 0, 2))

    y_prompt = xp.reshape(n_b, seq_len, D_MODEL)
    y_sample = xs.reshape(dec_len, n_dec, D_MODEL).transpose(1, 0, 2)
    return (y_prompt, y_sample, jnp.stack(kv_p), jnp.stack(kv_s), jnp.stack(win_p), jnp.stack(win_s),
            jnp.stack(conv_p), jnp.stack(conv_s))
```

```python
import functools

import numpy as np
import jax
import jax.numpy as jnp
from jax import lax
from jax.experimental import pallas as pl
from jax.experimental.pallas import tpu as pltpu

D_MODEL = 1024
CONV_DIM = D_MODEL
CONV_W = 3
N_HEADS = 16
HEAD_DIM = 64
N_KV = 4
GROUP = N_HEADS // N_KV
ATT_DIM = N_HEADS * HEAD_DIM
KV_DIM = N_KV * HEAD_DIM
PAGE_SIZE = 128
CMP_STRIDE = 16
CMP_BLOCK = 2 * CMP_STRIDE
CMP_HID = 4 * HEAD_DIM
SLC_BLOCK = 64
TOP_N = 16
WINDOW = 512
PLE_DIM = 256
ROPE_THETA = 10000.0
RMS_EPS = 1e-6
NEG_INF = -1e30
FORCE_SCORE = 1e6

LANE = 128
SUBLANE = 8
VMEM_LIMIT = 56 * 1024 * 1024
KV_SLABS = KV_DIM // LANE
N_GATE_ROWS = 2 * SUBLANE

OFF_A = 0
OFF_Q = 4 * CONV_DIM
OFF_KV = OFF_Q + ATT_DIM
OFF_WIN = OFF_KV + 4 * KV_DIM
OFF_ZB = OFF_WIN + 2 * KV_DIM
OFF_GM = OFF_ZB + ATT_DIM
OFF_GB = OFF_GM + 2 * D_MODEL
N_PROJ = OFF_GB + N_KV * LANE
MAX_BLOCKS = LANE - HEAD_DIM

F32 = jnp.float32
BF16 = jnp.bfloat16
NT_DIMS = (((1,), (1,)), ((), ()))


def _params(n_axes):
    return pltpu.CompilerParams(dimension_semantics=("arbitrary",) * n_axes, vmem_limit_bytes=VMEM_LIMIT)


def _const_spec(shape):
    zeros = (0,) * len(shape)
    return pl.BlockSpec(shape, lambda *_: zeros, pipeline_mode=pl.Buffered(1))


def _sigmoid(x):
    return 1.0 / (1.0 + jnp.exp(-x))


def _div(x, d):
    assert d & (d - 1) == 0
    return jnp.right_shift(x, d.bit_length() - 1)


def _mod(x, d):
    assert d & (d - 1) == 0
    return jnp.bitwise_and(x, d - 1)


def _iota(shape, dim):
    return lax.broadcasted_iota(jnp.int32, shape, dim)


def _rope_slab(x, cos, sin, lane):
    swapped = jnp.where(_mod(lane, HEAD_DIM) < HEAD_DIM // 2, pltpu.roll(x, LANE - HEAD_DIM // 2, 1),
                        pltpu.roll(x, HEAD_DIM // 2, 1))
    return x * cos + swapped * sin


def _overlap_t(n_blk, n_cmp):
    bj = _iota((n_blk, n_cmp), 0)
    bm = _iota((n_blk, n_cmp), 1)
    ratio = SLC_BLOCK // CMP_STRIDE
    return jnp.where((bm >= ratio * bj) & (bm <= ratio * bj + ratio) & (bm >= 1), 1.0, 0.0).astype(BF16)


def _proj_kernel(x_ref, g_ref, w_ref, cos_ref, sin_ref, pre_ref, cw_ref,
                 a_ref, kv_ref, win_ref, zb_ref, gm_ref, tail_ref, *rest, tm, stride, carry, transposed):
    if transposed:
        qt_ref, ks_ref, vst_ref, kw_ref, vwt_ref, gbt_ref, u_scr = rest
    else:
        q_ref, gb_ref, u_scr = rest
    t = pl.program_id(1)
    x = x_ref[...]
    ms = jnp.mean(x * x, axis=-1, keepdims=True)
    h = (x * lax.rsqrt(ms + RMS_EPS) * g_ref[...]).astype(BF16)

    def seg(off, width):
        return jnp.dot(h, w_ref[:, off:off + width], preferred_element_type=F32)

    @pl.when(t == 0)
    def _():
        u_scr[0:carry, :] = pre_ref[0]

    u = seg(OFF_A + CONV_DIM, CONV_DIM) * seg(OFF_A + 2 * CONV_DIM, CONV_DIM)
    u_scr[carry:carry + tm, :] = u
    cw = cw_ref[...]
    conv = cw[0:1] * u_scr[carry - 2 * stride:carry - 2 * stride + tm, :]
    conv = conv + cw[1:2] * u_scr[carry - stride:carry - stride + tm, :]
    conv = conv + cw[2:3] * u
    ba = seg(OFF_A, CONV_DIM)
    za = seg(OFF_A + 3 * CONV_DIM, CONV_DIM)
    a_ref[...] = ((za * _sigmoid(za)) * (ba * conv)).astype(BF16)
    tail = u_scr[tm:tm + carry, :]
    tail_ref[0] = tail
    u_scr[0:carry, :] = tail

    lane = _iota((tm, LANE), 1)
    cos = cos_ref[...]
    sin = sin_ref[...]
    scale = HEAD_DIM ** -0.5

    zq = seg(OFF_Q, ATT_DIM)
    for s in range(ATT_DIM // LANE):
        qs = _rope_slab(zq[:, s * LANE:(s + 1) * LANE], cos, sin, lane) * scale
        if transposed:
            qst = qs.T.astype(BF16)
            qt_ref[0, 2 * s] = qst[:HEAD_DIM]
            qt_ref[0, 2 * s + 1] = qst[HEAD_DIM:]
        else:
            q_ref[:, s * LANE:(s + 1) * LANE] = qs.astype(BF16)

    zkv = seg(OFF_KV, 4 * KV_DIM)
    zwin = seg(OFF_WIN, 2 * KV_DIM)
    kv_slabs = []
    for s in range(4 * KV_SLABS):
        v = zkv[:, s * LANE:(s + 1) * LANE]
        if (s // KV_SLABS) % 2 == 0:
            v = _rope_slab(v, cos, sin, lane)
        kv_ref[:, s * LANE:(s + 1) * LANE] = v
        kv_slabs.append(v)
    win_slabs = []
    for s in range(2 * KV_SLABS):
        v = zwin[:, s * LANE:(s + 1) * LANE]
        if s // KV_SLABS == 0:
            v = _rope_slab(v, cos, sin, lane)
        win_ref[:, s * LANE:(s + 1) * LANE] = v
        win_slabs.append(v)

    zb = seg(OFF_ZB, ATT_DIM)
    zb_ref[...] = zb * _sigmoid(zb)
    gm_ref[...] = _sigmoid(seg(OFF_GM, 2 * D_MODEL))
    gb = _sigmoid(seg(OFF_GB, N_KV * LANE))

    if not transposed:
        gb_ref[...] = gb
        return

    row = _iota((tm, LANE), 0)
    onehot = jnp.where(lane - HEAD_DIM == _div(t * tm + row, SLC_BLOCK), 1.0, 0.0)
    low = lane < HEAD_DIM
    ones_row = jnp.where(_iota((HEAD_DIM, tm), 0) == 0, 1.0, 0.0).astype(BF16)
    vs_t = [kv_slabs[3 * KV_SLABS + s].T.astype(BF16) for s in range(KV_SLABS)]
    vw_t = [win_slabs[KV_SLABS + s].T.astype(BF16) for s in range(KV_SLABS)]
    for g in range(N_KV):
        s, hi = divmod(g, 2)

        def half(v):
            return v if hi == 0 else pltpu.roll(v, HEAD_DIM, 1)

        ks_ref[0, g] = jnp.where(low, half(kv_slabs[2 * KV_SLABS + s]), onehot).astype(BF16)
        kw_ref[0, g] = jnp.where(low, half(win_slabs[s]), 0.0).astype(BF16)
        vst_ref[0, g] = jnp.concatenate([vs_t[s][hi * HEAD_DIM:(hi + 1) * HEAD_DIM], ones_row], axis=0)
        vwt_ref[0, g] = jnp.concatenate([vw_t[s][hi * HEAD_DIM:(hi + 1) * HEAD_DIM], ones_row], axis=0)
        gbt_ref[0, g] = gb[:, g * LANE:(g + 1) * LANE].T[:N_GATE_ROWS]


def _project(x2d, ln_g, w_perm, cos, sin, prefix, conv_w8, *, n_seq, rows_per_seq, tm, stride, carry, transposed,
             pos_per_seq):
    nt = rows_per_seq // tm
    n_rows = n_seq * rows_per_seq
    row_map = lambda b, t: (b * nt + t, 0)
    pos_map = (lambda b, t: (t, 0)) if pos_per_seq else row_map
    in_specs = [
        pl.BlockSpec((tm, D_MODEL), row_map),
        _const_spec((1, D_MODEL)),
        _const_spec((D_MODEL, N_PROJ)),
        pl.BlockSpec((tm, LANE), pos_map),
        pl.BlockSpec((tm, LANE), pos_map),
        pl.BlockSpec((1, carry, CONV_DIM), lambda b, t: (b, 0, 0)),
        _const_spec((SUBLANE, CONV_DIM)),
    ]
    rows = lambda width, dtype: (jax.ShapeDtypeStruct((n_rows, width), dtype), pl.BlockSpec((tm, width), row_map))
    outs = [
        rows(CONV_DIM, BF16),
        rows(4 * KV_DIM, F32),
        rows(2 * KV_DIM, F32),
        rows(ATT_DIM, F32),
        rows(2 * D_MODEL, F32),
        (jax.ShapeDtypeStruct((n_seq, carry, CONV_DIM), F32),
         pl.BlockSpec((1, carry, CONV_DIM), lambda b, t: (b, 0, 0))),
    ]
    if transposed:
        tok_sub = (jax.ShapeDtypeStruct((n_seq, N_KV, rows_per_seq, LANE), BF16),
                   pl.BlockSpec((1, N_KV, tm, LANE), lambda b, t: (b, 0, t, 0)))
        tok_lane = lambda n, rows_, dtype: (jax.ShapeDtypeStruct((n_seq, n, rows_, rows_per_seq), dtype),
                                            pl.BlockSpec((1, n, rows_, tm), lambda b, t: (b, 0, 0, t)))
        outs += [tok_lane(N_HEADS, HEAD_DIM, BF16),
                 tok_sub, tok_lane(N_KV, LANE, BF16),
                 tok_sub, tok_lane(N_KV, LANE, BF16),
                 tok_lane(N_KV, N_GATE_ROWS, F32)]
    else:
        outs += [rows(ATT_DIM, BF16), rows(N_KV * LANE, F32)]
    return pl.pallas_call(
        functools.partial(_proj_kernel, tm=tm, stride=stride, carry=carry, transposed=transposed),
        grid=(n_seq, nt),
        in_specs=in_specs,
        out_specs=[o[1] for o in outs],
        out_shape=[o[0] for o in outs],
        scratch_shapes=[pltpu.VMEM((carry + tm, CONV_DIM), F32)],
        compiler_params=_params(2),
        name="project_prompt" if transposed else "project_sample",
    )(x2d, ln_g, w_perm, cos, sin, prefix, conv_w8)


def _merge_kernel(x_ref, a_ref, gm_ref, p_ref, woa_ref, wob_ref, wo_ref, wple_ref, wpg_ref, fg_ref, y_ref, *rest,
                  y_needs_gate, final):
    br_a = jnp.dot(a_ref[...], woa_ref[...], preferred_element_type=F32)
    if y_needs_gate:
        zb_ref, o_ref = rest
        y = (zb_ref[...] * y_ref[...]).astype(BF16)
    else:
        (o_ref,) = rest
        y = y_ref[...]
    br_b = jnp.dot(y, wob_ref[...], preferred_element_type=F32)
    gm = gm_ref[...]
    mix = gm[:, :D_MODEL] * br_a + gm[:, D_MODEL:] * br_b
    x1 = x_ref[...] + jnp.dot(mix.astype(BF16), wo_ref[...], preferred_element_type=F32)
    gate = _sigmoid(jnp.dot(x1.astype(BF16), wpg_ref[...], preferred_element_type=F32))
    x2 = x1 + gate * jnp.dot(p_ref[...].astype(BF16), wple_ref[...], preferred_element_type=F32)
    if final:
        ms = jnp.mean(x2 * x2, axis=-1, keepdims=True)
        x2 = x2 * lax.rsqrt(ms + RMS_EPS) * fg_ref[...]
    o_ref[...] = x2


def _merge(x2d, a_act, y, zb_act, gmix, p2d, w_out_a, w_out_b, w_o, w_ple, w_pg, final_g, *, tm, final):
    n_rows = x2d.shape[0]
    row = lambda width: pl.BlockSpec((tm, width), lambda t: (t, 0))
    y_args = (y,) if zb_act is None else (y, zb_act)
    return pl.pallas_call(
        functools.partial(_merge_kernel, y_needs_gate=zb_act is not None, final=final),
        grid=(n_rows // tm,),
        in_specs=[row(D_MODEL), row(CONV_DIM), row(2 * D_MODEL), row(PLE_DIM),
                  _const_spec((CONV_DIM, D_MODEL)), _const_spec((ATT_DIM, D_MODEL)),
                  _const_spec((D_MODEL, D_MODEL)), _const_spec((PLE_DIM, D_MODEL)),
                  _const_spec((D_MODEL, D_MODEL)), _const_spec((1, D_MODEL))] + [row(ATT_DIM)] * len(y_args),
        out_specs=row(D_MODEL),
        out_shape=jax.ShapeDtypeStruct((n_rows, D_MODEL), F32),
        compiler_params=_params(1),
        name="merge_prompt" if zb_act is None else "merge_sample",
    )(x2d, a_act, gmix, p2d, w_out_a, w_out_b, w_o, w_ple, w_pg, final_g, *y_args)


def _compress_kernel(pt_ref, *refs, n_pages, prompt):
    pages = refs[:n_pages]
    pos_ref, w1_ref, w2_ref, ko_ref, vo_ref, rows_scr, x_scr, carry_scr = refs[n_pages:]
    st = pl.program_id(1)
    per_page = PAGE_SIZE // CMP_STRIDE
    n = n_pages * per_page
    low = _iota((n, LANE), 1) < HEAD_DIM

    @pl.when(st == 0)
    def _():
        carry_scr[...] = jnp.zeros_like(carry_scr)

    row0 = _iota((n, CMP_HID), 0) == 0
    for slot, out_ref in enumerate((ko_ref, vo_ref)):
        for j in range(n_pages):
            for v in range(KV_SLABS):
                slab = slot * KV_SLABS + v
                if prompt:
                    piece = pages[j][pl.ds(slab, PAGE_SIZE, stride=4 * KV_SLABS), :]
                else:
                    piece = pages[j][slot, v * LANE:(v + 1) * LANE, :].T
                rows_scr[v, j * PAGE_SIZE:(j + 1) * PAGE_SIZE, :] = piece
        for pair in range(CMP_STRIDE // 2):
            for v in range(KV_SLABS):
                a = rows_scr[v, pl.ds(2 * pair, n, stride=CMP_STRIDE), :]
                b = rows_scr[v, pl.ds(2 * pair + 1, n, stride=CMP_STRIDE), :]
                x_scr[2 * v * n:(2 * v + 1) * n, pair * LANE:(pair + 1) * LANE] = jnp.where(
                    low, a, pltpu.roll(b, HEAD_DIM, 1))
                x_scr[(2 * v + 1) * n:(2 * v + 2) * n, pair * LANE:(pair + 1) * LANE] = jnp.where(
                    low, pltpu.roll(a, HEAD_DIM, 1), b)
        xs = x_scr[...]
        pos = pos_ref[slot]
        f = jnp.dot((xs + pos[0:1]).astype(BF16), w1_ref[slot, 0], preferred_element_type=F32)
        s = jnp.dot((xs + pos[1:2]).astype(BF16), w1_ref[slot, 1], preferred_element_type=F32)
        acc = None
        for g in range(N_KV):
            fg = f[g * n:(g + 1) * n]
            prev = carry_scr[slot, g, SUBLANE - 1:SUBLANE, :]
            shifted = jnp.where(row0, prev, pltpu.roll(fg, 1, 0))
            carry_scr[slot, g] = fg[n - SUBLANE:n]
            pre = shifted + s[g * n:(g + 1) * n]
            hid = (pre * _sigmoid(pre)).astype(BF16)
            if prompt and slot == 0:
                out_ref[0, g] = jnp.dot(hid, w2_ref[0, 0, :, :LANE], preferred_element_type=F32).astype(BF16)
            elif prompt:
                out_ref[0, g] = lax.dot_general(w2_ref[1, 0, :LANE, :], hid, NT_DIMS,
                                                preferred_element_type=F32).astype(BF16)
            else:
                part = jnp.dot(hid, w2_ref[slot, g], preferred_element_type=F32)
                acc = part if acc is None else acc + part
        if not prompt:
            out_ref[0] = acc.astype(BF16)


def _compress(pages_arr, page_table, page_index, pos2, w1, w2, *, n_pages, prompt):
    n_seq, pages_per_seq = page_table.shape
    n_steps = pages_per_seq // n_pages
    n = n_pages * (PAGE_SIZE // CMP_STRIDE)
    n_chunk = pages_per_seq * (PAGE_SIZE // CMP_STRIDE)
    flat = CMP_STRIDE * HEAD_DIM

    def page_spec(j):
        if prompt:
            return pl.BlockSpec((PAGE_SIZE * 4 * KV_SLABS, LANE),
                                lambda b, s, pt: (page_index(pt[b, s * n_pages + j]), 0))
        return pl.BlockSpec((2, KV_DIM, PAGE_SIZE), lambda b, s, pt: (page_index(pt[b, s * n_pages + j]), 0, 0))

    if prompt:
        out_sds = [jax.ShapeDtypeStruct((n_seq, N_KV, n_chunk, LANE), BF16),
                   jax.ShapeDtypeStruct((n_seq, N_KV, LANE, n_chunk), BF16)]
        out_specs = [pl.BlockSpec((1, N_KV, n, LANE), lambda b, s, pt: (b, 0, s, 0)),
                     pl.BlockSpec((1, N_KV, LANE, n), lambda b, s, pt: (b, 0, 0, s))]
    else:
        out_sds = [jax.ShapeDtypeStruct((n_seq, n_chunk, KV_DIM), BF16)] * 2
        out_specs = [pl.BlockSpec((1, n, KV_DIM), lambda b, s, pt: (b, s, 0))] * 2
    const = lambda shape: pl.BlockSpec(shape, lambda b, s, pt: (0,) * len(shape), pipeline_mode=pl.Buffered(1))
    grid_spec = pltpu.PrefetchScalarGridSpec(
        num_scalar_prefetch=1,
        grid=(n_seq, n_steps),
        in_specs=[page_spec(j) for j in range(n_pages)] + [const(pos2.shape), const(w1.shape), const(w2.shape)],
        out_specs=out_specs,
        scratch_shapes=[pltpu.VMEM((KV_SLABS, n_pages * PAGE_SIZE, LANE), F32), pltpu.VMEM((N_KV * n, flat), F32),
                        pltpu.VMEM((2, N_KV, SUBLANE, CMP_HID), F32)],
    )
    return pl.pallas_call(
        functools.partial(_compress_kernel, n_pages=n_pages, prompt=prompt),
        grid_spec=grid_spec,
        out_shape=out_sds,
        compiler_params=_params(2),
        name="compress_prompt" if prompt else "compress_sample",
    )(page_table, *([pages_arr] * n_pages), pos2, w1, w2)


def _select_bias(score_ref, blk, n_real, n_sel):
    score = score_ref[...]

    def body(i, cnt):
        si = score_ref[pl.ds(i, 1), :]
        return cnt + jnp.where(blk > i, jnp.where(si >= score, 1.0, 0.0), jnp.where(si > score, 1.0, 0.0))

    cnt = lax.fori_loop(0, n_real, body, jnp.zeros(score.shape, F32))
    return jnp.where((cnt < n_sel) & (score >= 0.0), 0.0, NEG_INF)


def _attn_prompt_kernel(qt_ref, gbt_ref, zb_ref, kc_ref, vct_ref, ks_ref, vst_ref, kw_ref, vwt_ref, o_ref,
                        score_scr, *, tq, kchunk, n_cmp, n_blk):
    qi = pl.program_id(2)
    s0 = qi * tq
    rows = GROUP * tq
    qpos = s0 + _mod(_iota((1, rows), 1), tq)

    def stack_q(lower):
        return jnp.concatenate([jnp.concatenate([qt_ref[0, r], lower], axis=0) for r in range(GROUP)], axis=1)

    q0 = stack_q(jnp.zeros((LANE - HEAD_DIM, tq), BF16))

    s_c = jnp.dot(kc_ref[0, 0], q0, preferred_element_type=F32)
    m_idx = _iota((n_cmp, 1), 0)
    mask_c = (m_idx >= 1) & (CMP_STRIDE * m_idx + CMP_STRIDE - 1 <= qpos)
    s_c = jnp.where(mask_c, s_c, NEG_INF)
    e_c = jnp.where(mask_c, jnp.exp(s_c - jnp.max(s_c, axis=0, keepdims=True)), 0.0)
    p_c = e_c / jnp.maximum(jnp.sum(e_c, axis=0, keepdims=True), 1e-30)
    p_hi = p_c.astype(BF16)
    p_lo = (p_c - p_hi.astype(F32)).astype(BF16)
    o_c = jnp.dot(vct_ref[0, 0], p_hi, preferred_element_type=F32)

    ov_t = _overlap_t(MAX_BLOCKS, n_cmp)
    imp_t = jnp.dot(ov_t, p_hi, preferred_element_type=F32) + jnp.dot(ov_t, p_lo, preferred_element_type=F32)
    imp = imp_t[:, 0:tq]
    for r in range(1, GROUP):
        imp = imp + imp_t[:, r * tq:(r + 1) * tq]
    blk = _iota((MAX_BLOCKS, tq), 0)
    cur = _div(s0 + _iota((MAX_BLOCKS, tq), 1), SLC_BLOCK)
    forced = (blk == 0) | (blk == cur) | (blk == cur - 1)
    score = jnp.where(forced, FORCE_SCORE, imp)
    score_scr[...] = jnp.where(blk <= cur, score, -1.0)
    bias_t = _select_bias(score_scr, blk, n_blk, min(TOP_N, n_blk))
    qx = stack_q(bias_t.astype(BF16))

    def chunk(k0, carry, causal):
        m_run, acc = carry
        s = jnp.dot(ks_ref[0, 0, pl.ds(k0, kchunk), :], qx, preferred_element_type=F32)
        if causal:
            s = jnp.where(k0 + _iota((kchunk, 1), 0) <= qpos, s, NEG_INF)
        m_new = jnp.maximum(m_run, jnp.max(s, axis=0, keepdims=True))
        p = jnp.exp(s - m_new).astype(BF16)
        pv = jnp.dot(vst_ref[0, 0, :, pl.ds(k0, kchunk)], p, preferred_element_type=F32)
        return m_new, jnp.exp(m_run - m_new) * acc + pv

    n_full = s0 // kchunk
    init = (jnp.full((1, rows), NEG_INF, F32), jnp.zeros((LANE, rows), F32))
    carry = lax.fori_loop(0, n_full, lambda i, c: chunk(pl.multiple_of(i * kchunk, kchunk), c, False), init)
    _, acc = chunk(pl.multiple_of(n_full * kchunk, kchunk), carry, True)
    o_s = acc[:HEAD_DIM] / jnp.maximum(acc[HEAD_DIM:HEAD_DIM + 1], 1e-30)

    span = WINDOW + tq
    w0 = pl.multiple_of(jnp.maximum(s0 - WINDOW, 0), tq)
    s_w = jnp.dot(kw_ref[0, 0, pl.ds(w0, span), :], q0, preferred_element_type=F32)
    dist = qpos - (w0 + _iota((span, 1), 0))
    mask_w = (dist >= 0) & (dist < WINDOW)
    s_w = jnp.where(mask_w, s_w, NEG_INF)
    p_w = jnp.where(mask_w, jnp.exp(s_w - jnp.max(s_w, axis=0, keepdims=True)), 0.0).astype(BF16)
    acc_w = jnp.dot(vwt_ref[0, 0, :, pl.ds(w0, span)], p_w, preferred_element_type=F32)
    o_w = acc_w[:HEAD_DIM] / jnp.maximum(acc_w[HEAD_DIM:HEAD_DIM + 1], 1e-30)

    gb = gbt_ref[0, 0]
    zb = zb_ref[0]
    heads = []
    for r in range(GROUP):
        rs = slice(r * tq, (r + 1) * tq)
        heads.append(gb[3 * r:3 * r + 1] * o_c[:HEAD_DIM, rs] + gb[3 * r + 1:3 * r + 2] * o_s[:, rs]
                     + gb[3 * r + 2:3 * r + 3] * o_w[:, rs])
    for pair in range(GROUP // 2):
        slab = jnp.concatenate([heads[2 * pair], heads[2 * pair + 1]], axis=0).T
        o_ref[0, :, pair * LANE:(pair + 1) * LANE] = (zb[:, pair * LANE:(pair + 1) * LANE] * slab).astype(BF16)


def _attn_prompt(q_t, gb_t, zb_act, kc_ext, vc_t, ks_ext, vs_t, kw_ext, vw_t, *, tq, kchunk):
    n_b, seq_len = zb_act.shape[0], zb_act.shape[1]
    n_cmp = kc_ext.shape[2]
    n_blk = seq_len // SLC_BLOCK
    head_w = GROUP * HEAD_DIM
    tok_sub = lambda n: pl.BlockSpec((1, 1, n, LANE), lambda b, g, i: (b, g, 0, 0))
    tok_lane = lambda n: pl.BlockSpec((1, 1, LANE, n), lambda b, g, i: (b, g, 0, 0))
    return pl.pallas_call(
        functools.partial(_attn_prompt_kernel, tq=tq, kchunk=kchunk, n_cmp=n_cmp, n_blk=n_blk),
        grid=(n_b, N_KV, seq_len // tq),
        in_specs=[pl.BlockSpec((1, GROUP, HEAD_DIM, tq), lambda b, g, i: (b, g, 0, i)),
                  pl.BlockSpec((1, 1, N_GATE_ROWS, tq), lambda b, g, i: (b, g, 0, i)),
                  pl.BlockSpec((1, tq, head_w), lambda b, g, i: (b, i, g)),
                  tok_sub(n_cmp), tok_lane(n_cmp), tok_sub(seq_len), tok_lane(seq_len), tok_sub(seq_len),
                  tok_lane(seq_len)],
        out_specs=pl.BlockSpec((1, tq, head_w), lambda b, g, i: (b, i, g)),
        out_shape=jax.ShapeDtypeStruct((n_b, seq_len, ATT_DIM), BF16),
        scratch_shapes=[pltpu.VMEM((MAX_BLOCKS, tq), F32)],
        compiler_params=_params(3),
        name="attention_prompt",
    )(q_t, gb_t, zb_act, kc_ext, vc_t, ks_ext, vs_t, kw_ext, vw_t)


def _attn_sample_kernel(pt_ref, *refs, n_pages, past_len, dec_len, n_blk_pad):
    pages = refs[:n_pages]
    (qbd_ref, qbdt_ref, kc_ref, vc_ref, newkv_ref, win_ref, newwin_ref, gate_ref, o_ref,
     score_scr, bias_scr, m_scr, l_scr, acc_scr, oc_scr) = refs[n_pages:]
    st = pl.program_id(1)
    n_steps = pl.num_programs(1)
    cols = LANE
    tslots = cols // N_HEADS
    n_cmp = kc_ref.shape[1]
    n_past_blk = past_len // SLC_BLOCK
    per_page = PAGE_SIZE // SLC_BLOCK
    qbdt = qbdt_ref[0]
    tok_c = _mod(_iota((cols, 1), 0), tslots)

    @pl.when(st == 0)
    def _():
        qpos = past_len + _mod(_iota((1, cols), 1), tslots)
        s_c = jnp.dot(kc_ref[0], qbd_ref[0], preferred_element_type=F32)
        m_idx = _iota((n_cmp, 1), 0)
        mask_c = (m_idx >= 1) & (CMP_STRIDE * m_idx + CMP_STRIDE - 1 <= qpos)
        s_c = jnp.where(mask_c, s_c, NEG_INF)
        e_c = jnp.where(mask_c, jnp.exp(s_c - jnp.max(s_c, axis=0, keepdims=True)), 0.0)
        p_c = e_c / jnp.maximum(jnp.sum(e_c, axis=0, keepdims=True), 1e-30)
        p_hi = p_c.astype(BF16)
        p_lo = (p_c - p_hi.astype(F32)).astype(BF16)
        oc_scr[...] = jnp.dot(p_c.T.astype(BF16), vc_ref[0], preferred_element_type=F32)
        ov_t = _overlap_t(n_blk_pad, n_cmp)
        imp1 = jnp.dot(ov_t, p_hi, preferred_element_type=F32) + jnp.dot(ov_t, p_lo, preferred_element_type=F32)
        ca = _iota((cols, cols), 0)
        cb = _iota((cols, cols), 1)
        per_group = GROUP * tslots
        same = jnp.where((_div(ca, per_group) == _div(cb, per_group)) & (_mod(ca, tslots) == _mod(cb, tslots)),
                         1.0, 0.0).astype(BF16)
        i_hi = imp1.astype(BF16)
        i_mid = (imp1 - i_hi.astype(F32)).astype(BF16)
        i_lo = (imp1 - i_hi.astype(F32) - i_mid.astype(F32)).astype(BF16)
        imp = (jnp.dot(i_hi, same, preferred_element_type=F32) + jnp.dot(i_mid, same, preferred_element_type=F32)
               + jnp.dot(i_lo, same, preferred_element_type=F32))
        blk = _iota((n_blk_pad, cols), 0)
        cur = _div(qpos, SLC_BLOCK)
        forced = (blk == 0) | (blk == cur) | (blk == cur - 1)
        score = jnp.where(forced, FORCE_SCORE, imp)
        score_scr[...] = jnp.where(blk <= cur, score, -1.0)
        n_blk = n_past_blk + 1
        bias_scr[...] = _select_bias(score_scr, blk, n_blk, min(TOP_N, n_blk)).T
        m_scr[...] = jnp.full(m_scr.shape, NEG_INF, F32)
        l_scr[...] = jnp.zeros(l_scr.shape, F32)
        acc_scr[...] = jnp.zeros(acc_scr.shape, F32)

    def update(s, pv_fn):
        m_old = m_scr[...]
        m_new = jnp.maximum(m_old, jnp.max(s, axis=1, keepdims=True))
        alpha = jnp.exp(m_old - m_new)
        p = jnp.exp(s - m_new)
        l_scr[...] = alpha * l_scr[...] + jnp.sum(p, axis=1, keepdims=True)
        acc_scr[...] = alpha * acc_scr[...] + pv_fn(p.astype(BF16))
        m_scr[...] = m_new

    bias_b = bias_scr[...].astype(BF16)
    blk0 = st * (n_pages * per_page)
    key_blk = blk0 + _div(_iota((n_blk_pad, n_pages * PAGE_SIZE), 1), SLC_BLOCK)
    expand = jnp.where(_iota((n_blk_pad, n_pages * PAGE_SIZE), 0) == key_blk, 1.0, 0.0).astype(BF16)
    s = jnp.concatenate([jnp.dot(qbdt, pages[j][0].astype(BF16), preferred_element_type=F32)
                         for j in range(n_pages)], axis=1)
    s = s + jnp.dot(bias_b, expand, preferred_element_type=F32)

    def pv_pages(p):
        out = None
        for j in range(n_pages):
            part = lax.dot_general(p[:, j * PAGE_SIZE:(j + 1) * PAGE_SIZE], pages[j][1].astype(BF16), NT_DIMS,
                                   preferred_element_type=F32)
            out = part if out is None else out + part
        return out

    update(s, pv_pages)

    @pl.when(st == n_steps - 1)
    def _():
        def pad_rows(a):
            return jnp.concatenate([a, jnp.zeros((LANE - a.shape[0], a.shape[1]), a.dtype)], axis=0)

        key_i = _iota((1, LANE), 1)
        new = pad_rows(newkv_ref[0])
        s_n = lax.dot_general(qbdt, new[:, 2 * KV_DIM:3 * KV_DIM].astype(BF16), NT_DIMS, preferred_element_type=F32)
        s_n = s_n + bias_scr[:, n_past_blk:n_past_blk + 1]
        s_n = jnp.where((key_i <= tok_c) & (key_i < dec_len), s_n, NEG_INF)
        v_n = new[:, 3 * KV_DIM:].astype(BF16)
        update(s_n, lambda p: jnp.dot(p, v_n, preferred_element_type=F32))
        o_s = acc_scr[...] / jnp.maximum(l_scr[...], 1e-30)

        w_buf = win_ref.shape[2]
        neww = pad_rows(newwin_ref[0])
        s_w = jnp.concatenate(
            [jnp.dot(qbdt, win_ref[0].astype(BF16), preferred_element_type=F32),
             lax.dot_general(qbdt, neww[:, :KV_DIM].astype(BF16), NT_DIMS, preferred_element_type=F32)], axis=1)
        idx = _iota((1, w_buf + LANE), 1)
        kpos = past_len - w_buf + idx
        dist = past_len + tok_c - kpos
        mask_w = (dist >= 0) & (dist < WINDOW) & (idx < w_buf + dec_len) & (kpos >= 0)
        s_w = jnp.where(mask_w, s_w, NEG_INF)
        e_w = jnp.where(mask_w, jnp.exp(s_w - jnp.max(s_w, axis=1, keepdims=True)), 0.0)
        p_w = (e_w / jnp.maximum(jnp.sum(e_w, axis=1, keepdims=True), 1e-30)).astype(BF16)
        o_w = (lax.dot_general(p_w[:, :w_buf], win_ref[1].astype(BF16), NT_DIMS, preferred_element_type=F32)
               + jnp.dot(p_w[:, w_buf:], neww[:, KV_DIM:].astype(BF16), preferred_element_type=F32))
        gate = gate_ref[0]
        o_ref[0] = gate[:, 0:1] * oc_scr[...] + gate[:, 1:2] * o_s + gate[:, 2:3] * o_w


def _attn_sample(cache_t, page_table, page_index, win_t, win_index, qbd, qbdt, kc_flat, vc_flat, newkv8, newwin8,
                 gates, *, n_pages, past_len, dec_len):
    n_seq, pages_per_seq = page_table.shape
    n_steps = pages_per_seq // n_pages
    n_cmp = kc_flat.shape[1]
    n_blk_pad = -(-(past_len // SLC_BLOCK + 1) // LANE) * LANE
    w_buf = win_t.shape[2]

    def page_spec(j):
        return pl.BlockSpec((2, KV_DIM, PAGE_SIZE), lambda b, s, pt: (page_index(pt[b, s * n_pages + j]), 0, 0))

    per_seq = lambda shape: pl.BlockSpec((1,) + shape, lambda b, s, pt: (b,) + (0,) * len(shape))
    grid_spec = pltpu.PrefetchScalarGridSpec(
        num_scalar_prefetch=1,
        grid=(n_seq, n_steps),
        in_specs=[page_spec(j) for j in range(n_pages)] + [
            per_seq((KV_DIM, LANE)), per_seq((LANE, KV_DIM)), per_seq((n_cmp, KV_DIM)), per_seq((n_cmp, KV_DIM)),
            per_seq((SUBLANE, 4 * KV_DIM)),
            pl.BlockSpec((2, KV_DIM, w_buf), lambda b, s, pt: (win_index(b), 0, 0)),
            per_seq((SUBLANE, 2 * KV_DIM)), per_seq((LANE, LANE))],
        out_specs=per_seq((LANE, KV_DIM)),
        scratch_shapes=[pltpu.VMEM((n_blk_pad, LANE), F32), pltpu.VMEM((LANE, n_blk_pad), F32),
                        pltpu.VMEM((LANE, 1), F32), pltpu.VMEM((LANE, 1), F32),
                        pltpu.VMEM((LANE, KV_DIM), F32), pltpu.VMEM((LANE, KV_DIM), F32)],
    )
    return pl.pallas_call(
        functools.partial(_attn_sample_kernel, n_pages=n_pages, past_len=past_len, dec_len=dec_len,
                          n_blk_pad=n_blk_pad),
        grid_spec=grid_spec,
        out_shape=jax.ShapeDtypeStruct((n_seq, LANE, KV_DIM), F32),
        compiler_params=_params(2),
        name="attention_sample",
    )(page_table, *([cache_t] * n_pages), qbd, qbdt, kc_flat, vc_flat, newkv8, win_t, newwin8, gates)


def _permute_w_in(w):
    d = w.shape[0]
    c = np.cumsum([0, 4 * CONV_DIM, ATT_DIM, 4 * KV_DIM, 2 * KV_DIM, 3 * N_HEADS, ATT_DIM, 2 * D_MODEL])
    wa, wq, wkv, wwin, wgb, wzb, wgm = (w[:, c[i]:c[i + 1]] for i in range(7))
    wgb = jnp.pad(wgb.reshape(d, N_KV, 3 * GROUP), ((0, 0), (0, 0), (0, LANE - 3 * GROUP))).reshape(d, N_KV * LANE)
    return jnp.concatenate([wa, wq, wkv, wwin, wzb, wgm, wgb], axis=1).astype(BF16)


def _rope_tables(pos):
    half = HEAD_DIM // 2
    inv = jnp.power(ROPE_THETA, -jnp.arange(half, dtype=F32) / half)
    ang = pos.astype(F32)[:, None] * inv[None, :]
    cos, sin = jnp.cos(ang), jnp.sin(ang)
    reps = LANE // HEAD_DIM
    return jnp.tile(jnp.concatenate([cos, cos], axis=1), (1, reps)), jnp.tile(jnp.concatenate([-sin, sin], axis=1), (1, reps))


def _compress_weights(pos_k, w1_k, w2_k, pos_v, w1_v, w2_v, grouped):
    flat = CMP_STRIDE * HEAD_DIM
    pos2 = jnp.stack([jnp.pad(p.reshape(2, flat), ((0, SUBLANE - 2), (0, 0))) for p in (pos_k, pos_v)])
    w1 = jnp.stack([w.reshape(2, flat, CMP_HID) for w in (w1_k, w1_v)]).astype(BF16)
    if grouped:
        w2 = jnp.stack([jnp.pad(w2_k, ((0, 0), (0, CMP_HID - HEAD_DIM)))[None],
                        jnp.pad(w2_v.T, ((0, CMP_HID - HEAD_DIM), (0, 0)))[None]])
    else:
        w2 = jnp.stack([jnp.stack([jnp.pad(w, ((0, 0), (g * HEAD_DIM, KV_DIM - (g + 1) * HEAD_DIM)))
                                   for g in range(N_KV)]) for w in (w2_k, w2_v)])
    return pos2, w1, w2.astype(BF16)


def kernel(x_prompt, x_sample, cache_kv, state_win, state_conv, page_table, p_prompt, p_sample, ln_g, w_in, conv_w,
           cmp_pos_k, cmp_w1_k, cmp_w2_k, cmp_pos_v, cmp_w1_v, cmp_w2_v, w_out_a, w_out_b, w_o, w_ple, w_ple_gate,
           final_g):
    depth = ln_g.shape[0]
    n_b, seq_len, _ = x_prompt.shape
    n_dec, dec_len, _ = x_sample.shape
    pages_per_seq = page_table.shape[1]
    past_len = pages_per_seq * PAGE_SIZE
    n_pool = cache_kv.shape[1]
    w_buf = state_win.shape[2]

    tm_p = min(256, seq_len)
    tq = min(256, seq_len)
    kchunk = min(512, seq_len)
    tm_merge = min(512, seq_len)
    pages_p = min(16, seq_len // PAGE_SIZE)
    pages_s = min(16, pages_per_seq)
    rows_s = n_dec * dec_len
    carry_p = SUBLANE
    carry_s = (CONV_W - 1) * n_dec
    assert seq_len % PAGE_SIZE == 0 and seq_len // SLC_BLOCK <= MAX_BLOCKS and seq_len >= WINDOW + tq
    assert CONV_W - 1 <= dec_len <= SUBLANE and rows_s % SUBLANE == 0
    assert w_buf == min(WINDOW, past_len) and N_HEADS * SUBLANE == LANE

    cos_p, sin_p = _rope_tables(jnp.arange(seq_len, dtype=jnp.int32))
    pos_s = past_len + jnp.repeat(jnp.arange(dec_len, dtype=jnp.int32), n_dec)
    cos_s, sin_s = _rope_tables(pos_s)
    prompt_pages = jnp.arange(n_b * (seq_len // PAGE_SIZE), dtype=jnp.int32).reshape(n_b, seq_len // PAGE_SIZE)
    eye_g = jnp.eye(N_KV, dtype=F32)
    tpad = SUBLANE - dec_len
    cache_t = cache_kv.transpose(0, 1, 3, 4, 5, 2).reshape(depth * n_pool * 4, KV_DIM, PAGE_SIZE)
    win_t = state_win.transpose(0, 1, 3, 4, 5, 2).reshape(depth * n_dec * 2, KV_DIM, w_buf)

    xp = x_prompt.reshape(n_b * seq_len, D_MODEL)
    xs = x_sample.transpose(1, 0, 2).reshape(rows_s, D_MODEL)
    kv_p, kv_s, win_p, win_s, conv_p, conv_s = [], [], [], [], [], []
    for i in range(depth):
        last = i == depth - 1
        w_perm = _permute_w_in(w_in[i])
        g_row = ln_g[i].reshape(1, D_MODEL)
        cw8 = jnp.pad(conv_w[i], ((0, SUBLANE - CONV_W), (0, 0)))
        merge_w = (w_out_a[i].astype(BF16), w_out_b[i].astype(BF16), w_o[i].astype(BF16), w_ple[i].astype(BF16),
                   w_ple_gate[i].astype(BF16), final_g.reshape(1, D_MODEL))
        cmp_args = (cmp_pos_k[i], cmp_w1_k[i], cmp_w2_k[i], cmp_pos_v[i], cmp_w1_v[i], cmp_w2_v[i])

        (a_act, newkv, win, zb_act, gmix, tail, q_t, ks_ext, vs_t, kw_ext, vw_t, gb_t) = _project(
            xp, g_row, w_perm, cos_p, sin_p, jnp.zeros((n_b, carry_p, CONV_DIM), F32), cw8,
            n_seq=n_b, rows_per_seq=seq_len, tm=tm_p, stride=1, carry=carry_p, transposed=True, pos_per_seq=True)
        kc_ext, vc_t = _compress(newkv.reshape(n_b * seq_len * 4 * KV_SLABS, LANE), prompt_pages, lambda p: p,
                                 *_compress_weights(*cmp_args, True), n_pages=pages_p, prompt=True)
        y_act = _attn_prompt(q_t, gb_t, zb_act.reshape(n_b, seq_len, ATT_DIM), kc_ext, vc_t, ks_ext, vs_t, kw_ext,
                             vw_t, tq=tq, kchunk=kchunk)
        xp = _merge(xp, a_act, y_act.reshape(n_b * seq_len, ATT_DIM), None, gmix,
                    p_prompt[i].reshape(n_b * seq_len, PLE_DIM), *merge_w, tm=tm_merge, final=last)
        kv_p.append(newkv.reshape(n_b, seq_len, 4, N_KV, HEAD_DIM))
        win_p.append(win.reshape(n_b, seq_len, 2, N_KV, HEAD_DIM)[:, seq_len - min(WINDOW, seq_len):])
        conv_p.append(tail[:, carry_p - (CONV_W - 1):])

        prefix_s = state_conv[i].transpose(1, 0, 2).reshape(1, carry_s, CONV_DIM)
        (a_act, newkv, win, zb_act, gmix, tail, qp, gb) = _project(
            xs, g_row, w_perm, cos_s, sin_s, prefix_s, cw8,
            n_seq=1, rows_per_seq=rows_s, tm=rows_s, stride=n_dec, carry=carry_s, transposed=False, pos_per_seq=False)
        base = i * n_pool
        kc_flat, vc_flat = _compress(cache_t, page_table, lambda p: (base + p) * 2,
                                     *_compress_weights(*cmp_args, False), n_pages=pages_s, prompt=False)
        q5 = jnp.pad(qp.reshape(dec_len, n_dec, N_KV, GROUP, HEAD_DIM).astype(F32),
                     ((0, tpad), (0, 0), (0, 0), (0, 0), (0, 0)))
        qbd = q5.transpose(1, 4, 2, 3, 0)[:, None] * eye_g[None, :, None, :, None, None]
        qbd = qbd.reshape(n_dec, KV_DIM, LANE).astype(BF16)
        qbdt = q5.transpose(1, 2, 3, 0, 4)[:, :, :, :, None] * eye_g[None, :, None, None, :, None]
        qbdt = qbdt.reshape(n_dec, LANE, KV_DIM).astype(BF16)
        gates = gb.reshape(dec_len, n_dec, N_KV, LANE)[..., :3 * GROUP].reshape(dec_len, n_dec, N_KV, GROUP, 3)
        gates = jnp.pad(gates.transpose(1, 2, 3, 0, 4), ((0, 0), (0, 0), (0, 0), (0, tpad), (0, LANE - 3)))
        gates = gates.reshape(n_dec, LANE, LANE)
        newkv_b = newkv.reshape(dec_len, n_dec, 4 * KV_DIM).transpose(1, 0, 2)
        newwin_b = win.reshape(dec_len, n_dec, 2 * KV_DIM).transpose(1, 0, 2)
        pad_rows = lambda a: jnp.pad(a, ((0, 0), (0, tpad), (0, 0)))
        o_all = _attn_sample(cache_t, page_table, lambda p: (base + p) * 2 + 1, win_t, lambda b: i * n_dec + b, qbd,
                             qbdt, kc_flat, vc_flat, pad_rows(newkv_b), pad_rows(newwin_b), gates, n_pages=pages_s,
                             past_len=past_len, dec_len=dec_len)
        o6 = o_all.reshape(n_dec, N_KV, GROUP, SUBLANE, N_KV, HEAD_DIM)
        yb = jnp.stack([o6[:, g, :, :, g] for g in range(N_KV)], axis=1)
        yb = yb.transpose(3, 0, 1, 2, 4)[:dec_len].reshape(rows_s, ATT_DIM)
        xs = _merge(xs, a_act, yb, zb_act, gmix, p_sample[i].transpose(1, 0, 2).reshape(rows_s, PLE_DIM), *merge_w,
                    tm=rows_s, final=last)
        kv_s.append(newkv_b.reshape(n_dec, dec_len, 4, N_KV, HEAD_DIM))
        win_s.append(jnp.concatenate([state_win[i], newwin_b.reshape(n_dec, dec_len, 2, N_KV, HEAD_DIM)],
                                     axis=1)[:, -w_buf:])
        conv_s.append(tail.reshape(CONV_W - 1, n_dec, CONV_DIM).transpose(1, 0, 2))

    y_prompt = xp.reshape(n_b, seq_len, D_MODEL)
    y_sample = xs.reshape(dec_len, n_dec, D_MODEL).transpose(1, 0, 2)
    return (y_prompt, y_sample, jnp.stack(kv_p), jnp.stack(kv_s), jnp.stack(win_p), jnp.stack(win_s),
            jnp.stack(conv_p), jnp.stack(conv_s))
```

```python
import functools

import numpy as np
import jax
import jax.numpy as jnp
from jax import lax
from jax.experimental import pallas as pl
from jax.experimental.pallas import tpu as pltpu

D_MODEL = 1024
CONV_DIM = D_MODEL
CONV_W = 3
N_HEADS = 16
HEAD_DIM = 64
N_KV = 4
GROUP = N_HEADS // N_KV
ATT_DIM = N_HEADS * HEAD_DIM
KV_DIM = N_KV * HEAD_DIM
PAGE_SIZE = 128
CMP_STRIDE = 16
CMP_BLOCK = 2 * CMP_STRIDE
CMP_HID = 4 * HEAD_DIM
SLC_BLOCK = 64
TOP_N = 16
WINDOW = 512
PLE_DIM = 256
ROPE_THETA = 10000.0
RMS_EPS = 1e-6
NEG_INF = -1e30
FORCE_SCORE = 1e6

LANE = 128
SUBLANE = 8
VMEM_LIMIT = 56 * 1024 * 1024
KV_SLABS = KV_DIM // LANE
N_GATE_ROWS = 2 * SUBLANE
CHAIN_COLS = 4 * LANE

OFF_A = 0
OFF_Q = 4 * CONV_DIM
OFF_KV = OFF_Q + ATT_DIM
OFF_WIN = OFF_KV + 4 * KV_DIM
OFF_ZB = OFF_WIN + 2 * KV_DIM
OFF_GM = OFF_ZB + ATT_DIM
OFF_GB = OFF_GM + 2 * D_MODEL
N_PROJ = OFF_GB + N_KV * LANE
MAX_BLOCKS = LANE - HEAD_DIM

F32 = jnp.float32
BF16 = jnp.bfloat16
NT_DIMS = (((1,), (1,)), ((), ()))


def _params(n_axes):
    return pltpu.CompilerParams(dimension_semantics=("arbitrary",) * n_axes, vmem_limit_bytes=VMEM_LIMIT)


def _const_spec(shape):
    zeros = (0,) * len(shape)
    return pl.BlockSpec(shape, lambda *_: zeros, pipeline_mode=pl.Buffered(1))


def _sigmoid(x):
    return 1.0 / (1.0 + jnp.exp(-x))


def _div(x, d):
    assert d & (d - 1) == 0
    return jnp.right_shift(x, d.bit_length() - 1)


def _mod(x, d):
    assert d & (d - 1) == 0
    return jnp.bitwise_and(x, d - 1)


def _iota(shape, dim):
    return lax.broadcasted_iota(jnp.int32, shape, dim)


def _rope_slab(x, cos, sin, lane):
    swapped = jnp.where(_mod(lane, HEAD_DIM) < HEAD_DIM // 2, pltpu.roll(x, LANE - HEAD_DIM // 2, 1),
                        pltpu.roll(x, HEAD_DIM // 2, 1))
    return x * cos + swapped * sin


def _overlap_t(n_blk, n_cmp):
    bj = _iota((n_blk, n_cmp), 0)
    bm = _iota((n_blk, n_cmp), 1)
    ratio = SLC_BLOCK // CMP_STRIDE
    return jnp.where((bm >= ratio * bj) & (bm <= ratio * bj + ratio) & (bm >= 1), 1.0, 0.0).astype(BF16)


def _proj_kernel(x_ref, g_ref, w_ref, cos_ref, sin_ref, pre_ref, cw_ref,
                 a_ref, kv_ref, win_ref, zb_ref, gm_ref, tail_ref, *rest, tm, stride, carry, transposed):
    if transposed:
        qt_ref, ks_ref, vst_ref, kw_ref, vwt_ref, gbt_ref, u_scr = rest
    else:
        q_ref, gb_ref, u_scr = rest
    t = pl.program_id(1)
    x = x_ref[...]
    ms = jnp.mean(x * x, axis=-1, keepdims=True)
    h = (x * lax.rsqrt(ms + RMS_EPS) * g_ref[...]).astype(BF16)

    def seg(off, width):
        return jnp.dot(h, w_ref[:, off:off + width], preferred_element_type=F32)

    @pl.when(t == 0)
    def _():
        u_scr[0:carry, :] = pre_ref[0]

    u = seg(OFF_A + CONV_DIM, CONV_DIM) * seg(OFF_A + 2 * CONV_DIM, CONV_DIM)
    u_scr[carry:carry + tm, :] = u
    cw = cw_ref[...]
    conv = cw[0:1] * u_scr[carry - 2 * stride:carry - 2 * stride + tm, :]
    conv = conv + cw[1:2] * u_scr[carry - stride:carry - stride + tm, :]
    conv = conv + cw[2:3] * u
    ba = seg(OFF_A, CONV_DIM)
    za = seg(OFF_A + 3 * CONV_DIM, CONV_DIM)
    a_ref[...] = ((za * _sigmoid(za)) * (ba * conv)).astype(BF16)
    tail = u_scr[tm:tm + carry, :]
    tail_ref[0] = tail
    u_scr[0:carry, :] = tail

    lane = _iota((tm, LANE), 1)
    cos = cos_ref[...]
    sin = sin_ref[...]
    scale = HEAD_DIM ** -0.5 * (float(np.log2(np.e)) if transposed else 1.0)

    zq = seg(OFF_Q, ATT_DIM)
    for s in range(ATT_DIM // LANE):
        qs = _rope_slab(zq[:, s * LANE:(s + 1) * LANE], cos, sin, lane) * scale
        if transposed:
            qst = qs.T.astype(BF16)
            qt_ref[0, 2 * s] = qst[:HEAD_DIM]
            qt_ref[0, 2 * s + 1] = qst[HEAD_DIM:]
        else:
            q_ref[:, s * LANE:(s + 1) * LANE] = qs.astype(BF16)

    zkv = seg(OFF_KV, 4 * KV_DIM)
    zwin = seg(OFF_WIN, 2 * KV_DIM)
    kv_slabs = []
    for s in range(4 * KV_SLABS):
        v = zkv[:, s * LANE:(s + 1) * LANE]
        if (s // KV_SLABS) % 2 == 0:
            v = _rope_slab(v, cos, sin, lane)
        kv_ref[:, s * LANE:(s + 1) * LANE] = v
        kv_slabs.append(v)
    win_slabs = []
    for s in range(2 * KV_SLABS):
        v = zwin[:, s * LANE:(s + 1) * LANE]
        if s // KV_SLABS == 0:
            v = _rope_slab(v, cos, sin, lane)
        win_ref[:, s * LANE:(s + 1) * LANE] = v
        win_slabs.append(v)

    zb = seg(OFF_ZB, ATT_DIM)
    zb_ref[...] = zb * _sigmoid(zb)
    gm_ref[...] = _sigmoid(seg(OFF_GM, 2 * D_MODEL))
    gb = _sigmoid(seg(OFF_GB, N_KV * LANE))

    if not transposed:
        gb_ref[...] = gb
        return

    row = _iota((tm, LANE), 0)
    onehot = jnp.where(lane - HEAD_DIM == _div(t * tm + row, SLC_BLOCK), 1.0, 0.0)
    low = lane < HEAD_DIM
    ones_row = jnp.where(_iota((HEAD_DIM, tm), 0) == 0, 1.0, 0.0).astype(BF16)
    vs_t = [kv_slabs[3 * KV_SLABS + s].T.astype(BF16) for s in range(KV_SLABS)]
    vw_t = [win_slabs[KV_SLABS + s].T.astype(BF16) for s in range(KV_SLABS)]
    for g in range(N_KV):
        s, hi = divmod(g, 2)

        def half(v):
            return v if hi == 0 else pltpu.roll(v, HEAD_DIM, 1)

        ks_ref[0, g] = jnp.where(low, half(kv_slabs[2 * KV_SLABS + s]), onehot).astype(BF16)
        kw_ref[0, g] = jnp.where(low, half(win_slabs[s]), 0.0).astype(BF16)
        vst_ref[0, g] = jnp.concatenate([vs_t[s][hi * HEAD_DIM:(hi + 1) * HEAD_DIM], ones_row], axis=0)
        vwt_ref[0, g] = jnp.concatenate([vw_t[s][hi * HEAD_DIM:(hi + 1) * HEAD_DIM], ones_row], axis=0)
        gbt_ref[0, g] = gb[:, g * LANE:(g + 1) * LANE].T[:N_GATE_ROWS]


def _project(x2d, ln_g, w_perm, cos, sin, prefix, conv_w8, *, n_seq, rows_per_seq, tm, stride, carry, transposed,
             pos_per_seq):
    nt = rows_per_seq // tm
    n_rows = n_seq * rows_per_seq
    row_map = lambda b, t: (b * nt + t, 0)
    pos_map = (lambda b, t: (t, 0)) if pos_per_seq else row_map
    in_specs = [
        pl.BlockSpec((tm, D_MODEL), row_map),
        _const_spec((1, D_MODEL)),
        _const_spec((D_MODEL, N_PROJ)),
        pl.BlockSpec((tm, LANE), pos_map),
        pl.BlockSpec((tm, LANE), pos_map),
        pl.BlockSpec((1, carry, CONV_DIM), lambda b, t: (b, 0, 0)),
        _const_spec((SUBLANE, CONV_DIM)),
    ]
    rows = lambda width, dtype: (jax.ShapeDtypeStruct((n_rows, width), dtype), pl.BlockSpec((tm, width), row_map))
    outs = [
        rows(CONV_DIM, BF16),
        rows(4 * KV_DIM, F32),
        rows(2 * KV_DIM, F32),
        rows(ATT_DIM, F32),
        rows(2 * D_MODEL, F32),
        (jax.ShapeDtypeStruct((n_seq, carry, CONV_DIM), F32),
         pl.BlockSpec((1, carry, CONV_DIM), lambda b, t: (b, 0, 0))),
    ]
    if transposed:
        tok_sub = (jax.ShapeDtypeStruct((n_seq, N_KV, rows_per_seq, LANE), BF16),
                   pl.BlockSpec((1, N_KV, tm, LANE), lambda b, t: (b, 0, t, 0)))
        tok_lane = lambda n, rows_, dtype: (jax.ShapeDtypeStruct((n_seq, n, rows_, rows_per_seq), dtype),
                                            pl.BlockSpec((1, n, rows_, tm), lambda b, t: (b, 0, 0, t)))
        outs += [tok_lane(N_HEADS, HEAD_DIM, BF16),
                 tok_sub, tok_lane(N_KV, LANE, BF16),
                 tok_sub, tok_lane(N_KV, LANE, BF16),
                 tok_lane(N_KV, N_GATE_ROWS, F32)]
    else:
        outs += [rows(ATT_DIM, BF16), rows(N_KV * LANE, F32)]
    return pl.pallas_call(
        functools.partial(_proj_kernel, tm=tm, stride=stride, carry=carry, transposed=transposed),
        grid=(n_seq, nt),
        in_specs=in_specs,
        out_specs=[o[1] for o in outs],
        out_shape=[o[0] for o in outs],
        scratch_shapes=[pltpu.VMEM((carry + tm, CONV_DIM), F32)],
        compiler_params=_params(2),
        name="project_prompt" if transposed else "project_sample",
    )(x2d, ln_g, w_perm, cos, sin, prefix, conv_w8)


def _merge_kernel(x_ref, a_ref, gm_ref, p_ref, woa_ref, wob_ref, wo_ref, wple_ref, wpg_ref, fg_ref, y_ref, *rest,
                  y_needs_gate, final):
    br_a = jnp.dot(a_ref[...], woa_ref[...], preferred_element_type=F32)
    if y_needs_gate:
        zb_ref, o_ref = rest
        y = (zb_ref[...] * y_ref[...]).astype(BF16)
    else:
        (o_ref,) = rest
        y = y_ref[...]
    br_b = jnp.dot(y, wob_ref[...], preferred_element_type=F32)
    gm = gm_ref[...]
    mix = gm[:, :D_MODEL] * br_a + gm[:, D_MODEL:] * br_b
    x1 = x_ref[...] + jnp.dot(mix.astype(BF16), wo_ref[...], preferred_element_type=F32)
    gate = _sigmoid(jnp.dot(x1.astype(BF16), wpg_ref[...], preferred_element_type=F32))
    x2 = x1 + gate * jnp.dot(p_ref[...].astype(BF16), wple_ref[...], preferred_element_type=F32)
    if final:
        ms = jnp.mean(x2 * x2, axis=-1, keepdims=True)
        x2 = x2 * lax.rsqrt(ms + RMS_EPS) * fg_ref[...]
    o_ref[...] = x2


def _merge(x2d, a_act, y, zb_act, gmix, p2d, w_out_a, w_out_b, w_o, w_ple, w_pg, final_g, *, tm, final):
    n_rows = x2d.shape[0]
    row = lambda width: pl.BlockSpec((tm, width), lambda t: (t, 0))
    y_args = (y,) if zb_act is None else (y, zb_act)
    return pl.pallas_call(
        functools.partial(_merge_kernel, y_needs_gate=zb_act is not None, final=final),
        grid=(n_rows // tm,),
        in_specs=[row(D_MODEL), row(CONV_DIM), row(2 * D_MODEL), row(PLE_DIM),
                  _const_spec((CONV_DIM, D_MODEL)), _const_spec((ATT_DIM, D_MODEL)),
                  _const_spec((D_MODEL, D_MODEL)), _const_spec((PLE_DIM, D_MODEL)),
                  _const_spec((D_MODEL, D_MODEL)), _const_spec((1, D_MODEL))] + [row(ATT_DIM)] * len(y_args),
        out_specs=row(D_MODEL),
        out_shape=jax.ShapeDtypeStruct((n_rows, D_MODEL), F32),
        compiler_params=_params(1),
        name="merge_prompt" if zb_act is None else "merge_sample",
    )(x2d, a_act, gmix, p2d, w_out_a, w_out_b, w_o, w_ple, w_pg, final_g, *y_args)


def _compress_kernel(pt_ref, *refs, n_pages, prompt):
    pages = refs[:n_pages]
    pos_ref, w1_ref, w2_ref, ko_ref, vo_ref, rows_scr, x_scr, carry_scr = refs[n_pages:]
    st = pl.program_id(1)
    per_page = PAGE_SIZE // CMP_STRIDE
    n = n_pages * per_page
    low = _iota((n, LANE), 1) < HEAD_DIM

    @pl.when(st == 0)
    def _():
        carry_scr[...] = jnp.zeros_like(carry_scr)

    row0 = _iota((n, CMP_HID), 0) == 0
    for slot, out_ref in enumerate((ko_ref, vo_ref)):
        for j in range(n_pages):
            for v in range(KV_SLABS):
                slab = slot * KV_SLABS + v
                if prompt:
                    piece = pages[j][pl.ds(slab, PAGE_SIZE, stride=4 * KV_SLABS), :]
                else:
                    piece = pages[j][slot, v * LANE:(v + 1) * LANE, :].T
                rows_scr[v, j * PAGE_SIZE:(j + 1) * PAGE_SIZE, :] = piece
        for pair in range(CMP_STRIDE // 2):
            for v in range(KV_SLABS):
                a = rows_scr[v, pl.ds(2 * pair, n, stride=CMP_STRIDE), :]
                b = rows_scr[v, pl.ds(2 * pair + 1, n, stride=CMP_STRIDE), :]
                x_scr[2 * v * n:(2 * v + 1) * n, pair * LANE:(pair + 1) * LANE] = jnp.where(
                    low, a, pltpu.roll(b, HEAD_DIM, 1))
                x_scr[(2 * v + 1) * n:(2 * v + 2) * n, pair * LANE:(pair + 1) * LANE] = jnp.where(
                    low, pltpu.roll(a, HEAD_DIM, 1), b)
        xs = x_scr[...]
        pos = pos_ref[slot]
        f = jnp.dot((xs + pos[0:1]).astype(BF16), w1_ref[slot, 0], preferred_element_type=F32)
        s = jnp.dot((xs + pos[1:2]).astype(BF16), w1_ref[slot, 1], preferred_element_type=F32)
        acc = None
        for g in range(N_KV):
            fg = f[g * n:(g + 1) * n]
            prev = carry_scr[slot, g, SUBLANE - 1:SUBLANE, :]
            shifted = jnp.where(row0, prev, pltpu.roll(fg, 1, 0))
            carry_scr[slot, g] = fg[n - SUBLANE:n]
            pre = shifted + s[g * n:(g + 1) * n]
            hid = (pre * _sigmoid(pre)).astype(BF16)
            if prompt and slot == 0:
                out_ref[0, g] = jnp.dot(hid, w2_ref[0, 0, :, :LANE], preferred_element_type=F32).astype(BF16)
            elif prompt:
                out_ref[0, g] = lax.dot_general(w2_ref[1, 0, :LANE, :], hid, NT_DIMS,
                                                preferred_element_type=F32).astype(BF16)
            else:
                part = jnp.dot(hid, w2_ref[slot, g], preferred_element_type=F32)
                acc = part if acc is None else acc + part
        if not prompt:
            out_ref[0] = acc.astype(BF16)


def _compress(pages_arr, page_table, page_index, pos2, w1, w2, *, n_pages, prompt):
    n_seq, pages_per_seq = page_table.shape
    n_steps = pages_per_seq // n_pages
    n = n_pages * (PAGE_SIZE // CMP_STRIDE)
    n_chunk = pages_per_seq * (PAGE_SIZE // CMP_STRIDE)
    flat = CMP_STRIDE * HEAD_DIM

    def page_spec(j):
        if prompt:
            return pl.BlockSpec((PAGE_SIZE * 4 * KV_SLABS, LANE),
                                lambda b, s, pt: (page_index(pt[b, s * n_pages + j]), 0))
        return pl.BlockSpec((2, KV_DIM, PAGE_SIZE), lambda b, s, pt: (page_index(pt[b, s * n_pages + j]), 0, 0))

    if prompt:
        out_sds = [jax.ShapeDtypeStruct((n_seq, N_KV, n_chunk, LANE), BF16),
                   jax.ShapeDtypeStruct((n_seq, N_KV, LANE, n_chunk), BF16)]
        out_specs = [pl.BlockSpec((1, N_KV, n, LANE), lambda b, s, pt: (b, 0, s, 0)),
                     pl.BlockSpec((1, N_KV, LANE, n), lambda b, s, pt: (b, 0, 0, s))]
    else:
        out_sds = [jax.ShapeDtypeStruct((n_seq, n_chunk, KV_DIM), BF16)] * 2
        out_specs = [pl.BlockSpec((1, n, KV_DIM), lambda b, s, pt: (b, s, 0))] * 2
    const = lambda shape: pl.BlockSpec(shape, lambda b, s, pt: (0,) * len(shape), pipeline_mode=pl.Buffered(1))
    grid_spec = pltpu.PrefetchScalarGridSpec(
        num_scalar_prefetch=1,
        grid=(n_seq, n_steps),
        in_specs=[page_spec(j) for j in range(n_pages)] + [const(pos2.shape), const(w1.shape), const(w2.shape)],
        out_specs=out_specs,
        scratch_shapes=[pltpu.VMEM((KV_SLABS, n_pages * PAGE_SIZE, LANE), F32), pltpu.VMEM((N_KV * n, flat), F32),
                        pltpu.VMEM((2, N_KV, SUBLANE, CMP_HID), F32)],
    )
    return pl.pallas_call(
        functools.partial(_compress_kernel, n_pages=n_pages, prompt=prompt),
        grid_spec=grid_spec,
        out_shape=out_sds,
        compiler_params=_params(2),
        name="compress_prompt" if prompt else "compress_sample",
    )(page_table, *([pages_arr] * n_pages), pos2, w1, w2)


def _select_bias(score_ref, blk, n_real, n_sel):
    score = score_ref[...]

    def body(i, cnt):
        si = score_ref[pl.ds(i, 1), :]
        return cnt + jnp.where(blk > i, jnp.where(si >= score, 1.0, 0.0), jnp.where(si > score, 1.0, 0.0))

    cnt = lax.fori_loop(0, n_real, body, jnp.zeros(score.shape, F32))
    return jnp.where((cnt < n_sel) & (score >= 0.0), 0.0, NEG_INF)


def _attn_prompt_kernel(qt_ref, gbt_ref, zb_ref, tri_ref, kc_ref, vct_ref, ks_ref, vst_ref, kw_ref, vwt_ref, o_ref,
                        score_scr, m_scr, acc_scr, mx_scr, s_even_scr, s_odd_scr, *, tq, n_cmp, n_blk):
    qi = pl.program_id(2)
    s0 = qi * tq
    rows = GROUP * tq
    cw = min(rows, CHAIN_COLS)
    col_groups = [slice(c, c + cw) for c in range(0, rows, cw)]
    qpos = s0 + _mod(_iota((1, rows), 1), tq)

    def stack_q(lower):
        return jnp.concatenate([jnp.concatenate([qt_ref[0, r], lower], axis=0) for r in range(GROUP)], axis=1)

    q0 = stack_q(jnp.where(_iota((LANE - HEAD_DIM, tq), 0) == 0, NEG_INF, 0.0).astype(BF16))

    s_c = jnp.dot(kc_ref[0, 0], q0, preferred_element_type=F32)

    def window_scores(cs, first, n_keys):
        keys = kw_ref[0, 0, pl.ds(pl.multiple_of(s0 + first, tq), n_keys), :]
        return jnp.dot(keys, q0[:, cs], preferred_element_type=F32)

    w_parts = []
    for cs in col_groups:
        parts = [window_scores(cs, 0, tq) + tri_ref[0, :, cs]]
        if WINDOW > tq:
            parts.append(window_scores(cs, tq, WINDOW - tq))
        parts.append(window_scores(cs, WINDOW, tq) + tri_ref[1, :, cs])
        w_parts.append(parts)

    m_idx = _iota((n_cmp, 1), 0)
    mask_c = (m_idx >= 1) & (CMP_STRIDE * m_idx + CMP_STRIDE - 1 <= qpos)
    s_c = jnp.where(mask_c, s_c, NEG_INF)
    e_c = jnp.where(mask_c, jnp.exp2(s_c - jnp.max(s_c, axis=0, keepdims=True)), 0.0)
    p_c = e_c * (1.0 / jnp.maximum(jnp.sum(e_c, axis=0, keepdims=True), 1e-30))
    o_c = jnp.dot(vct_ref[0, 0], p_c.astype(BF16), preferred_element_type=F32)

    p_sum = p_c[:, 0:tq]
    for r in range(1, GROUP):
        p_sum = p_sum + p_c[:, r * tq:(r + 1) * tq]
    p_hi = p_sum.astype(BF16)
    p_lo = (p_sum - p_hi.astype(F32)).astype(BF16)
    ov_t = _overlap_t(MAX_BLOCKS, n_cmp)
    imp = jnp.dot(ov_t, p_hi, preferred_element_type=F32) + jnp.dot(ov_t, p_lo, preferred_element_type=F32)

    w_out = []
    for cs, parts in zip(col_groups, w_parts):
        m_w = functools.reduce(jnp.maximum, [jnp.max(s, axis=0, keepdims=True) for s in parts])
        p_w = jnp.concatenate([jnp.exp2(s - m_w).astype(BF16) for s in parts], axis=0)
        acc_w = jnp.dot(vwt_ref[0, 0, :, pl.ds(pl.multiple_of(s0, tq), WINDOW + tq)], p_w, preferred_element_type=F32)
        w_out.append(acc_w[:HEAD_DIM] / jnp.maximum(acc_w[HEAD_DIM:HEAD_DIM + 1], 1e-30))
    o_w = jnp.concatenate(w_out, axis=1)

    blk = _iota((MAX_BLOCKS, tq), 0)
    cur = _div(s0 + _iota((MAX_BLOCKS, tq), 1), SLC_BLOCK)
    forced = (blk == 0) | (blk == cur) | (blk == cur - 1)
    score = jnp.where(forced, FORCE_SCORE, imp)
    score_scr[...] = jnp.where(blk <= cur, score, -1.0)
    n_live = jnp.minimum(n_blk, (s0 + tq) // SLC_BLOCK)
    bias_t = _select_bias(score_scr, blk, n_live, min(TOP_N, n_blk))
    qx = stack_q(bias_t.astype(BF16))

    m_scr[...] = jnp.full(m_scr.shape, NEG_INF, F32)
    acc_scr[...] = jnp.zeros(acc_scr.shape, F32)
    s_bufs = (s_even_scr, s_odd_scr)

    def scores(k0, slot, causal):
        keys = ks_ref[0, 0, pl.ds(pl.multiple_of(k0, tq), tq), :]
        prods = [jnp.dot(keys, qx[:, cs], preferred_element_type=F32) for cs in col_groups]
        for cs, s in zip(col_groups, prods):
            if causal:
                s = s + tri_ref[1, :, cs]
            s_bufs[slot][:, cs] = s
            mx_scr[slot, 0:1, cs] = jnp.max(s, axis=0, keepdims=True)

    def accumulate(k0, slot):
        vals_t = vst_ref[0, 0, :, pl.ds(pl.multiple_of(k0, tq), tq)]
        for cs in col_groups:
            m_old = m_scr[0:1, cs]
            m_new = jnp.maximum(m_old, mx_scr[slot, 0:1, cs])
            p = jnp.exp2(s_bufs[slot][:, cs] - m_new).astype(BF16)
            acc_scr[:, cs] = jnp.exp2(m_old - m_new) * acc_scr[:, cs] + jnp.dot(vals_t, p, preferred_element_type=F32)
            m_scr[0:1, cs] = m_new

    n_pairs = qi // 2
    scores(s0, 0, True)

    def pair(j, _):
        scores(2 * j * tq, 1, False)
        accumulate(jnp.where(j == 0, s0, (2 * j - 1) * tq), 0)
        scores((2 * j + 1) * tq, 0, False)
        accumulate(2 * j * tq, 1)
        return 0

    lax.fori_loop(0, n_pairs, pair, 0)
    pending = jnp.where(n_pairs == 0, s0, (2 * n_pairs - 1) * tq)

    @pl.when(qi - 2 * n_pairs == 1)
    def _():
        scores(s0 - tq, 1, False)
        accumulate(pending, 0)
        accumulate(s0 - tq, 1)

    @pl.when(qi - 2 * n_pairs == 0)
    def _():
        accumulate(pending, 0)

    o_s = acc_scr[:HEAD_DIM, :] / jnp.maximum(acc_scr[HEAD_DIM:HEAD_DIM + 1, :], 1e-30)

    gb = gbt_ref[0, 0]
    zb = zb_ref[0]
    heads = []
    for r in range(GROUP):
        rs = slice(r * tq, (r + 1) * tq)
        heads.append(gb[3 * r:3 * r + 1] * o_c[:HEAD_DIM, rs] + gb[3 * r + 1:3 * r + 2] * o_s[:, rs]
                     + gb[3 * r + 2:3 * r + 3] * o_w[:, rs])
    for pair in range(GROUP // 2):
        slab = jnp.concatenate([heads[2 * pair], heads[2 * pair + 1]], axis=0).T
        o_ref[0, :, pair * LANE:(pair + 1) * LANE] = (zb[:, pair * LANE:(pair + 1) * LANE] * slab).astype(BF16)


def _attn_prompt(q_t, gb_t, zb_act, kc_ext, vc_t, ks_ext, vs_t, kw_pad, vw_pad, *, tq):
    n_b, seq_len = zb_act.shape[0], zb_act.shape[1]
    n_cmp = kc_ext.shape[2]
    n_blk = seq_len // SLC_BLOCK
    head_w = GROUP * HEAD_DIM
    rows = GROUP * tq
    assert WINDOW % tq == 0 and rows % min(rows, CHAIN_COLS) == 0
    off = _iota((tq, rows), 0) - _mod(_iota((tq, rows), 1), tq)
    tri = jnp.stack([jnp.where(off > 0, 0.0, NEG_INF), jnp.where(off <= 0, 0.0, NEG_INF)]).astype(F32)
    tok_sub = lambda n: pl.BlockSpec((1, 1, n, LANE), lambda b, g, i: (b, g, 0, 0))
    tok_lane = lambda n: pl.BlockSpec((1, 1, LANE, n), lambda b, g, i: (b, g, 0, 0))
    return pl.pallas_call(
        functools.partial(_attn_prompt_kernel, tq=tq, n_cmp=n_cmp, n_blk=n_blk),
        grid=(n_b, N_KV, seq_len // tq),
        in_specs=[pl.BlockSpec((1, GROUP, HEAD_DIM, tq), lambda b, g, i: (b, g, 0, i)),
                  pl.BlockSpec((1, 1, N_GATE_ROWS, tq), lambda b, g, i: (b, g, 0, i)),
                  pl.BlockSpec((1, tq, head_w), lambda b, g, i: (b, i, g)),
                  _const_spec((2, tq, rows)),
                  tok_sub(n_cmp), tok_lane(n_cmp), tok_sub(seq_len), tok_lane(seq_len),
                  tok_sub(seq_len + WINDOW), tok_lane(seq_len + WINDOW)],
        out_specs=pl.BlockSpec((1, tq, head_w), lambda b, g, i: (b, i, g)),
        out_shape=jax.ShapeDtypeStruct((n_b, seq_len, ATT_DIM), BF16),
        scratch_shapes=[pltpu.VMEM((MAX_BLOCKS, tq), F32), pltpu.VMEM((SUBLANE, rows), F32),
                        pltpu.VMEM((LANE, rows), F32), pltpu.VMEM((2, SUBLANE, rows), F32),
                        pltpu.VMEM((tq, rows), F32), pltpu.VMEM((tq, rows), F32)],
        compiler_params=_params(3),
        name="attention_prompt",
    )(q_t, gb_t, zb_act, tri, kc_ext, vc_t, ks_ext, vs_t, kw_pad, vw_pad)


def _attn_sample_kernel(pt_ref, *refs, n_pages, past_len, dec_len, n_blk_pad):
    pages = refs[:n_pages]
    (qbd_ref, qbdt_ref, kc_ref, vc_ref, newkv_ref, win_ref, newwin_ref, gate_ref, o_ref,
     score_scr, bias_scr, m_scr, l_scr, acc_scr, oc_scr) = refs[n_pages:]
    st = pl.program_id(1)
    n_steps = pl.num_programs(1)
    cols = LANE
    tslots = cols // N_HEADS
    n_cmp = kc_ref.shape[1]
    n_past_blk = past_len // SLC_BLOCK
    per_page = PAGE_SIZE // SLC_BLOCK
    qbdt = qbdt_ref[0]
    tok_c = _mod(_iota((cols, 1), 0), tslots)

    @pl.when(st == 0)
    def _():
        qpos = past_len + _mod(_iota((1, cols), 1), tslots)
        s_c = jnp.dot(kc_ref[0], qbd_ref[0], preferred_element_type=F32)
        m_idx = _iota((n_cmp, 1), 0)
        mask_c = (m_idx >= 1) & (CMP_STRIDE * m_idx + CMP_STRIDE - 1 <= qpos)
        s_c = jnp.where(mask_c, s_c, NEG_INF)
        e_c = jnp.where(mask_c, jnp.exp(s_c - jnp.max(s_c, axis=0, keepdims=True)), 0.0)
        p_c = e_c / jnp.maximum(jnp.sum(e_c, axis=0, keepdims=True), 1e-30)
        p_hi = p_c.astype(BF16)
        p_lo = (p_c - p_hi.astype(F32)).astype(BF16)
        oc_scr[...] = jnp.dot(p_c.T.astype(BF16), vc_ref[0], preferred_element_type=F32)
        ov_t = _overlap_t(n_blk_pad, n_cmp)
        imp1 = jnp.dot(ov_t, p_hi, preferred_element_type=F32) + jnp.dot(ov_t, p_lo, preferred_element_type=F32)
        ca = _iota((cols, cols), 0)
        cb = _iota((cols, cols), 1)
        per_group = GROUP * tslots
        same = jnp.where((_div(ca, per_group) == _div(cb, per_group)) & (_mod(ca, tslots) == _mod(cb, tslots)),
                         1.0, 0.0).astype(BF16)
        i_hi = imp1.astype(BF16)
        i_mid = (imp1 - i_hi.astype(F32)).astype(BF16)
        i_lo = (imp1 - i_hi.astype(F32) - i_mid.astype(F32)).astype(BF16)
        imp = (jnp.dot(i_hi, same, preferred_element_type=F32) + jnp.dot(i_mid, same, preferred_element_type=F32)
               + jnp.dot(i_lo, same, preferred_element_type=F32))
        blk = _iota((n_blk_pad, cols), 0)
        cur = _div(qpos, SLC_BLOCK)
        forced = (blk == 0) | (blk == cur) | (blk == cur - 1)
        score = jnp.where(forced, FORCE_SCORE, imp)
        score_scr[...] = jnp.where(blk <= cur, score, -1.0)
        n_blk = n_past_blk + 1
        bias_scr[...] = _select_bias(score_scr, blk, n_blk, min(TOP_N, n_blk)).T
        m_scr[...] = jnp.full(m_scr.shape, NEG_INF, F32)
        l_scr[...] = jnp.zeros(l_scr.shape, F32)
        acc_scr[...] = jnp.zeros(acc_scr.shape, F32)

    def update(s, pv_fn):
        m_old = m_scr[...]
        m_new = jnp.maximum(m_old, jnp.max(s, axis=1, keepdims=True))
        alpha = jnp.exp(m_old - m_new)
        p = jnp.exp(s - m_new)
        l_scr[...] = alpha * l_scr[...] + jnp.sum(p, axis=1, keepdims=True)
        acc_scr[...] = alpha * acc_scr[...] + pv_fn(p.astype(BF16))
        m_scr[...] = m_new

    bias_b = bias_scr[...].astype(BF16)
    blk0 = st * (n_pages * per_page)
    key_blk = blk0 + _div(_iota((n_blk_pad, n_pages * PAGE_SIZE), 1), SLC_BLOCK)
    expand = jnp.where(_iota((n_blk_pad, n_pages * PAGE_SIZE), 0) == key_blk, 1.0, 0.0).astype(BF16)
    s = jnp.concatenate([jnp.dot(qbdt, pages[j][0].astype(BF16), preferred_element_type=F32)
                         for j in range(n_pages)], axis=1)
    s = s + jnp.dot(bias_b, expand, preferred_element_type=F32)

    def pv_pages(p):
        out = None
        for j in range(n_pages):
            part = lax.dot_general(p[:, j * PAGE_SIZE:(j + 1) * PAGE_SIZE], pages[j][1].astype(BF16), NT_DIMS,
                                   preferred_element_type=F32)
            out = part if out is None else out + part
        return out

    update(s, pv_pages)

    @pl.when(st == n_steps - 1)
    def _():
        def pad_rows(a):
            return jnp.concatenate([a, jnp.zeros((LANE - a.shape[0], a.shape[1]), a.dtype)], axis=0)

        key_i = _iota((1, LANE), 1)
        new = pad_rows(newkv_ref[0])
        s_n = lax.dot_general(qbdt, new[:, 2 * KV_DIM:3 * KV_DIM].astype(BF16), NT_DIMS, preferred_element_type=F32)
        s_n = s_n + bias_scr[:, n_past_blk:n_past_blk + 1]
        s_n = jnp.where((key_i <= tok_c) & (key_i < dec_len), s_n, NEG_INF)
        v_n = new[:, 3 * KV_DIM:].astype(BF16)
        update(s_n, lambda p: jnp.dot(p, v_n, preferred_element_type=F32))
        o_s = acc_scr[...] / jnp.maximum(l_scr[...], 1e-30)

        w_buf = win_ref.shape[2]
        neww = pad_rows(newwin_ref[0])
        s_w = jnp.concatenate(
            [jnp.dot(qbdt, win_ref[0].astype(BF16), preferred_element_type=F32),
             lax.dot_general(qbdt, neww[:, :KV_DIM].astype(BF16), NT_DIMS, preferred_element_type=F32)], axis=1)
        idx = _iota((1, w_buf + LANE), 1)
        kpos = past_len - w_buf + idx
        dist = past_len + tok_c - kpos
        mask_w = (dist >= 0) & (dist < WINDOW) & (idx < w_buf + dec_len) & (kpos >= 0)
        s_w = jnp.where(mask_w, s_w, NEG_INF)
        e_w = jnp.where(mask_w, jnp.exp(s_w - jnp.max(s_w, axis=1, keepdims=True)), 0.0)
        p_w = (e_w / jnp.maximum(jnp.sum(e_w, axis=1, keepdims=True), 1e-30)).astype(BF16)
        o_w = (lax.dot_general(p_w[:, :w_buf], win_ref[1].astype(BF16), NT_DIMS, preferred_element_type=F32)
               + jnp.dot(p_w[:, w_buf:], neww[:, KV_DIM:].astype(BF16), preferred_element_type=F32))
        gate = gate_ref[0]
        o_ref[0] = gate[:, 0:1] * oc_scr[...] + gate[:, 1:2] * o_s + gate[:, 2:3] * o_w


def _attn_sample(cache_t, page_table, page_index, win_t, win_index, qbd, qbdt, kc_flat, vc_flat, newkv8, newwin8,
                 gates, *, n_pages, past_len, dec_len):
    n_seq, pages_per_seq = page_table.shape
    n_steps = pages_per_seq // n_pages
    n_cmp = kc_flat.shape[1]
    n_blk_pad = -(-(past_len // SLC_BLOCK + 1) // LANE) * LANE
    w_buf = win_t.shape[2]

    def page_spec(j):
        return pl.BlockSpec((2, KV_DIM, PAGE_SIZE), lambda b, s, pt: (page_index(pt[b, s * n_pages + j]), 0, 0))

    per_seq = lambda shape: pl.BlockSpec((1,) + shape, lambda b, s, pt: (b,) + (0,) * len(shape))
    grid_spec = pltpu.PrefetchScalarGridSpec(
        num_scalar_prefetch=1,
        grid=(n_seq, n_steps),
        in_specs=[page_spec(j) for j in range(n_pages)] + [
            per_seq((KV_DIM, LANE)), per_seq((LANE, KV_DIM)), per_seq((n_cmp, KV_DIM)), per_seq((n_cmp, KV_DIM)),
            per_seq((SUBLANE, 4 * KV_DIM)),
            pl.BlockSpec((2, KV_DIM, w_buf), lambda b, s, pt: (win_index(b), 0, 0)),
            per_seq((SUBLANE, 2 * KV_DIM)), per_seq((LANE, LANE))],
        out_specs=per_seq((LANE, KV_DIM)),
        scratch_shapes=[pltpu.VMEM((n_blk_pad, LANE), F32), pltpu.VMEM((LANE, n_blk_pad), F32),
                        pltpu.VMEM((LANE, 1), F32), pltpu.VMEM((LANE, 1), F32),
                        pltpu.VMEM((LANE, KV_DIM), F32), pltpu.VMEM((LANE, KV_DIM), F32)],
    )
    return pl.pallas_call(
        functools.partial(_attn_sample_kernel, n_pages=n_pages, past_len=past_len, dec_len=dec_len,
                          n_blk_pad=n_blk_pad),
        grid_spec=grid_spec,
        out_shape=jax.ShapeDtypeStruct((n_seq, LANE, KV_DIM), F32),
        compiler_params=_params(2),
        name="attention_sample",
    )(page_table, *([cache_t] * n_pages), qbd, qbdt, kc_flat, vc_flat, newkv8, win_t, newwin8, gates)


def _permute_w_in(w):
    d = w.shape[0]
    c = np.cumsum([0, 4 * CONV_DIM, ATT_DIM, 4 * KV_DIM, 2 * KV_DIM, 3 * N_HEADS, ATT_DIM, 2 * D_MODEL])
    wa, wq, wkv, wwin, wgb, wzb, wgm = (w[:, c[i]:c[i + 1]] for i in range(7))
    wgb = jnp.pad(wgb.reshape(d, N_KV, 3 * GROUP), ((0, 0), (0, 0), (0, LANE - 3 * GROUP))).reshape(d, N_KV * LANE)
    return jnp.concatenate([wa, wq, wkv, wwin, wzb, wgm, wgb], axis=1).astype(BF16)


def _rope_tables(pos):
    half = HEAD_DIM // 2
    inv = jnp.power(ROPE_THETA, -jnp.arange(half, dtype=F32) / half)
    ang = pos.astype(F32)[:, None] * inv[None, :]
    cos, sin = jnp.cos(ang), jnp.sin(ang)
    reps = LANE // HEAD_DIM
    return jnp.tile(jnp.concatenate([cos, cos], axis=1), (1, reps)), jnp.tile(jnp.concatenate([-sin, sin], axis=1), (1, reps))


def _compress_weights(pos_k, w1_k, w2_k, pos_v, w1_v, w2_v, grouped):
    flat = CMP_STRIDE * HEAD_DIM
    pos2 = jnp.stack([jnp.pad(p.reshape(2, flat), ((0, SUBLANE - 2), (0, 0))) for p in (pos_k, pos_v)])
    w1 = jnp.stack([w.reshape(2, flat, CMP_HID) for w in (w1_k, w1_v)]).astype(BF16)
    if grouped:
        w2 = jnp.stack([jnp.pad(w2_k, ((0, 0), (0, CMP_HID - HEAD_DIM)))[None],
                        jnp.pad(w2_v.T, ((0, CMP_HID - HEAD_DIM), (0, 0)))[None]])
    else:
        w2 = jnp.stack([jnp.stack([jnp.pad(w, ((0, 0), (g * HEAD_DIM, KV_DIM - (g + 1) * HEAD_DIM)))
                                   for g in range(N_KV)]) for w in (w2_k, w2_v)])
    return pos2, w1, w2.astype(BF16)


def kernel(x_prompt, x_sample, cache_kv, state_win, state_conv, page_table, p_prompt, p_sample, ln_g, w_in, conv_w,
           cmp_pos_k, cmp_w1_k, cmp_w2_k, cmp_pos_v, cmp_w1_v, cmp_w2_v, w_out_a, w_out_b, w_o, w_ple, w_ple_gate,
           final_g):
    depth = ln_g.shape[0]
    n_b, seq_len, _ = x_prompt.shape
    n_dec, dec_len, _ = x_sample.shape
    pages_per_seq = page_table.shape[1]
    past_len = pages_per_seq * PAGE_SIZE
    n_pool = cache_kv.shape[1]
    w_buf = state_win.shape[2]

    tm_p = min(256, seq_len)
    tq = min(256, seq_len)
    tm_merge = min(512, seq_len)
    pages_p = min(16, seq_len // PAGE_SIZE)
    pages_s = min(16, pages_per_seq)
    rows_s = n_dec * dec_len
    carry_p = SUBLANE
    carry_s = (CONV_W - 1) * n_dec
    assert seq_len % PAGE_SIZE == 0 and seq_len // SLC_BLOCK <= MAX_BLOCKS and seq_len >= WINDOW + tq
    assert CONV_W - 1 <= dec_len <= SUBLANE and rows_s % SUBLANE == 0
    assert w_buf == min(WINDOW, past_len) and N_HEADS * SUBLANE == LANE

    cos_p, sin_p = _rope_tables(jnp.arange(seq_len, dtype=jnp.int32))
    pos_s = past_len + jnp.repeat(jnp.arange(dec_len, dtype=jnp.int32), n_dec)
    cos_s, sin_s = _rope_tables(pos_s)
    prompt_pages = jnp.arange(n_b * (seq_len // PAGE_SIZE), dtype=jnp.int32).reshape(n_b, seq_len // PAGE_SIZE)
    eye_g = jnp.eye(N_KV, dtype=F32)
    tpad = SUBLANE - dec_len
    win_pad_keys = (jnp.arange(LANE) == HEAD_DIM).astype(BF16)
    cache_t = cache_kv.transpose(0, 1, 3, 4, 5, 2).reshape(depth * n_pool * 4, KV_DIM, PAGE_SIZE)
    win_t = state_win.transpose(0, 1, 3, 4, 5, 2).reshape(depth * n_dec * 2, KV_DIM, w_buf)

    xp = x_prompt.reshape(n_b * seq_len, D_MODEL)
    xs = x_sample.transpose(1, 0, 2).reshape(rows_s, D_MODEL)
    kv_p, kv_s, win_p, win_s, conv_p, conv_s = [], [], [], [], [], []
    for i in range(depth):
        last = i == depth - 1
        w_perm = _permute_w_in(w_in[i])
        g_row = ln_g[i].reshape(1, D_MODEL)
        cw8 = jnp.pad(conv_w[i], ((0, SUBLANE - CONV_W), (0, 0)))
        merge_w = (w_out_a[i].astype(BF16), w_out_b[i].astype(BF16), w_o[i].astype(BF16), w_ple[i].astype(BF16),
                   w_ple_gate[i].astype(BF16), final_g.reshape(1, D_MODEL))
        cmp_args = (cmp_pos_k[i], cmp_w1_k[i], cmp_w2_k[i], cmp_pos_v[i], cmp_w1_v[i], cmp_w2_v[i])

        (a_act, newkv, win, zb_act, gmix, tail, q_t, ks_ext, vs_t, kw_ext, vw_t, gb_t) = _project(
            xp, g_row, w_perm, cos_p, sin_p, jnp.zeros((n_b, carry_p, CONV_DIM), F32), cw8,
            n_seq=n_b, rows_per_seq=seq_len, tm=tm_p, stride=1, carry=carry_p, transposed=True, pos_per_seq=True)
        kc_ext, vc_t = _compress(newkv.reshape(n_b * seq_len * 4 * KV_SLABS, LANE), prompt_pages, lambda p: p,
                                 *_compress_weights(*cmp_args, True), n_pages=pages_p, prompt=True)
        kw_pad = jnp.concatenate([jnp.broadcast_to(win_pad_keys, (n_b, N_KV, WINDOW, LANE)), kw_ext], axis=2)
        vw_pad = jnp.pad(vw_t, ((0, 0), (0, 0), (0, 0), (WINDOW, 0)))
        y_act = _attn_prompt(q_t, gb_t, zb_act.reshape(n_b, seq_len, ATT_DIM), kc_ext, vc_t, ks_ext, vs_t, kw_pad,
                             vw_pad, tq=tq)
        xp = _merge(xp, a_act, y_act.reshape(n_b * seq_len, ATT_DIM), None, gmix,
                    p_prompt[i].reshape(n_b * seq_len, PLE_DIM), *merge_w, tm=tm_merge, final=last)
        kv_p.append(newkv.reshape(n_b, seq_len, 4, N_KV, HEAD_DIM))
        win_p.append(win.reshape(n_b, seq_len, 2, N_KV, HEAD_DIM)[:, seq_len - min(WINDOW, seq_len):])
        conv_p.append(tail[:, carry_p - (CONV_W - 1):])

        prefix_s = state_conv[i].transpose(1, 0, 2).reshape(1, carry_s, CONV_DIM)
        (a_act, newkv, win, zb_act, gmix, tail, qp, gb) = _project(
            xs, g_row, w_perm, cos_s, sin_s, prefix_s, cw8,
            n_seq=1, rows_per_seq=rows_s, tm=rows_s, stride=n_dec, carry=carry_s, transposed=False, pos_per_seq=False)
        base = i * n_pool
        kc_flat, vc_flat = _compress(cache_t, page_table, lambda p: (base + p) * 2,
                                     *_compress_weights(*cmp_args, False), n_pages=pages_s, prompt=False)
        q5 = jnp.pad(qp.reshape(dec_len, n_dec, N_KV, GROUP, HEAD_DIM).astype(F32),
                     ((0, tpad), (0, 0), (0, 0), (0, 0), (0, 0)))
        qbd = q5.transpose(1, 4, 2, 3, 0)[:, None] * eye_g[None, :, None, :, None, None]
        qbd = qbd.reshape(n_dec, KV_DIM, LANE).astype(BF16)
        qbdt = q5.transpose(1, 2, 3, 0, 4)[:, :, :, :, None] * eye_g[None, :, None, None, :, None]
        qbdt = qbdt.reshape(n_dec, LANE, KV_DIM).astype(BF16)
        gates = gb.reshape(dec_len, n_dec, N_KV, LANE)[..., :3 * GROUP].reshape(dec_len, n_dec, N_KV, GROUP, 3)
        gates = jnp.pad(gates.transpose(1, 2, 3, 0, 4), ((0, 0), (0, 0), (0, 0), (0, tpad), (0, LANE - 3)))
        gates = gates.reshape(n_dec, LANE, LANE)
        newkv_b = newkv.reshape(dec_len, n_dec, 4 * KV_DIM).transpose(1, 0, 2)
        newwin_b = win.reshape(dec_len, n_dec, 2 * KV_DIM).transpose(1, 0, 2)
        pad_rows = lambda a: jnp.pad(a, ((0, 0), (0, tpad), (0, 0)))
        o_all = _attn_sample(cache_t, page_table, lambda p: (base + p) * 2 + 1, win_t, lambda b: i * n_dec + b, qbd,
                             qbdt, kc_flat, vc_flat, pad_rows(newkv_b), pad_rows(newwin_b), gates, n_pages=pages_s,
                             past_len=past_len, dec_len=dec_len)
        o6 = o_all.reshape(n_dec, N_KV, GROUP, SUBLANE, N_KV, HEAD_DIM)
        yb = jnp.stack([o6[:, g, :, :, g] for g in range(N_KV)], axis=1)
        yb = yb.transpose(3, 0, 1, 2, 4)[:dec_len].reshape(rows_s, ATT_DIM)
        xs = _merge(xs, a_act, yb, zb_act, gmix, p_sample[i].transpose(1, 0, 2).reshape(rows_s, PLE_DIM), *merge_w,
                    tm=rows_s, final=last)
        kv_s.append(newkv_b.reshape(n_dec, dec_len, 4, N_KV, HEAD_DIM))
        win_s.append(jnp.concatenate([state_win[i], newwin_b.reshape(n_dec, dec_len, 2, N_KV, HEAD_DIM)],
                                     axis=1)[:, -w_buf:])
        conv_s.append(tail.reshape(CONV_W - 1, n_dec, CONV_DIM).transpose(1, 0, 2))

    y_prompt = xp.reshape(n_b, seq_len, D_MODEL)
    y_sample = xs.reshape(dec_len, n_dec, D_MODEL).transpose(1, 0, 2)
    return (y_prompt, y_sample, jnp.stack(kv_p), jnp.stack(kv_s), jnp.stack(win_p), jnp.stack(win_s),
            jnp.stack(conv_p), jnp.stack(conv_s))
```

```python
import functools

import numpy as np
import jax
import jax.numpy as jnp
from jax import lax
from jax.experimental import pallas as pl
from jax.experimental.pallas import tpu as pltpu

D_MODEL = 1024
CONV_DIM = D_MODEL
CONV_W = 3
N_HEADS = 16
HEAD_DIM = 64
N_KV = 4
GROUP = N_HEADS // N_KV
ATT_DIM = N_HEADS * HEAD_DIM
KV_DIM = N_KV * HEAD_DIM
PAGE_SIZE = 128
CMP_STRIDE = 16
CMP_BLOCK = 2 * CMP_STRIDE
CMP_HID = 4 * HEAD_DIM
SLC_BLOCK = 64
TOP_N = 16
WINDOW = 512
PLE_DIM = 256
ROPE_THETA = 10000.0
RMS_EPS = 1e-6
NEG_INF = -1e30
FORCE_SCORE = 1e6

LANE = 128
SUBLANE = 8
VMEM_LIMIT = 56 * 1024 * 1024
KV_SLABS = KV_DIM // LANE
N_GATE_ROWS = 2 * SUBLANE
CHAIN_COLS = 4 * LANE

OFF_A = 0
OFF_Q = 4 * CONV_DIM
OFF_KV = OFF_Q + ATT_DIM
OFF_WIN = OFF_KV + 4 * KV_DIM
OFF_ZB = OFF_WIN + 2 * KV_DIM
OFF_GM = OFF_ZB + ATT_DIM
OFF_GB = OFF_GM + 2 * D_MODEL
N_PROJ = OFF_GB + N_KV * LANE
MAX_BLOCKS = LANE - HEAD_DIM

F32 = jnp.float32
BF16 = jnp.bfloat16
NT_DIMS = (((1,), (1,)), ((), ()))


def _params(n_axes):
    return pltpu.CompilerParams(dimension_semantics=("arbitrary",) * n_axes, vmem_limit_bytes=VMEM_LIMIT)


def _const_spec(shape):
    zeros = (0,) * len(shape)
    return pl.BlockSpec(shape, lambda *_: zeros, pipeline_mode=pl.Buffered(1))


def _sigmoid(x):
    return 1.0 / (1.0 + jnp.exp(-x))


def _div(x, d):
    assert d & (d - 1) == 0
    return jnp.right_shift(x, d.bit_length() - 1)


def _mod(x, d):
    assert d & (d - 1) == 0
    return jnp.bitwise_and(x, d - 1)


def _iota(shape, dim):
    return lax.broadcasted_iota(jnp.int32, shape, dim)


def _rope_slab(x, cos, sin, lane):
    swapped = jnp.where(_mod(lane, HEAD_DIM) < HEAD_DIM // 2, pltpu.roll(x, LANE - HEAD_DIM // 2, 1),
                        pltpu.roll(x, HEAD_DIM // 2, 1))
    return x * cos + swapped * sin


def _overlap_t(n_blk, n_cmp):
    bj = _iota((n_blk, n_cmp), 0)
    bm = _iota((n_blk, n_cmp), 1)
    ratio = SLC_BLOCK // CMP_STRIDE
    return jnp.where((bm >= ratio * bj) & (bm <= ratio * bj + ratio) & (bm >= 1), 1.0, 0.0).astype(BF16)


def _proj_kernel(x_ref, g_ref, w_ref, cos_ref, sin_ref, pre_ref, cw_ref,
                 a_ref, kv_ref, win_ref, zb_ref, gm_ref, tail_ref, *rest, tm, stride, carry, transposed):
    if transposed:
        qt_ref, ks_ref, vst_ref, kw_ref, vwt_ref, gbt_ref, cmp_ref, u_scr = rest
    else:
        q_ref, gb_ref, u_scr = rest
    t = pl.program_id(1)
    x = x_ref[...]
    ms = jnp.mean(x * x, axis=-1, keepdims=True)
    h = (x * lax.rsqrt(ms + RMS_EPS) * g_ref[...]).astype(BF16)

    def seg(off, width):
        return jnp.dot(h, w_ref[:, off:off + width], preferred_element_type=F32)

    @pl.when(t == 0)
    def _():
        u_scr[0:carry, :] = pre_ref[0]

    u = seg(OFF_A + CONV_DIM, CONV_DIM) * seg(OFF_A + 2 * CONV_DIM, CONV_DIM)
    u_scr[carry:carry + tm, :] = u
    cw = cw_ref[...]
    conv = cw[0:1] * u_scr[carry - 2 * stride:carry - 2 * stride + tm, :]
    conv = conv + cw[1:2] * u_scr[carry - stride:carry - stride + tm, :]
    conv = conv + cw[2:3] * u
    ba = seg(OFF_A, CONV_DIM)
    za = seg(OFF_A + 3 * CONV_DIM, CONV_DIM)
    a_ref[...] = ((za * _sigmoid(za)) * (ba * conv)).astype(BF16)
    tail = u_scr[tm:tm + carry, :]
    tail_ref[0] = tail
    u_scr[0:carry, :] = tail

    lane = _iota((tm, LANE), 1)
    cos = cos_ref[...]
    sin = sin_ref[...]
    scale = HEAD_DIM ** -0.5 * (float(np.log2(np.e)) if transposed else 1.0)

    zq = seg(OFF_Q, ATT_DIM)
    for s in range(ATT_DIM // LANE):
        qs = _rope_slab(zq[:, s * LANE:(s + 1) * LANE], cos, sin, lane) * scale
        if transposed:
            qst = qs.T.astype(BF16)
            qt_ref[0, 2 * s] = qst[:HEAD_DIM]
            qt_ref[0, 2 * s + 1] = qst[HEAD_DIM:]
        else:
            q_ref[:, s * LANE:(s + 1) * LANE] = qs.astype(BF16)

    zkv = seg(OFF_KV, 4 * KV_DIM)
    zwin = seg(OFF_WIN, 2 * KV_DIM)
    kv_slabs = []
    for s in range(4 * KV_SLABS):
        v = zkv[:, s * LANE:(s + 1) * LANE]
        if (s // KV_SLABS) % 2 == 0:
            v = _rope_slab(v, cos, sin, lane)
        kv_ref[:, s * LANE:(s + 1) * LANE] = v
        kv_slabs.append(v)
    win_slabs = []
    for s in range(2 * KV_SLABS):
        v = zwin[:, s * LANE:(s + 1) * LANE]
        if s // KV_SLABS == 0:
            v = _rope_slab(v, cos, sin, lane)
        win_ref[:, s * LANE:(s + 1) * LANE] = v
        win_slabs.append(v)

    zb = seg(OFF_ZB, ATT_DIM)
    zb_ref[...] = zb * _sigmoid(zb)
    gm_ref[...] = _sigmoid(seg(OFF_GM, 2 * D_MODEL))
    gb = _sigmoid(seg(OFF_GB, N_KV * LANE))

    if not transposed:
        gb_ref[...] = gb
        return

    row = _iota((tm, LANE), 0)
    onehot = jnp.where(lane - HEAD_DIM == _div(t * tm + row, SLC_BLOCK), 1.0, 0.0)
    low = lane < HEAD_DIM
    ones_row = jnp.where(_iota((HEAD_DIM, tm), 0) == 0, 1.0, 0.0).astype(BF16)
    vs_t = [kv_slabs[3 * KV_SLABS + s].T.astype(BF16) for s in range(KV_SLABS)]
    vw_t = [win_slabs[KV_SLABS + s].T.astype(BF16) for s in range(KV_SLABS)]
    for s in range(2 * KV_SLABS):
        cmp_ref[s] = kv_slabs[s]
    for g in range(N_KV):
        s, hi = divmod(g, 2)

        def half(v):
            return v if hi == 0 else pltpu.roll(v, HEAD_DIM, 1)

        ks_ref[0, g] = jnp.where(low, half(kv_slabs[2 * KV_SLABS + s]), onehot).astype(BF16)
        kw_ref[0, g] = jnp.where(low, half(win_slabs[s]), 0.0).astype(BF16)
        vst_ref[0, g] = jnp.concatenate([vs_t[s][hi * HEAD_DIM:(hi + 1) * HEAD_DIM], ones_row], axis=0)
        vwt_ref[0, g] = jnp.concatenate([vw_t[s][hi * HEAD_DIM:(hi + 1) * HEAD_DIM], ones_row], axis=0)
        gbt_ref[0, g] = gb[:, g * LANE:(g + 1) * LANE].T[:N_GATE_ROWS]


def _project(x2d, ln_g, w_perm, cos, sin, prefix, conv_w8, *, n_seq, rows_per_seq, tm, stride, carry, transposed,
             pos_per_seq):
    nt = rows_per_seq // tm
    n_rows = n_seq * rows_per_seq
    row_map = lambda b, t: (b * nt + t, 0)
    pos_map = (lambda b, t: (t, 0)) if pos_per_seq else row_map
    in_specs = [
        pl.BlockSpec((tm, D_MODEL), row_map),
        _const_spec((1, D_MODEL)),
        _const_spec((D_MODEL, N_PROJ)),
        pl.BlockSpec((tm, LANE), pos_map),
        pl.BlockSpec((tm, LANE), pos_map),
        pl.BlockSpec((1, carry, CONV_DIM), lambda b, t: (b, 0, 0)),
        _const_spec((SUBLANE, CONV_DIM)),
    ]
    rows = lambda width, dtype: (jax.ShapeDtypeStruct((n_rows, width), dtype), pl.BlockSpec((tm, width), row_map))
    outs = [
        rows(CONV_DIM, BF16),
        rows(4 * KV_DIM, F32),
        rows(2 * KV_DIM, F32),
        rows(ATT_DIM, F32),
        rows(2 * D_MODEL, F32),
        (jax.ShapeDtypeStruct((n_seq, carry, CONV_DIM), F32),
         pl.BlockSpec((1, carry, CONV_DIM), lambda b, t: (b, 0, 0))),
    ]
    if transposed:
        tok_sub = (jax.ShapeDtypeStruct((n_seq, N_KV, rows_per_seq, LANE), BF16),
                   pl.BlockSpec((1, N_KV, tm, LANE), lambda b, t: (b, 0, t, 0)))
        tok_lane = lambda n, rows_, dtype: (jax.ShapeDtypeStruct((n_seq, n, rows_, rows_per_seq), dtype),
                                            pl.BlockSpec((1, n, rows_, tm), lambda b, t: (b, 0, 0, t)))
        outs += [tok_lane(N_HEADS, HEAD_DIM, BF16),
                 tok_sub, tok_lane(N_KV, LANE, BF16),
                 tok_sub, tok_lane(N_KV, LANE, BF16),
                 tok_lane(N_KV, N_GATE_ROWS, F32),
                 (jax.ShapeDtypeStruct((2 * KV_SLABS, n_rows, LANE), F32),
                  pl.BlockSpec((2 * KV_SLABS, tm, LANE), lambda b, t: (0, b * nt + t, 0)))]
    else:
        outs += [rows(ATT_DIM, BF16), rows(N_KV * LANE, F32)]
    return pl.pallas_call(
        functools.partial(_proj_kernel, tm=tm, stride=stride, carry=carry, transposed=transposed),
        grid=(n_seq, nt),
        in_specs=in_specs,
        out_specs=[o[1] for o in outs],
        out_shape=[o[0] for o in outs],
        scratch_shapes=[pltpu.VMEM((carry + tm, CONV_DIM), F32)],
        compiler_params=_params(2),
        name="project_prompt" if transposed else "project_sample",
    )(x2d, ln_g, w_perm, cos, sin, prefix, conv_w8)


def _merge_kernel(x_ref, a_ref, gm_ref, p_ref, woa_ref, wob_ref, wo_ref, wple_ref, wpg_ref, fg_ref, y_ref, *rest,
                  y_needs_gate, final):
    br_a = jnp.dot(a_ref[...], woa_ref[...], preferred_element_type=F32)
    if y_needs_gate:
        zb_ref, o_ref = rest
        y = (zb_ref[...] * y_ref[...]).astype(BF16)
    else:
        (o_ref,) = rest
        y = y_ref[...]
    br_b = jnp.dot(y, wob_ref[...], preferred_element_type=F32)
    gm = gm_ref[...]
    mix = gm[:, :D_MODEL] * br_a + gm[:, D_MODEL:] * br_b
    x1 = x_ref[...] + jnp.dot(mix.astype(BF16), wo_ref[...], preferred_element_type=F32)
    gate = _sigmoid(jnp.dot(x1.astype(BF16), wpg_ref[...], preferred_element_type=F32))
    x2 = x1 + gate * jnp.dot(p_ref[...].astype(BF16), wple_ref[...], preferred_element_type=F32)
    if final:
        ms = jnp.mean(x2 * x2, axis=-1, keepdims=True)
        x2 = x2 * lax.rsqrt(ms + RMS_EPS) * fg_ref[...]
    o_ref[...] = x2


def _merge(x2d, a_act, y, zb_act, gmix, p2d, w_out_a, w_out_b, w_o, w_ple, w_pg, final_g, *, tm, final):
    n_rows = x2d.shape[0]
    row = lambda width: pl.BlockSpec((tm, width), lambda t: (t, 0))
    y_args = (y,) if zb_act is None else (y, zb_act)
    return pl.pallas_call(
        functools.partial(_merge_kernel, y_needs_gate=zb_act is not None, final=final),
        grid=(n_rows // tm,),
        in_specs=[row(D_MODEL), row(CONV_DIM), row(2 * D_MODEL), row(PLE_DIM),
                  _const_spec((CONV_DIM, D_MODEL)), _const_spec((ATT_DIM, D_MODEL)),
                  _const_spec((D_MODEL, D_MODEL)), _const_spec((PLE_DIM, D_MODEL)),
                  _const_spec((D_MODEL, D_MODEL)), _const_spec((1, D_MODEL))] + [row(ATT_DIM)] * len(y_args),
        out_specs=row(D_MODEL),
        out_shape=jax.ShapeDtypeStruct((n_rows, D_MODEL), F32),
        compiler_params=_params(1),
        name="merge_prompt" if zb_act is None else "merge_sample",
    )(x2d, a_act, gmix, p2d, w_out_a, w_out_b, w_o, w_ple, w_pg, final_g, *y_args)


def _compress_kernel(pt_ref, *refs, n_pages, prompt):
    pages = refs[:n_pages]
    pos_ref, w1_ref, w2_ref, ko_ref, vo_ref, rows_scr, x_scr, carry_scr = refs[n_pages:]
    st = pl.program_id(1)
    per_page = PAGE_SIZE // CMP_STRIDE
    n = n_pages * per_page
    low = _iota((n, LANE), 1) < HEAD_DIM

    @pl.when(st == 0)
    def _():
        carry_scr[...] = jnp.zeros_like(carry_scr)

    row0 = _iota((n, CMP_HID), 0) == 0
    for slot in range(2):
        for j in range(n_pages):
            for v in range(KV_SLABS):
                if prompt:
                    piece = pages[j][slot * KV_SLABS + v]
                else:
                    piece = pages[j][slot, v * LANE:(v + 1) * LANE, :].T
                rows_scr[slot, v, j * PAGE_SIZE:(j + 1) * PAGE_SIZE, :] = piece
        for pair in range(CMP_STRIDE // 2):
            for v in range(KV_SLABS):
                a = rows_scr[slot, v, pl.ds(2 * pair, n, stride=CMP_STRIDE), :]
                b = rows_scr[slot, v, pl.ds(2 * pair + 1, n, stride=CMP_STRIDE), :]
                x_scr[slot, 2 * v * n:(2 * v + 1) * n, pair * LANE:(pair + 1) * LANE] = jnp.where(
                    low, a, pltpu.roll(b, HEAD_DIM, 1))
                x_scr[slot, (2 * v + 1) * n:(2 * v + 2) * n, pair * LANE:(pair + 1) * LANE] = jnp.where(
                    low, pltpu.roll(a, HEAD_DIM, 1), b)
    for slot, out_ref in enumerate((ko_ref, vo_ref)):
        xs = x_scr[slot]
        pos = pos_ref[slot]
        f = jnp.dot((xs + pos[0:1]).astype(BF16), w1_ref[slot, 0], preferred_element_type=F32)
        s = jnp.dot((xs + pos[1:2]).astype(BF16), w1_ref[slot, 1], preferred_element_type=F32)
        acc = None
        for g in range(N_KV):
            fg = f[g * n:(g + 1) * n]
            prev = carry_scr[slot, g, SUBLANE - 1:SUBLANE, :]
            shifted = jnp.where(row0, prev, pltpu.roll(fg, 1, 0))
            carry_scr[slot, g] = fg[n - SUBLANE:n]
            pre = shifted + s[g * n:(g + 1) * n]
            hid = (pre * _sigmoid(pre)).astype(BF16)
            if prompt and slot == 0:
                out_ref[0, g] = jnp.dot(hid, w2_ref[0, 0, :, :LANE], preferred_element_type=F32).astype(BF16)
            elif prompt:
                out_ref[0, g] = lax.dot_general(w2_ref[1, 0, :LANE, :], hid, NT_DIMS,
                                                preferred_element_type=F32).astype(BF16)
            else:
                part = jnp.dot(hid, w2_ref[slot, g], preferred_element_type=F32)
                acc = part if acc is None else acc + part
        if not prompt:
            out_ref[0] = acc.astype(BF16)


def _compress(pages_arr, page_table, page_index, pos2, w1, w2, *, n_pages, prompt):
    n_seq, pages_per_seq = page_table.shape
    n_steps = pages_per_seq // n_pages
    n = n_pages * (PAGE_SIZE // CMP_STRIDE)
    n_chunk = pages_per_seq * (PAGE_SIZE // CMP_STRIDE)
    flat = CMP_STRIDE * HEAD_DIM

    def page_spec(j):
        if prompt:
            return pl.BlockSpec((2 * KV_SLABS, PAGE_SIZE, LANE),
                                lambda b, s, pt: (0, page_index(pt[b, s * n_pages + j]), 0))
        return pl.BlockSpec((2, KV_DIM, PAGE_SIZE), lambda b, s, pt: (page_index(pt[b, s * n_pages + j]), 0, 0))

    if prompt:
        out_sds = [jax.ShapeDtypeStruct((n_seq, N_KV, n_chunk, LANE), BF16),
                   jax.ShapeDtypeStruct((n_seq, N_KV, LANE, n_chunk), BF16)]
        out_specs = [pl.BlockSpec((1, N_KV, n, LANE), lambda b, s, pt: (b, 0, s, 0)),
                     pl.BlockSpec((1, N_KV, LANE, n), lambda b, s, pt: (b, 0, 0, s))]
    else:
        out_sds = [jax.ShapeDtypeStruct((n_seq, n_chunk, KV_DIM), BF16)] * 2
        out_specs = [pl.BlockSpec((1, n, KV_DIM), lambda b, s, pt: (b, s, 0))] * 2
    const = lambda shape: pl.BlockSpec(shape, lambda b, s, pt: (0,) * len(shape), pipeline_mode=pl.Buffered(1))
    grid_spec = pltpu.PrefetchScalarGridSpec(
        num_scalar_prefetch=1,
        grid=(n_seq, n_steps),
        in_specs=[page_spec(j) for j in range(n_pages)] + [const(pos2.shape), const(w1.shape), const(w2.shape)],
        out_specs=out_specs,
        scratch_shapes=[pltpu.VMEM((2, KV_SLABS, n_pages * PAGE_SIZE, LANE), F32),
                        pltpu.VMEM((2, N_KV * n, flat), F32),
                        pltpu.VMEM((2, N_KV, SUBLANE, CMP_HID), F32)],
    )
    return pl.pallas_call(
        functools.partial(_compress_kernel, n_pages=n_pages, prompt=prompt),
        grid_spec=grid_spec,
        out_shape=out_sds,
        compiler_params=_params(2),
        name="compress_prompt" if prompt else "compress_sample",
    )(page_table, *([pages_arr] * n_pages), pos2, w1, w2)


def _select_bias(score_ref, cnt_ref, n_real, n_live, n_sel):
    score = score_ref[...]
    n_rows = -(-n_real // SUBLANE) * SUBLANE
    sub = _iota((SUBLANE, score.shape[1]), 0)
    cnt_ref[...] = jnp.zeros(score.shape, F32)

    def group(i0):
        cnt = cnt_ref[0:n_rows, :]
        for i in range(i0, min(i0 + SUBLANE, n_real)):
            si = score_ref[i:i + 1, :]
            pieces = []
            for r0 in range(0, n_rows, SUBLANE):
                sv = score[r0:r0 + SUBLANE]
                if r0 > i:
                    pieces.append(jnp.where(si >= sv, 1.0, 0.0))
                elif r0 + SUBLANE - 1 < i:
                    pieces.append(jnp.where(si > sv, 1.0, 0.0))
                else:
                    pieces.append(jnp.where(sub > i - r0, jnp.where(si >= sv, 1.0, 0.0), jnp.where(si > sv, 1.0, 0.0)))
            cnt = cnt + jnp.concatenate(pieces, axis=0)
        cnt_ref[0:n_rows, :] = cnt

    for i0 in range(0, n_real, SUBLANE):
        if n_live is None:
            group(i0)
        else:
            pl.when(i0 < n_live)(functools.partial(group, i0))
    return jnp.where((cnt_ref[...] < n_sel) & (score >= 0.0), 0.0, NEG_INF)


def _attn_prompt_kernel(qt_ref, gbt_ref, zb_ref, tri_ref, kc_ref, vct_ref, ks_ref, vst_ref, kw_ref, vwt_ref, o_ref,
                        score_scr, cnt_scr, m_scr, acc_scr, mx_scr, s_even_scr, s_odd_scr, *, tq, n_cmp, n_blk):
    qi = pl.program_id(2)
    s0 = qi * tq
    rows = GROUP * tq
    cw = min(rows, CHAIN_COLS)
    col_groups = [slice(c, c + cw) for c in range(0, rows, cw)]
    qpos = s0 + _mod(_iota((1, rows), 1), tq)

    def stack_q(lower):
        return jnp.concatenate([jnp.concatenate([qt_ref[0, r], lower], axis=0) for r in range(GROUP)], axis=1)

    q0 = stack_q(jnp.where(_iota((LANE - HEAD_DIM, tq), 0) == 0, NEG_INF, 0.0).astype(BF16))

    s_c = jnp.dot(kc_ref[0, 0], q0, preferred_element_type=F32)

    def window_scores(cs, first, n_keys):
        keys = kw_ref[0, 0, pl.ds(pl.multiple_of(s0 + first, tq), n_keys), :]
        return jnp.dot(keys, q0[:, cs], preferred_element_type=F32)

    w_parts = []
    for cs in col_groups:
        parts = [window_scores(cs, 0, tq) + tri_ref[0, :, cs]]
        if WINDOW > tq:
            parts.append(window_scores(cs, tq, WINDOW - tq))
        parts.append(window_scores(cs, WINDOW, tq) + tri_ref[1, :, cs])
        w_parts.append(parts)

    m_idx = _iota((n_cmp, 1), 0)
    mask_c = (m_idx >= 1) & (CMP_STRIDE * m_idx + CMP_STRIDE - 1 <= qpos)
    s_c = jnp.where(mask_c, s_c, NEG_INF)
    e_c = jnp.where(mask_c, jnp.exp2(s_c - jnp.max(s_c, axis=0, keepdims=True)), 0.0)
    p_c = e_c * (1.0 / jnp.maximum(jnp.sum(e_c, axis=0, keepdims=True), 1e-30))
    o_c = jnp.dot(vct_ref[0, 0], p_c.astype(BF16), preferred_element_type=F32)

    p_sum = p_c[:, 0:tq]
    for r in range(1, GROUP):
        p_sum = p_sum + p_c[:, r * tq:(r + 1) * tq]
    p_hi = p_sum.astype(BF16)
    p_lo = (p_sum - p_hi.astype(F32)).astype(BF16)
    ov_t = _overlap_t(MAX_BLOCKS, n_cmp)
    imp = jnp.dot(ov_t, p_hi, preferred_element_type=F32) + jnp.dot(ov_t, p_lo, preferred_element_type=F32)

    w_out = []
    for cs, parts in zip(col_groups, w_parts):
        m_w = functools.reduce(jnp.maximum, [jnp.max(s, axis=0, keepdims=True) for s in parts])
        p_w = jnp.concatenate([jnp.exp2(s - m_w).astype(BF16) for s in parts], axis=0)
        acc_w = jnp.dot(vwt_ref[0, 0, :, pl.ds(pl.multiple_of(s0, tq), WINDOW + tq)], p_w, preferred_element_type=F32)
        w_out.append(acc_w[:HEAD_DIM] / jnp.maximum(acc_w[HEAD_DIM:HEAD_DIM + 1], 1e-30))
    o_w = jnp.concatenate(w_out, axis=1)

    blk = _iota((MAX_BLOCKS, tq), 0)
    cur = _div(s0 + _iota((MAX_BLOCKS, tq), 1), SLC_BLOCK)
    forced = (blk == 0) | (blk == cur) | (blk == cur - 1)
    score = jnp.where(forced, FORCE_SCORE, imp)
    score_scr[...] = jnp.where(blk <= cur, score, -1.0)
    n_live = jnp.minimum(n_blk, (s0 + tq) // SLC_BLOCK)
    bias_t = _select_bias(score_scr, cnt_scr, n_blk, n_live, min(TOP_N, n_blk))
    qx = stack_q(bias_t.astype(BF16))

    m_scr[...] = jnp.full(m_scr.shape, NEG_INF, F32)
    acc_scr[...] = jnp.zeros(acc_scr.shape, F32)
    s_bufs = (s_even_scr, s_odd_scr)

    def scores(k0, slot, causal):
        keys = ks_ref[0, 0, pl.ds(pl.multiple_of(k0, tq), tq), :]
        prods = [jnp.dot(keys, qx[:, cs], preferred_element_type=F32) for cs in col_groups]
        for cs, s in zip(col_groups, prods):
            if causal:
                s = s + tri_ref[1, :, cs]
            s_bufs[slot][:, cs] = s
            mx_scr[slot, 0:1, cs] = jnp.max(s, axis=0, keepdims=True)

    def accumulate(k0, slot):
        vals_t = vst_ref[0, 0, :, pl.ds(pl.multiple_of(k0, tq), tq)]
        for cs in col_groups:
            m_old = m_scr[0:1, cs]
            m_new = jnp.maximum(m_old, mx_scr[slot, 0:1, cs])
            p = jnp.exp2(s_bufs[slot][:, cs] - m_new).astype(BF16)
            acc_scr[:, cs] = jnp.exp2(m_old - m_new) * acc_scr[:, cs] + jnp.dot(vals_t, p, preferred_element_type=F32)
            m_scr[0:1, cs] = m_new

    n_pairs = qi // 2
    scores(s0, 0, True)

    def pair(j, _):
        scores(2 * j * tq, 1, False)
        accumulate(jnp.where(j == 0, s0, (2 * j - 1) * tq), 0)
        scores((2 * j + 1) * tq, 0, False)
        accumulate(2 * j * tq, 1)
        return 0

    lax.fori_loop(0, n_pairs, pair, 0)
    pending = jnp.where(n_pairs == 0, s0, (2 * n_pairs - 1) * tq)

    @pl.when(qi - 2 * n_pairs == 1)
    def _():
        scores(s0 - tq, 1, False)
        accumulate(pending, 0)
        accumulate(s0 - tq, 1)

    @pl.when(qi - 2 * n_pairs == 0)
    def _():
        accumulate(pending, 0)

    o_s = acc_scr[:HEAD_DIM, :] / jnp.maximum(acc_scr[HEAD_DIM:HEAD_DIM + 1, :], 1e-30)

    gb = gbt_ref[0, 0]
    zb = zb_ref[0]
    heads = []
    for r in range(GROUP):
        rs = slice(r * tq, (r + 1) * tq)
        heads.append(gb[3 * r:3 * r + 1] * o_c[:HEAD_DIM, rs] + gb[3 * r + 1:3 * r + 2] * o_s[:, rs]
                     + gb[3 * r + 2:3 * r + 3] * o_w[:, rs])
    for pair in range(GROUP // 2):
        slab = jnp.concatenate([heads[2 * pair], heads[2 * pair + 1]], axis=0).T
        o_ref[0, :, pair * LANE:(pair + 1) * LANE] = (zb[:, pair * LANE:(pair + 1) * LANE] * slab).astype(BF16)


def _attn_prompt(q_t, gb_t, zb_act, kc_ext, vc_t, ks_ext, vs_t, kw_pad, vw_pad, *, tq):
    n_b, seq_len = zb_act.shape[0], zb_act.shape[1]
    n_cmp = kc_ext.shape[2]
    n_blk = seq_len // SLC_BLOCK
    head_w = GROUP * HEAD_DIM
    rows = GROUP * tq
    assert WINDOW % tq == 0 and rows % min(rows, CHAIN_COLS) == 0
    off = _iota((tq, rows), 0) - _mod(_iota((tq, rows), 1), tq)
    tri = jnp.stack([jnp.where(off > 0, 0.0, NEG_INF), jnp.where(off <= 0, 0.0, NEG_INF)]).astype(F32)
    tok_sub = lambda n: pl.BlockSpec((1, 1, n, LANE), lambda b, g, i: (b, g, 0, 0))
    tok_lane = lambda n: pl.BlockSpec((1, 1, LANE, n), lambda b, g, i: (b, g, 0, 0))
    return pl.pallas_call(
        functools.partial(_attn_prompt_kernel, tq=tq, n_cmp=n_cmp, n_blk=n_blk),
        grid=(n_b, N_KV, seq_len // tq),
        in_specs=[pl.BlockSpec((1, GROUP, HEAD_DIM, tq), lambda b, g, i: (b, g, 0, i)),
                  pl.BlockSpec((1, 1, N_GATE_ROWS, tq), lambda b, g, i: (b, g, 0, i)),
                  pl.BlockSpec((1, tq, head_w), lambda b, g, i: (b, i, g)),
                  _const_spec((2, tq, rows)),
                  tok_sub(n_cmp), tok_lane(n_cmp), tok_sub(seq_len), tok_lane(seq_len),
                  tok_sub(seq_len + WINDOW), tok_lane(seq_len + WINDOW)],
        out_specs=pl.BlockSpec((1, tq, head_w), lambda b, g, i: (b, i, g)),
        out_shape=jax.ShapeDtypeStruct((n_b, seq_len, ATT_DIM), BF16),
        scratch_shapes=[pltpu.VMEM((MAX_BLOCKS, tq), F32), pltpu.VMEM((MAX_BLOCKS, tq), F32),
                        pltpu.VMEM((SUBLANE, rows), F32),
                        pltpu.VMEM((LANE, rows), F32), pltpu.VMEM((2, SUBLANE, rows), F32),
                        pltpu.VMEM((tq, rows), F32), pltpu.VMEM((tq, rows), F32)],
        compiler_params=_params(3),
        name="attention_prompt",
    )(q_t, gb_t, zb_act, tri, kc_ext, vc_t, ks_ext, vs_t, kw_pad, vw_pad)


def _attn_sample_kernel(pt_ref, *refs, n_pages, past_len, dec_len, n_blk_pad):
    pages = refs[:n_pages]
    (qbd_ref, qbdt_ref, kc_ref, vc_ref, newkv_ref, win_ref, newwin_ref, gate_ref, o_ref,
     score_scr, cnt_scr, bias_scr, m_scr, l_scr, acc_scr, oc_scr) = refs[n_pages:]
    st = pl.program_id(1)
    n_steps = pl.num_programs(1)
    cols = LANE
    tslots = cols // N_HEADS
    n_cmp = kc_ref.shape[1]
    n_past_blk = past_len // SLC_BLOCK
    per_page = PAGE_SIZE // SLC_BLOCK
    qbdt = qbdt_ref[0]
    tok_c = _mod(_iota((cols, 1), 0), tslots)

    @pl.when(st == 0)
    def _():
        qpos = past_len + _mod(_iota((1, cols), 1), tslots)
        s_c = jnp.dot(kc_ref[0], qbd_ref[0], preferred_element_type=F32)
        m_idx = _iota((n_cmp, 1), 0)
        mask_c = (m_idx >= 1) & (CMP_STRIDE * m_idx + CMP_STRIDE - 1 <= qpos)
        s_c = jnp.where(mask_c, s_c, NEG_INF)
        e_c = jnp.where(mask_c, jnp.exp(s_c - jnp.max(s_c, axis=0, keepdims=True)), 0.0)
        p_c = e_c / jnp.maximum(jnp.sum(e_c, axis=0, keepdims=True), 1e-30)
        p_hi = p_c.astype(BF16)
        p_lo = (p_c - p_hi.astype(F32)).astype(BF16)
        oc_scr[...] = jnp.dot(p_c.T.astype(BF16), vc_ref[0], preferred_element_type=F32)
        ov_t = _overlap_t(n_blk_pad, n_cmp)
        imp1 = jnp.dot(ov_t, p_hi, preferred_element_type=F32) + jnp.dot(ov_t, p_lo, preferred_element_type=F32)
        ca = _iota((cols, cols), 0)
        cb = _iota((cols, cols), 1)
        per_group = GROUP * tslots
        same = jnp.where((_div(ca, per_group) == _div(cb, per_group)) & (_mod(ca, tslots) == _mod(cb, tslots)),
                         1.0, 0.0).astype(BF16)
        i_hi = imp1.astype(BF16)
        i_mid = (imp1 - i_hi.astype(F32)).astype(BF16)
        i_lo = (imp1 - i_hi.astype(F32) - i_mid.astype(F32)).astype(BF16)
        imp = (jnp.dot(i_hi, same, preferred_element_type=F32) + jnp.dot(i_mid, same, preferred_element_type=F32)
               + jnp.dot(i_lo, same, preferred_element_type=F32))
        blk = _iota((n_blk_pad, cols), 0)
        cur = _div(qpos, SLC_BLOCK)
        forced = (blk == 0) | (blk == cur) | (blk == cur - 1)
        score = jnp.where(forced, FORCE_SCORE, imp)
        score_scr[...] = jnp.where(blk <= cur, score, -1.0)
        n_blk = n_past_blk + 1
        bias_scr[...] = _select_bias(score_scr, cnt_scr, n_blk, None, min(TOP_N, n_blk)).T
        m_scr[...] = jnp.full(m_scr.shape, NEG_INF, F32)
        l_scr[...] = jnp.zeros(l_scr.shape, F32)
        acc_scr[...] = jnp.zeros(acc_scr.shape, F32)

    def update(s, pv_fn):
        m_old = m_scr[...]
        m_new = jnp.maximum(m_old, jnp.max(s, axis=1, keepdims=True))
        alpha = jnp.exp(m_old - m_new)
        p = jnp.exp(s - m_new)
        l_scr[...] = alpha * l_scr[...] + jnp.sum(p, axis=1, keepdims=True)
        acc_scr[...] = alpha * acc_scr[...] + pv_fn(p.astype(BF16))
        m_scr[...] = m_new

    bias_b = bias_scr[...].astype(BF16)
    blk0 = st * (n_pages * per_page)
    key_blk = blk0 + _div(_iota((n_blk_pad, n_pages * PAGE_SIZE), 1), SLC_BLOCK)
    expand = jnp.where(_iota((n_blk_pad, n_pages * PAGE_SIZE), 0) == key_blk, 1.0, 0.0).astype(BF16)
    keys_t = jnp.concatenate([pages[j][0].astype(BF16) for j in range(n_pages)], axis=1)
    vals_t = jnp.concatenate([pages[j][1].astype(BF16) for j in range(n_pages)], axis=1)
    s = (jnp.dot(qbdt, keys_t, preferred_element_type=F32) + jnp.dot(bias_b, expand, preferred_element_type=F32))
    update(s, lambda p: lax.dot_general(p, vals_t, NT_DIMS, preferred_element_type=F32))

    @pl.when(st == n_steps - 1)
    def _():
        def pad_rows(a):
            return jnp.concatenate([a, jnp.zeros((LANE - a.shape[0], a.shape[1]), a.dtype)], axis=0)

        key_i = _iota((1, LANE), 1)
        new = pad_rows(newkv_ref[0])
        s_n = lax.dot_general(qbdt, new[:, 2 * KV_DIM:3 * KV_DIM].astype(BF16), NT_DIMS, preferred_element_type=F32)
        s_n = s_n + bias_scr[:, n_past_blk:n_past_blk + 1]
        s_n = jnp.where((key_i <= tok_c) & (key_i < dec_len), s_n, NEG_INF)
        v_n = new[:, 3 * KV_DIM:].astype(BF16)
        update(s_n, lambda p: jnp.dot(p, v_n, preferred_element_type=F32))
        o_s = acc_scr[...] / jnp.maximum(l_scr[...], 1e-30)

        w_buf = win_ref.shape[2]
        neww = pad_rows(newwin_ref[0])
        s_w = jnp.concatenate(
            [jnp.dot(qbdt, win_ref[0].astype(BF16), preferred_element_type=F32),
             lax.dot_general(qbdt, neww[:, :KV_DIM].astype(BF16), NT_DIMS, preferred_element_type=F32)], axis=1)
        idx = _iota((1, w_buf + LANE), 1)
        kpos = past_len - w_buf + idx
        dist = past_len + tok_c - kpos
        mask_w = (dist >= 0) & (dist < WINDOW) & (idx < w_buf + dec_len) & (kpos >= 0)
        s_w = jnp.where(mask_w, s_w, NEG_INF)
        e_w = jnp.where(mask_w, jnp.exp(s_w - jnp.max(s_w, axis=1, keepdims=True)), 0.0)
        p_w = (e_w / jnp.maximum(jnp.sum(e_w, axis=1, keepdims=True), 1e-30)).astype(BF16)
        o_w = (lax.dot_general(p_w[:, :w_buf], win_ref[1].astype(BF16), NT_DIMS, preferred_element_type=F32)
               + jnp.dot(p_w[:, w_buf:], neww[:, KV_DIM:].astype(BF16), preferred_element_type=F32))
        gate = gate_ref[0]
        o_ref[0] = gate[:, 0:1] * oc_scr[...] + gate[:, 1:2] * o_s + gate[:, 2:3] * o_w


def _attn_sample(cache_t, page_table, page_index, win_t, win_index, qbd, qbdt, kc_flat, vc_flat, newkv8, newwin8,
                 gates, *, n_pages, past_len, dec_len):
    n_seq, pages_per_seq = page_table.shape
    n_steps = pages_per_seq // n_pages
    n_cmp = kc_flat.shape[1]
    n_blk_pad = -(-(past_len // SLC_BLOCK + 1) // LANE) * LANE
    w_buf = win_t.shape[2]

    def page_spec(j):
        return pl.BlockSpec((2, KV_DIM, PAGE_SIZE), lambda b, s, pt: (page_index(pt[b, s * n_pages + j]), 0, 0))

    per_seq = lambda shape: pl.BlockSpec((1,) + shape, lambda b, s, pt: (b,) + (0,) * len(shape))
    grid_spec = pltpu.PrefetchScalarGridSpec(
        num_scalar_prefetch=1,
        grid=(n_seq, n_steps),
        in_specs=[page_spec(j) for j in range(n_pages)] + [
            per_seq((KV_DIM, LANE)), per_seq((LANE, KV_DIM)), per_seq((n_cmp, KV_DIM)), per_seq((n_cmp, KV_DIM)),
            per_seq((SUBLANE, 4 * KV_DIM)),
            pl.BlockSpec((2, KV_DIM, w_buf), lambda b, s, pt: (win_index(b), 0, 0)),
            per_seq((SUBLANE, 2 * KV_DIM)), per_seq((LANE, LANE))],
        out_specs=per_seq((LANE, KV_DIM)),
        scratch_shapes=[pltpu.VMEM((n_blk_pad, LANE), F32), pltpu.VMEM((n_blk_pad, LANE), F32),
                        pltpu.VMEM((LANE, n_blk_pad), F32),
                        pltpu.VMEM((LANE, 1), F32), pltpu.VMEM((LANE, 1), F32),
                        pltpu.VMEM((LANE, KV_DIM), F32), pltpu.VMEM((LANE, KV_DIM), F32)],
    )
    return pl.pallas_call(
        functools.partial(_attn_sample_kernel, n_pages=n_pages, past_len=past_len, dec_len=dec_len,
                          n_blk_pad=n_blk_pad),
        grid_spec=grid_spec,
        out_shape=jax.ShapeDtypeStruct((n_seq, LANE, KV_DIM), F32),
        compiler_params=_params(2),
        name="attention_sample",
    )(page_table, *([cache_t] * n_pages), qbd, qbdt, kc_flat, vc_flat, newkv8, win_t, newwin8, gates)


def _permute_w_in(w):
    d = w.shape[0]
    c = np.cumsum([0, 4 * CONV_DIM, ATT_DIM, 4 * KV_DIM, 2 * KV_DIM, 3 * N_HEADS, ATT_DIM, 2 * D_MODEL])
    wa, wq, wkv, wwin, wgb, wzb, wgm = (w[:, c[i]:c[i + 1]] for i in range(7))
    wgb = jnp.pad(wgb.reshape(d, N_KV, 3 * GROUP), ((0, 0), (0, 0), (0, LANE - 3 * GROUP))).reshape(d, N_KV * LANE)
    return jnp.concatenate([wa, wq, wkv, wwin, wzb, wgm, wgb], axis=1).astype(BF16)


def _rope_tables(pos):
    half = HEAD_DIM // 2
    inv = jnp.power(ROPE_THETA, -jnp.arange(half, dtype=F32) / half)
    ang = pos.astype(F32)[:, None] * inv[None, :]
    cos, sin = jnp.cos(ang), jnp.sin(ang)
    reps = LANE // HEAD_DIM
    return jnp.tile(jnp.concatenate([cos, cos], axis=1), (1, reps)), jnp.tile(jnp.concatenate([-sin, sin], axis=1), (1, reps))


def _compress_weights(pos_k, w1_k, w2_k, pos_v, w1_v, w2_v, grouped):
    flat = CMP_STRIDE * HEAD_DIM
    pos2 = jnp.stack([jnp.pad(p.reshape(2, flat), ((0, SUBLANE - 2), (0, 0))) for p in (pos_k, pos_v)])
    w1 = jnp.stack([w.reshape(2, flat, CMP_HID) for w in (w1_k, w1_v)]).astype(BF16)
    if grouped:
        w2 = jnp.stack([jnp.pad(w2_k, ((0, 0), (0, CMP_HID - HEAD_DIM)))[None],
                        jnp.pad(w2_v.T, ((0, CMP_HID - HEAD_DIM), (0, 0)))[None]])
    else:
        w2 = jnp.stack([jnp.stack([jnp.pad(w, ((0, 0), (g * HEAD_DIM, KV_DIM - (g + 1) * HEAD_DIM)))
                                   for g in range(N_KV)]) for w in (w2_k, w2_v)])
    return pos2, w1, w2.astype(BF16)


def kernel(x_prompt, x_sample, cache_kv, state_win, state_conv, page_table, p_prompt, p_sample, ln_g, w_in, conv_w,
           cmp_pos_k, cmp_w1_k, cmp_w2_k, cmp_pos_v, cmp_w1_v, cmp_w2_v, w_out_a, w_out_b, w_o, w_ple, w_ple_gate,
           final_g):
    depth = ln_g.shape[0]
    n_b, seq_len, _ = x_prompt.shape
    n_dec, dec_len, _ = x_sample.shape
    pages_per_seq = page_table.shape[1]
    past_len = pages_per_seq * PAGE_SIZE
    n_pool = cache_kv.shape[1]
    w_buf = state_win.shape[2]

    tm_p = min(256, seq_len)
    tq = min(256, seq_len)
    tm_merge = min(512, seq_len)
    pages_p = min(16, seq_len // PAGE_SIZE)
    pages_s = min(16, pages_per_seq)
    rows_s = n_dec * dec_len
    carry_p = SUBLANE
    carry_s = (CONV_W - 1) * n_dec
    assert seq_len % PAGE_SIZE == 0 and seq_len // SLC_BLOCK <= MAX_BLOCKS and seq_len >= WINDOW + tq
    assert CONV_W - 1 <= dec_len <= SUBLANE and rows_s % SUBLANE == 0
    assert w_buf == min(WINDOW, past_len) and N_HEADS * SUBLANE == LANE

    cos_p, sin_p = _rope_tables(jnp.arange(seq_len, dtype=jnp.int32))
    pos_s = past_len + jnp.repeat(jnp.arange(dec_len, dtype=jnp.int32), n_dec)
    cos_s, sin_s = _rope_tables(pos_s)
    prompt_pages = jnp.arange(n_b * (seq_len // PAGE_SIZE), dtype=jnp.int32).reshape(n_b, seq_len // PAGE_SIZE)
    eye_g = jnp.eye(N_KV, dtype=F32)
    tpad = SUBLANE - dec_len
    win_pad_keys = (jnp.arange(LANE) == HEAD_DIM).astype(BF16)
    cache_t = cache_kv.transpose(0, 1, 3, 4, 5, 2).reshape(depth * n_pool * 4, KV_DIM, PAGE_SIZE)
    win_t = state_win.transpose(0, 1, 3, 4, 5, 2).reshape(depth * n_dec * 2, KV_DIM, w_buf)

    xp = x_prompt.reshape(n_b * seq_len, D_MODEL)
    xs = x_sample.transpose(1, 0, 2).reshape(rows_s, D_MODEL)
    kv_p, kv_s, win_p, win_s, conv_p, conv_s = [], [], [], [], [], []
    for i in range(depth):
        last = i == depth - 1
        w_perm = _permute_w_in(w_in[i])
        g_row = ln_g[i].reshape(1, D_MODEL)
        cw8 = jnp.pad(conv_w[i], ((0, SUBLANE - CONV_W), (0, 0)))
        merge_w = (w_out_a[i].astype(BF16), w_out_b[i].astype(BF16), w_o[i].astype(BF16), w_ple[i].astype(BF16),
                   w_ple_gate[i].astype(BF16), final_g.reshape(1, D_MODEL))
        cmp_args = (cmp_pos_k[i], cmp_w1_k[i], cmp_w2_k[i], cmp_pos_v[i], cmp_w1_v[i], cmp_w2_v[i])

        (a_act, newkv, win, zb_act, gmix, tail, q_t, ks_ext, vs_t, kw_ext, vw_t, gb_t, cmp_rows) = _project(
            xp, g_row, w_perm, cos_p, sin_p, jnp.zeros((n_b, carry_p, CONV_DIM), F32), cw8,
            n_seq=n_b, rows_per_seq=seq_len, tm=tm_p, stride=1, carry=carry_p, transposed=True, pos_per_seq=True)
        kc_ext, vc_t = _compress(cmp_rows, prompt_pages, lambda p: p,
                                 *_compress_weights(*cmp_args, True), n_pages=pages_p, prompt=True)
        kw_pad = jnp.concatenate([jnp.broadcast_to(win_pad_keys, (n_b, N_KV, WINDOW, LANE)), kw_ext], axis=2)
        vw_pad = jnp.pad(vw_t, ((0, 0), (0, 0), (0, 0), (WINDOW, 0)))
        y_act = _attn_prompt(q_t, gb_t, zb_act.reshape(n_b, seq_len, ATT_DIM), kc_ext, vc_t, ks_ext, vs_t, kw_pad,
                             vw_pad, tq=tq)
        xp = _merge(xp, a_act, y_act.reshape(n_b * seq_len, ATT_DIM), None, gmix,
                    p_prompt[i].reshape(n_b * seq_len, PLE_DIM), *merge_w, tm=tm_merge, final=last)
        kv_p.append(newkv.reshape(n_b, seq_len, 4, N_KV, HEAD_DIM))
        win_p.append(win.reshape(n_b, seq_len, 2, N_KV, HEAD_DIM)[:, seq_len - min(WINDOW, seq_len):])
        conv_p.append(tail[:, carry_p - (CONV_W - 1):])

        prefix_s = state_conv[i].transpose(1, 0, 2).reshape(1, carry_s, CONV_DIM)
        (a_act, newkv, win, zb_act, gmix, tail, qp, gb) = _project(
            xs, g_row, w_perm, cos_s, sin_s, prefix_s, cw8,
            n_seq=1, rows_per_seq=rows_s, tm=rows_s, stride=n_dec, carry=carry_s, transposed=False, pos_per_seq=False)
        base = i * n_pool
        kc_flat, vc_flat = _compress(cache_t, page_table, lambda p: (base + p) * 2,
                                     *_compress_weights(*cmp_args, False), n_pages=pages_s, prompt=False)
        q5 = jnp.pad(qp.reshape(dec_len, n_dec, N_KV, GROUP, HEAD_DIM).astype(F32),
                     ((0, tpad), (0, 0), (0, 0), (0, 0), (0, 0)))
        qbd = q5.transpose(1, 4, 2, 3, 0)[:, None] * eye_g[None, :, None, :, None, None]
        qbd = qbd.reshape(n_dec, KV_DIM, LANE).astype(BF16)
        qbdt = q5.transpose(1, 2, 3, 0, 4)[:, :, :, :, None] * eye_g[None, :, None, None, :, None]
        qbdt = qbdt.reshape(n_dec, LANE, KV_DIM).astype(BF16)
        gates = gb.reshape(dec_len, n_dec, N_KV, LANE)[..., :3 * GROUP].reshape(dec_len, n_dec, N_KV, GROUP, 3)
        gates = jnp.pad(gates.transpose(1, 2, 3, 0, 4), ((0, 0), (0, 0), (0, 0), (0, tpad), (0, LANE - 3)))
        gates = gates.reshape(n_dec, LANE, LANE)
        newkv_b = newkv.reshape(dec_len, n_dec, 4 * KV_DIM).transpose(1, 0, 2)
        newwin_b = win.reshape(dec_len, n_dec, 2 * KV_DIM).transpose(1, 0, 2)
        pad_rows = lambda a: jnp.pad(a, ((0, 0), (0, tpad), (0, 0)))
        o_all = _attn_sample(cache_t, page_table, lambda p: (base + p) * 2 + 1, win_t, lambda b: i * n_dec + b, qbd,
                             qbdt, kc_flat, vc_flat, pad_rows(newkv_b), pad_rows(newwin_b), gates, n_pages=pages_s,
                             past_len=past_len, dec_len=dec_len)
        o6 = o_all.reshape(n_dec, N_KV, GROUP, SUBLANE, N_KV, HEAD_DIM)
        yb = jnp.stack([o6[:, g, :, :, g] for g in range(N_KV)], axis=1)
        yb = yb.transpose(3, 0, 1, 2, 4)[:dec_len].reshape(rows_s, ATT_DIM)
        xs = _merge(xs, a_act, yb, zb_act, gmix, p_sample[i].transpose(1, 0, 2).reshape(rows_s, PLE_DIM), *merge_w,
                    tm=rows_s, final=last)
        kv_s.append(newkv_b.reshape(n_dec, dec_len, 4, N_KV, HEAD_DIM))
        win_s.append(jnp.concatenate([state_win[i], newwin_b.reshape(n_dec, dec_len, 2, N_KV, HEAD_DIM)],
                                     axis=1)[:, -w_buf:])
        conv_s.append(tail.reshape(CONV_W - 1, n_dec, CONV_DIM).transpose(1, 0, 2))

    y_prompt = xp.reshape(n_b, seq_len, D_MODEL)
    y_sample = xs.reshape(dec_len, n_dec, D_MODEL).transpose(1, 0, 2)
    return (y_prompt, y_sample, jnp.stack(kv_p), jnp.stack(kv_s), jnp.stack(win_p), jnp.stack(win_s),
            jnp.stack(conv_p), jnp.stack(conv_s))
```

```python
import functools

import numpy as np
import jax
import jax.numpy as jnp
from jax import lax
from jax.experimental import pallas as pl
from jax.experimental.pallas import tpu as pltpu

D_MODEL = 1024
CONV_DIM = D_MODEL
CONV_W = 3
N_HEADS = 16
HEAD_DIM = 64
N_KV = 4
GROUP = N_HEADS // N_KV
ATT_DIM = N_HEADS * HEAD_DIM
KV_DIM = N_KV * HEAD_DIM
PAGE_SIZE = 128
CMP_STRIDE = 16
CMP_BLOCK = 2 * CMP_STRIDE
CMP_HID = 4 * HEAD_DIM
SLC_BLOCK = 64
TOP_N = 16
WINDOW = 512
PLE_DIM = 256
ROPE_THETA = 10000.0
RMS_EPS = 1e-6
NEG_INF = -1e30
FORCE_SCORE = 1e6

LANE = 128
SUBLANE = 8
VMEM_LIMIT = 56 * 1024 * 1024
KV_SLABS = KV_DIM // LANE
N_GATE_ROWS = 2 * SUBLANE
CHAIN_COLS = 4 * LANE

OFF_A = 0
OFF_Q = 4 * CONV_DIM
OFF_KV = OFF_Q + ATT_DIM
OFF_WIN = OFF_KV + 4 * KV_DIM
OFF_ZB = OFF_WIN + 2 * KV_DIM
OFF_GM = OFF_ZB + ATT_DIM
OFF_GB = OFF_GM + 2 * D_MODEL
N_PROJ = OFF_GB + N_KV * LANE
MAX_BLOCKS = LANE - HEAD_DIM

F32 = jnp.float32
BF16 = jnp.bfloat16
NT_DIMS = (((1,), (1,)), ((), ()))


def _params(n_axes):
    return pltpu.CompilerParams(dimension_semantics=("arbitrary",) * n_axes, vmem_limit_bytes=VMEM_LIMIT)


def _const_spec(shape):
    zeros = (0,) * len(shape)
    return pl.BlockSpec(shape, lambda *_: zeros, pipeline_mode=pl.Buffered(1))


def _layer_spec(shape, layer):
    index = (layer,) + (0,) * len(shape)
    return pl.BlockSpec((None,) + tuple(shape), lambda *_: index, pipeline_mode=pl.Buffered(1))


def _sigmoid(x):
    return 1.0 / (1.0 + jnp.exp(-x))


def _div(x, d):
    assert d & (d - 1) == 0
    return jnp.right_shift(x, d.bit_length() - 1)


def _mod(x, d):
    assert d & (d - 1) == 0
    return jnp.bitwise_and(x, d - 1)


def _iota(shape, dim):
    return lax.broadcasted_iota(jnp.int32, shape, dim)


def _rope_slab(x, cos, sin, lane):
    swapped = jnp.where(_mod(lane, HEAD_DIM) < HEAD_DIM // 2, pltpu.roll(x, LANE - HEAD_DIM // 2, 1),
                        pltpu.roll(x, HEAD_DIM // 2, 1))
    return x * cos + swapped * sin


def _overlap_t(n_blk, n_cmp):
    bj = _iota((n_blk, n_cmp), 0)
    bm = _iota((n_blk, n_cmp), 1)
    ratio = SLC_BLOCK // CMP_STRIDE
    return jnp.where((bm >= ratio * bj) & (bm <= ratio * bj + ratio) & (bm >= 1), 1.0, 0.0).astype(BF16)


def _proj_kernel(x_ref, g_ref, w_ref, cos_ref, sin_ref, pre_ref, cw_ref, *refs, tm, stride, carry, transposed,
                 n_carried):
    a_ref, kv_ref, win_ref, zb_ref, gm_ref, tail_ref, *rest = refs[n_carried:]
    if transposed:
        qt_ref, ks_ref, vst_ref, kw_ref, vwt_ref, gbt_ref, cmp_ref, u_scr = rest
    else:
        q_ref, gb_ref, u_scr = rest
    t = pl.program_id(1)
    x = x_ref[...]
    ms = jnp.mean(x * x, axis=-1, keepdims=True)
    h = (x * lax.rsqrt(ms + RMS_EPS) * g_ref[...]).astype(BF16)

    def seg(off, width):
        return jnp.dot(h, w_ref[:, off:off + width], preferred_element_type=F32)

    @pl.when(t == 0)
    def _():
        u_scr[0:carry, :] = pre_ref[0]

    u = seg(OFF_A + CONV_DIM, CONV_DIM) * seg(OFF_A + 2 * CONV_DIM, CONV_DIM)
    u_scr[carry:carry + tm, :] = u
    cw = cw_ref[...]
    conv = cw[0:1] * u_scr[carry - 2 * stride:carry - 2 * stride + tm, :]
    conv = conv + cw[1:2] * u_scr[carry - stride:carry - stride + tm, :]
    conv = conv + cw[2:3] * u
    ba = seg(OFF_A, CONV_DIM)
    za = seg(OFF_A + 3 * CONV_DIM, CONV_DIM)
    a_ref[...] = ((za * _sigmoid(za)) * (ba * conv)).astype(BF16)
    tail = u_scr[tm:tm + carry, :]
    tail_ref[0] = tail
    u_scr[0:carry, :] = tail

    lane = _iota((tm, LANE), 1)
    cos = cos_ref[...]
    sin = sin_ref[...]
    scale = HEAD_DIM ** -0.5 * (float(np.log2(np.e)) if transposed else 1.0)

    zq = seg(OFF_Q, ATT_DIM)
    for s in range(ATT_DIM // LANE):
        qs = _rope_slab(zq[:, s * LANE:(s + 1) * LANE], cos, sin, lane) * scale
        if transposed:
            qst = qs.T.astype(BF16)
            qt_ref[0, 2 * s] = qst[:HEAD_DIM]
            qt_ref[0, 2 * s + 1] = qst[HEAD_DIM:]
        else:
            q_ref[:, s * LANE:(s + 1) * LANE] = qs.astype(BF16)

    zkv = seg(OFF_KV, 4 * KV_DIM)
    zwin = seg(OFF_WIN, 2 * KV_DIM)
    kv_slabs = []
    for s in range(4 * KV_SLABS):
        v = zkv[:, s * LANE:(s + 1) * LANE]
        if (s // KV_SLABS) % 2 == 0:
            v = _rope_slab(v, cos, sin, lane)
        kv_ref[:, s * LANE:(s + 1) * LANE] = v
        kv_slabs.append(v)
    win_slabs = []
    for s in range(2 * KV_SLABS):
        v = zwin[:, s * LANE:(s + 1) * LANE]
        if s // KV_SLABS == 0:
            v = _rope_slab(v, cos, sin, lane)
        win_ref[:, s * LANE:(s + 1) * LANE] = v
        win_slabs.append(v)

    zb = seg(OFF_ZB, ATT_DIM)
    zb_ref[...] = zb * _sigmoid(zb)
    gm_ref[...] = _sigmoid(seg(OFF_GM, 2 * D_MODEL))
    gb = _sigmoid(seg(OFF_GB, N_KV * LANE))

    if not transposed:
        gb_ref[...] = gb
        return

    row = _iota((tm, LANE), 0)
    onehot = jnp.where(lane - HEAD_DIM == _div(t * tm + row, SLC_BLOCK), 1.0, 0.0)
    low = lane < HEAD_DIM
    ones_row = jnp.where(_iota((HEAD_DIM, tm), 0) == 0, 1.0, 0.0).astype(BF16)
    vs_t = [kv_slabs[3 * KV_SLABS + s].T.astype(BF16) for s in range(KV_SLABS)]
    vw_t = [win_slabs[KV_SLABS + s].T.astype(BF16) for s in range(KV_SLABS)]
    for s in range(2 * KV_SLABS):
        cmp_ref[s] = kv_slabs[s]
    for g in range(N_KV):
        s, hi = divmod(g, 2)

        def half(v):
            return v if hi == 0 else pltpu.roll(v, HEAD_DIM, 1)

        ks_ref[0, g] = jnp.where(low, half(kv_slabs[2 * KV_SLABS + s]), onehot).astype(BF16)
        kw_ref[0, g] = jnp.where(low, half(win_slabs[s]), 0.0).astype(BF16)
        vst_ref[0, g] = jnp.concatenate([vs_t[s][hi * HEAD_DIM:(hi + 1) * HEAD_DIM], ones_row], axis=0)
        vwt_ref[0, g] = jnp.concatenate([vw_t[s][hi * HEAD_DIM:(hi + 1) * HEAD_DIM], ones_row], axis=0)
        gbt_ref[0, g] = gb[:, g * LANE:(g + 1) * LANE].T[:N_GATE_ROWS]


def _project(x2d, ln_g, w_perm, cos, sin, prefix, conv_w8, carried, *, layer, n_seq, rows_per_seq, tm, stride, carry,
             transposed, pos_per_seq):
    nt = rows_per_seq // tm
    n_rows = n_seq * rows_per_seq
    depth = w_perm.shape[0]
    row_map = lambda b, t: (b * nt + t, 0)
    pos_map = (lambda b, t: (t, 0)) if pos_per_seq else row_map
    in_specs = [
        pl.BlockSpec((tm, D_MODEL), row_map),
        _layer_spec((1, D_MODEL), layer),
        _layer_spec((D_MODEL, N_PROJ), layer),
        pl.BlockSpec((tm, LANE), pos_map),
        pl.BlockSpec((tm, LANE), pos_map),
        pl.BlockSpec((1, carry, CONV_DIM), lambda b, t: (b, 0, 0)),
        _layer_spec((SUBLANE, CONV_DIM), layer),
    ]
    rows = lambda width, dtype: (jax.ShapeDtypeStruct((n_rows, width), dtype), pl.BlockSpec((tm, width), row_map))
    if transposed:
        keep = min(WINDOW, rows_per_seq)
        assert keep % tm == 0
        first_kept = (rows_per_seq - keep) // tm
        kv_out = (jax.ShapeDtypeStruct((depth, n_rows, 4 * KV_DIM), F32),
                  pl.BlockSpec((None, tm, 4 * KV_DIM), lambda b, t: (layer, b * nt + t, 0)))
        win_out = (jax.ShapeDtypeStruct((depth, n_seq, keep, 2 * KV_DIM), F32),
                   pl.BlockSpec((None, None, tm, 2 * KV_DIM),
                                lambda b, t: (layer, b, jnp.maximum(t - first_kept, 0), 0)))
    else:
        kv_out, win_out = rows(4 * KV_DIM, F32), rows(2 * KV_DIM, F32)
    outs = [
        rows(CONV_DIM, BF16),
        kv_out,
        win_out,
        rows(ATT_DIM, F32),
        rows(2 * D_MODEL, F32),
        (jax.ShapeDtypeStruct((n_seq, carry, CONV_DIM), F32),
         pl.BlockSpec((1, carry, CONV_DIM), lambda b, t: (b, 0, 0))),
    ]
    if transposed:
        tok_sub = (jax.ShapeDtypeStruct((n_seq, N_KV, rows_per_seq, LANE), BF16),
                   pl.BlockSpec((1, N_KV, tm, LANE), lambda b, t: (b, 0, t, 0)))
        tok_lane = lambda n, rows_, dtype: (jax.ShapeDtypeStruct((n_seq, n, rows_, rows_per_seq), dtype),
                                            pl.BlockSpec((1, n, rows_, tm), lambda b, t: (b, 0, 0, t)))
        outs += [tok_lane(N_HEADS, HEAD_DIM, BF16),
                 tok_sub, tok_lane(N_KV, LANE, BF16),
                 tok_sub, tok_lane(N_KV, LANE, BF16),
                 tok_lane(N_KV, N_GATE_ROWS, F32),
                 (jax.ShapeDtypeStruct((2 * KV_SLABS, n_rows, LANE), F32),
                  pl.BlockSpec((2 * KV_SLABS, tm, LANE), lambda b, t: (0, b * nt + t, 0)))]
    else:
        outs += [rows(ATT_DIM, BF16), rows(N_KV * LANE, F32)]
    args = [x2d, ln_g, w_perm, cos, sin, prefix, conv_w8]
    aliases = {}
    if carried is not None:
        assert transposed
        aliases = {len(args): 1, len(args) + 1: 2}
        args += list(carried)
        in_specs += [pl.BlockSpec(memory_space=pl.ANY)] * 2
    return pl.pallas_call(
        functools.partial(_proj_kernel, tm=tm, stride=stride, carry=carry, transposed=transposed,
                          n_carried=len(aliases)),
        grid=(n_seq, nt),
        in_specs=in_specs,
        out_specs=[o[1] for o in outs],
        out_shape=[o[0] for o in outs],
        scratch_shapes=[pltpu.VMEM((carry + tm, CONV_DIM), F32)],
        input_output_aliases=aliases,
        compiler_params=_params(2),
        name="project_prompt" if transposed else "project_sample",
    )(*args)


def _merge_kernel(x_ref, a_ref, gm_ref, p_ref, woa_ref, wob_ref, wo_ref, wple_ref, wpg_ref, fg_ref, y_ref, *rest,
                  y_needs_gate, final):
    br_a = jnp.dot(a_ref[...], woa_ref[...], preferred_element_type=F32)
    if y_needs_gate:
        zb_ref, o_ref = rest
        y = (zb_ref[...] * y_ref[...]).astype(BF16)
    else:
        (o_ref,) = rest
        y = y_ref[...]
    br_b = jnp.dot(y, wob_ref[...], preferred_element_type=F32)
    gm = gm_ref[...]
    mix = gm[:, :D_MODEL] * br_a + gm[:, D_MODEL:] * br_b
    x1 = x_ref[...] + jnp.dot(mix.astype(BF16), wo_ref[...], preferred_element_type=F32)
    gate = _sigmoid(jnp.dot(x1.astype(BF16), wpg_ref[...], preferred_element_type=F32))
    x2 = x1 + gate * jnp.dot(p_ref[...].astype(BF16), wple_ref[...], preferred_element_type=F32)
    if final:
        ms = jnp.mean(x2 * x2, axis=-1, keepdims=True)
        x2 = x2 * lax.rsqrt(ms + RMS_EPS) * fg_ref[...]
    o_ref[...] = x2


def _merge(x2d, a_act, y, zb_act, gmix, p2d, w_out_a, w_out_b, w_o, w_ple, w_pg, final_g, *, layer, tm, final):
    n_rows = x2d.shape[0]
    row = lambda width: pl.BlockSpec((tm, width), lambda t: (t, 0))
    y_args = (y,) if zb_act is None else (y, zb_act)
    return pl.pallas_call(
        functools.partial(_merge_kernel, y_needs_gate=zb_act is not None, final=final),
        grid=(n_rows // tm,),
        in_specs=[row(D_MODEL), row(CONV_DIM), row(2 * D_MODEL), row(PLE_DIM),
                  _layer_spec((CONV_DIM, D_MODEL), layer), _layer_spec((ATT_DIM, D_MODEL), layer),
                  _layer_spec((D_MODEL, D_MODEL), layer), _layer_spec((PLE_DIM, D_MODEL), layer),
                  _layer_spec((D_MODEL, D_MODEL), layer), _const_spec((1, D_MODEL))] + [row(ATT_DIM)] * len(y_args),
        out_specs=row(D_MODEL),
        out_shape=jax.ShapeDtypeStruct((n_rows, D_MODEL), F32),
        compiler_params=_params(1),
        name="merge_prompt" if zb_act is None else "merge_sample",
    )(x2d, a_act, gmix, p2d, w_out_a, w_out_b, w_o, w_ple, w_pg, final_g, *y_args)


def _compress_kernel(pt_ref, *refs, n_pages, prompt):
    pages = refs[:n_pages]
    pos_ref, w1_ref, w2_ref, ko_ref, vo_ref, rows_scr, x_scr, carry_scr = refs[n_pages:]
    st = pl.program_id(1)
    per_page = PAGE_SIZE // CMP_STRIDE
    n = n_pages * per_page
    low = _iota((n, LANE), 1) < HEAD_DIM

    @pl.when(st == 0)
    def _():
        carry_scr[...] = jnp.zeros_like(carry_scr)

    row0 = _iota((n, CMP_HID), 0) == 0
    for slot in range(2):
        for j in range(n_pages):
            for v in range(KV_SLABS):
                if prompt:
                    piece = pages[j][slot * KV_SLABS + v]
                else:
                    piece = pages[j][slot, v * LANE:(v + 1) * LANE, :].T
                rows_scr[slot, v, j * PAGE_SIZE:(j + 1) * PAGE_SIZE, :] = piece
        for pair in range(CMP_STRIDE // 2):
            for v in range(KV_SLABS):
                a = rows_scr[slot, v, pl.ds(2 * pair, n, stride=CMP_STRIDE), :]
                b = rows_scr[slot, v, pl.ds(2 * pair + 1, n, stride=CMP_STRIDE), :]
                x_scr[slot, 2 * v * n:(2 * v + 1) * n, pair * LANE:(pair + 1) * LANE] = jnp.where(
                    low, a, pltpu.roll(b, HEAD_DIM, 1))
                x_scr[slot, (2 * v + 1) * n:(2 * v + 2) * n, pair * LANE:(pair + 1) * LANE] = jnp.where(
                    low, pltpu.roll(a, HEAD_DIM, 1), b)
    for slot, out_ref in enumerate((ko_ref, vo_ref)):
        xs = x_scr[slot]
        pos = pos_ref[slot]
        f = jnp.dot((xs + pos[0:1]).astype(BF16), w1_ref[slot, 0], preferred_element_type=F32)
        s = jnp.dot((xs + pos[1:2]).astype(BF16), w1_ref[slot, 1], preferred_element_type=F32)
        acc = None
        for g in range(N_KV):
            fg = f[g * n:(g + 1) * n]
            prev = carry_scr[slot, g, SUBLANE - 1:SUBLANE, :]
            shifted = jnp.where(row0, prev, pltpu.roll(fg, 1, 0))
            carry_scr[slot, g] = fg[n - SUBLANE:n]
            pre = shifted + s[g * n:(g + 1) * n]
            hid = (pre * _sigmoid(pre)).astype(BF16)
            if prompt and slot == 0:
                out_ref[0, g] = jnp.dot(hid, w2_ref[0, 0, :, :LANE], preferred_element_type=F32).astype(BF16)
            elif prompt:
                out_ref[0, g] = lax.dot_general(w2_ref[1, 0, :LANE, :], hid, NT_DIMS,
                                                preferred_element_type=F32).astype(BF16)
            else:
                part = jnp.dot(hid, w2_ref[slot, g], preferred_element_type=F32)
                acc = part if acc is None else acc + part
        if not prompt:
            out_ref[0] = acc.astype(BF16)


def _compress(pages_arr, page_table, page_index, pos2, w1, w2, *, layer, n_pages, prompt):
    n_seq, pages_per_seq = page_table.shape
    n_steps = pages_per_seq // n_pages
    n = n_pages * (PAGE_SIZE // CMP_STRIDE)
    n_chunk = pages_per_seq * (PAGE_SIZE // CMP_STRIDE)
    flat = CMP_STRIDE * HEAD_DIM

    def page_spec(j):
        if prompt:
            return pl.BlockSpec((2 * KV_SLABS, PAGE_SIZE, LANE),
                                lambda b, s, pt: (0, page_index(pt[b, s * n_pages + j]), 0))
        return pl.BlockSpec((2, KV_DIM, PAGE_SIZE), lambda b, s, pt: (page_index(pt[b, s * n_pages + j]), 0, 0))

    if prompt:
        out_sds = [jax.ShapeDtypeStruct((n_seq, N_KV, n_chunk, LANE), BF16),
                   jax.ShapeDtypeStruct((n_seq, N_KV, LANE, n_chunk), BF16)]
        out_specs = [pl.BlockSpec((1, N_KV, n, LANE), lambda b, s, pt: (b, 0, s, 0)),
                     pl.BlockSpec((1, N_KV, LANE, n), lambda b, s, pt: (b, 0, 0, s))]
    else:
        out_sds = [jax.ShapeDtypeStruct((n_seq, n_chunk, KV_DIM), BF16)] * 2
        out_specs = [pl.BlockSpec((1, n, KV_DIM), lambda b, s, pt: (b, s, 0))] * 2
    grid_spec = pltpu.PrefetchScalarGridSpec(
        num_scalar_prefetch=1,
        grid=(n_seq, n_steps),
        in_specs=[page_spec(j) for j in range(n_pages)] + [_layer_spec(a.shape[1:], layer) for a in (pos2, w1, w2)],
        out_specs=out_specs,
        scratch_shapes=[pltpu.VMEM((2, KV_SLABS, n_pages * PAGE_SIZE, LANE), F32),
                        pltpu.VMEM((2, N_KV * n, flat), F32),
                        pltpu.VMEM((2, N_KV, SUBLANE, CMP_HID), F32)],
    )
    return pl.pallas_call(
        functools.partial(_compress_kernel, n_pages=n_pages, prompt=prompt),
        grid_spec=grid_spec,
        out_shape=out_sds,
        compiler_params=_params(2),
        name="compress_prompt" if prompt else "compress_sample",
    )(page_table, *([pages_arr] * n_pages), pos2, w1, w2)


def _select_bias(score_ref, cnt_ref, n_real, n_live, n_sel):
    score = score_ref[...]
    n_rows = -(-n_real // SUBLANE) * SUBLANE
    sub = _iota((SUBLANE, score.shape[1]), 0)
    cnt_ref[...] = jnp.zeros(score.shape, F32)

    def group(i0):
        cnt = cnt_ref[0:n_rows, :]
        for i in range(i0, min(i0 + SUBLANE, n_real)):
            si = score_ref[i:i + 1, :]
            pieces = []
            for r0 in range(0, n_rows, SUBLANE):
                sv = score[r0:r0 + SUBLANE]
                if r0 > i:
                    pieces.append(jnp.where(si >= sv, 1.0, 0.0))
                elif r0 + SUBLANE - 1 < i:
                    pieces.append(jnp.where(si > sv, 1.0, 0.0))
                else:
                    pieces.append(jnp.where(sub > i - r0, jnp.where(si >= sv, 1.0, 0.0), jnp.where(si > sv, 1.0, 0.0)))
            cnt = cnt + jnp.concatenate(pieces, axis=0)
        cnt_ref[0:n_rows, :] = cnt

    for i0 in range(0, n_real, SUBLANE):
        if n_live is None:
            group(i0)
        else:
            pl.when(i0 < n_live)(functools.partial(group, i0))
    return jnp.where((cnt_ref[...] < n_sel) & (score >= 0.0), 0.0, NEG_INF)


def _attn_prompt_kernel(qt_ref, gbt_ref, zb_ref, tri_ref, kc_ref, vct_ref, ks_ref, vst_ref, kw_ref, vwt_ref, o_ref,
                        score_scr, cnt_scr, m_scr, acc_scr, mx_scr, s_even_scr, s_odd_scr, *, tq, n_cmp, n_blk):
    qi = pl.program_id(2)
    s0 = qi * tq
    rows = GROUP * tq
    cw = min(rows, CHAIN_COLS)
    col_groups = [slice(c, c + cw) for c in range(0, rows, cw)]
    qpos = s0 + _mod(_iota((1, rows), 1), tq)

    def stack_q(lower):
        return jnp.concatenate([jnp.concatenate([qt_ref[0, r], lower], axis=0) for r in range(GROUP)], axis=1)

    q0 = stack_q(jnp.where(_iota((LANE - HEAD_DIM, tq), 0) == 0, NEG_INF, 0.0).astype(BF16))

    s_c = jnp.dot(kc_ref[0, 0], q0, preferred_element_type=F32)

    def window_scores(cs, first, n_keys):
        keys = kw_ref[0, 0, pl.ds(pl.multiple_of(s0 + first, tq), n_keys), :]
        return jnp.dot(keys, q0[:, cs], preferred_element_type=F32)

    w_parts = []
    for cs in col_groups:
        parts = [window_scores(cs, 0, tq) + tri_ref[0, :, cs]]
        if WINDOW > tq:
            parts.append(window_scores(cs, tq, WINDOW - tq))
        parts.append(window_scores(cs, WINDOW, tq) + tri_ref[1, :, cs])
        w_parts.append(parts)

    m_idx = _iota((n_cmp, 1), 0)
    mask_c = (m_idx >= 1) & (CMP_STRIDE * m_idx + CMP_STRIDE - 1 <= qpos)
    s_c = jnp.where(mask_c, s_c, NEG_INF)
    e_c = jnp.where(mask_c, jnp.exp2(s_c - jnp.max(s_c, axis=0, keepdims=True)), 0.0)
    p_c = e_c * (1.0 / jnp.maximum(jnp.sum(e_c, axis=0, keepdims=True), 1e-30))
    o_c = jnp.dot(vct_ref[0, 0], p_c.astype(BF16), preferred_element_type=F32)

    p_sum = p_c[:, 0:tq]
    for r in range(1, GROUP):
        p_sum = p_sum + p_c[:, r * tq:(r + 1) * tq]
    p_hi = p_sum.astype(BF16)
    p_lo = (p_sum - p_hi.astype(F32)).astype(BF16)
    ov_t = _overlap_t(MAX_BLOCKS, n_cmp)
    imp = jnp.dot(ov_t, p_hi, preferred_element_type=F32) + jnp.dot(ov_t, p_lo, preferred_element_type=F32)

    w_out = []
    for cs, parts in zip(col_groups, w_parts):
        m_w = functools.reduce(jnp.maximum, [jnp.max(s, axis=0, keepdims=True) for s in parts])
        p_w = jnp.concatenate([jnp.exp2(s - m_w).astype(BF16) for s in parts], axis=0)
        acc_w = jnp.dot(vwt_ref[0, 0, :, pl.ds(pl.multiple_of(s0, tq), WINDOW + tq)], p_w, preferred_element_type=F32)
        w_out.append(acc_w[:HEAD_DIM] / jnp.maximum(acc_w[HEAD_DIM:HEAD_DIM + 1], 1e-30))
    o_w = jnp.concatenate(w_out, axis=1)

    blk = _iota((MAX_BLOCKS, tq), 0)
    cur = _div(s0 + _iota((MAX_BLOCKS, tq), 1), SLC_BLOCK)
    forced = (blk == 0) | (blk == cur) | (blk == cur - 1)
    score = jnp.where(forced, FORCE_SCORE, imp)
    score_scr[...] = jnp.where(blk <= cur, score, -1.0)
    n_live = jnp.minimum(n_blk, (s0 + tq) // SLC_BLOCK)
    bias_t = _select_bias(score_scr, cnt_scr, n_blk, n_live, min(TOP_N, n_blk))
    qx = stack_q(bias_t.astype(BF16))

    m_scr[...] = jnp.full(m_scr.shape, NEG_INF, F32)
    acc_scr[...] = jnp.zeros(acc_scr.shape, F32)
    s_bufs = (s_even_scr, s_odd_scr)

    def scores(k0, slot, causal):
        keys = ks_ref[0, 0, pl.ds(pl.multiple_of(k0, tq), tq), :]
        prods = [jnp.dot(keys, qx[:, cs], preferred_element_type=F32) for cs in col_groups]
        for cs, s in zip(col_groups, prods):
            if causal:
                s = s + tri_ref[1, :, cs]
            s_bufs[slot][:, cs] = s
            mx_scr[slot, 0:1, cs] = jnp.max(s, axis=0, keepdims=True)

    def accumulate(k0, slot):
        vals_t = vst_ref[0, 0, :, pl.ds(pl.multiple_of(k0, tq), tq)]
        for cs in col_groups:
            m_old = m_scr[0:1, cs]
            m_new = jnp.maximum(m_old, mx_scr[slot, 0:1, cs])
            p = jnp.exp2(s_bufs[slot][:, cs] - m_new).astype(BF16)
            acc_scr[:, cs] = jnp.exp2(m_old - m_new) * acc_scr[:, cs] + jnp.dot(vals_t, p, preferred_element_type=F32)
            m_scr[0:1, cs] = m_new

    n_pairs = qi // 2
    scores(s0, 0, True)

    def pair(j, _):
        scores(2 * j * tq, 1, False)
        accumulate(jnp.where(j == 0, s0, (2 * j - 1) * tq), 0)
        scores((2 * j + 1) * tq, 0, False)
        accumulate(2 * j * tq, 1)
        return 0

    lax.fori_loop(0, n_pairs, pair, 0)
    pending = jnp.where(n_pairs == 0, s0, (2 * n_pairs - 1) * tq)

    @pl.when(qi - 2 * n_pairs == 1)
    def _():
        scores(s0 - tq, 1, False)
        accumulate(pending, 0)
        accumulate(s0 - tq, 1)

    @pl.when(qi - 2 * n_pairs == 0)
    def _():
        accumulate(pending, 0)

    o_s = acc_scr[:HEAD_DIM, :] / jnp.maximum(acc_scr[HEAD_DIM:HEAD_DIM + 1, :], 1e-30)

    gb = gbt_ref[0, 0]
    zb = zb_ref[0]
    heads = []
    for r in range(GROUP):
        rs = slice(r * tq, (r + 1) * tq)
        heads.append(gb[3 * r:3 * r + 1] * o_c[:HEAD_DIM, rs] + gb[3 * r + 1:3 * r + 2] * o_s[:, rs]
                     + gb[3 * r + 2:3 * r + 3] * o_w[:, rs])
    for pair in range(GROUP // 2):
        slab = jnp.concatenate([heads[2 * pair], heads[2 * pair + 1]], axis=0).T
        o_ref[0, :, pair * LANE:(pair + 1) * LANE] = (zb[:, pair * LANE:(pair + 1) * LANE] * slab).astype(BF16)


def _attn_prompt(q_t, gb_t, zb_act, kc_ext, vc_t, ks_ext, vs_t, kw_pad, vw_pad, *, tq):
    n_b, seq_len = zb_act.shape[0], zb_act.shape[1]
    n_cmp = kc_ext.shape[2]
    n_blk = seq_len // SLC_BLOCK
    head_w = GROUP * HEAD_DIM
    rows = GROUP * tq
    assert WINDOW % tq == 0 and rows % min(rows, CHAIN_COLS) == 0
    off = _iota((tq, rows), 0) - _mod(_iota((tq, rows), 1), tq)
    tri = jnp.stack([jnp.where(off > 0, 0.0, NEG_INF), jnp.where(off <= 0, 0.0, NEG_INF)]).astype(F32)
    tok_sub = lambda n: pl.BlockSpec((1, 1, n, LANE), lambda b, g, i: (b, g, 0, 0))
    tok_lane = lambda n: pl.BlockSpec((1, 1, LANE, n), lambda b, g, i: (b, g, 0, 0))
    return pl.pallas_call(
        functools.partial(_attn_prompt_kernel, tq=tq, n_cmp=n_cmp, n_blk=n_blk),
        grid=(n_b, N_KV, seq_len // tq),
        in_specs=[pl.BlockSpec((1, GROUP, HEAD_DIM, tq), lambda b, g, i: (b, g, 0, i)),
                  pl.BlockSpec((1, 1, N_GATE_ROWS, tq), lambda b, g, i: (b, g, 0, i)),
                  pl.BlockSpec((1, tq, head_w), lambda b, g, i: (b, i, g)),
                  _const_spec((2, tq, rows)),
                  tok_sub(n_cmp), tok_lane(n_cmp), tok_sub(seq_len), tok_lane(seq_len),
                  tok_sub(seq_len + WINDOW), tok_lane(seq_len + WINDOW)],
        out_specs=pl.BlockSpec((1, tq, head_w), lambda b, g, i: (b, i, g)),
        out_shape=jax.ShapeDtypeStruct((n_b, seq_len, ATT_DIM), BF16),
        scratch_shapes=[pltpu.VMEM((MAX_BLOCKS, tq), F32), pltpu.VMEM((MAX_BLOCKS, tq), F32),
                        pltpu.VMEM((SUBLANE, rows), F32),
                        pltpu.VMEM((LANE, rows), F32), pltpu.VMEM((2, SUBLANE, rows), F32),
                        pltpu.VMEM((tq, rows), F32), pltpu.VMEM((tq, rows), F32)],
        compiler_params=_params(3),
        name="attention_prompt",
    )(q_t, gb_t, zb_act, tri, kc_ext, vc_t, ks_ext, vs_t, kw_pad, vw_pad)


def _attn_sample_kernel(pt_ref, *refs, n_pages, past_len, dec_len, n_blk_pad):
    pages = refs[:n_pages]
    (qbd_ref, qbdt_ref, kc_ref, vc_ref, newkv_ref, win_ref, newwin_ref, gate_ref, o_ref,
     score_scr, cnt_scr, bias_scr, m_scr, l_scr, acc_scr, oc_scr) = refs[n_pages:]
    st = pl.program_id(1)
    n_steps = pl.num_programs(1)
    cols = LANE
    tslots = cols // N_HEADS
    n_cmp = kc_ref.shape[1]
    n_past_blk = past_len // SLC_BLOCK
    per_page = PAGE_SIZE // SLC_BLOCK
    qbdt = qbdt_ref[0]
    tok_c = _mod(_iota((cols, 1), 0), tslots)

    @pl.when(st == 0)
    def _():
        qpos = past_len + _mod(_iota((1, cols), 1), tslots)
        s_c = jnp.dot(kc_ref[0], qbd_ref[0], preferred_element_type=F32)
        m_idx = _iota((n_cmp, 1), 0)
        mask_c = (m_idx >= 1) & (CMP_STRIDE * m_idx + CMP_STRIDE - 1 <= qpos)
        s_c = jnp.where(mask_c, s_c, NEG_INF)
        e_c = jnp.where(mask_c, jnp.exp(s_c - jnp.max(s_c, axis=0, keepdims=True)), 0.0)
        p_c = e_c / jnp.maximum(jnp.sum(e_c, axis=0, keepdims=True), 1e-30)
        p_hi = p_c.astype(BF16)
        p_lo = (p_c - p_hi.astype(F32)).astype(BF16)
        oc_scr[...] = jnp.dot(p_c.T.astype(BF16), vc_ref[0], preferred_element_type=F32)
        ov_t = _overlap_t(n_blk_pad, n_cmp)
        imp1 = jnp.dot(ov_t, p_hi, preferred_element_type=F32) + jnp.dot(ov_t, p_lo, preferred_element_type=F32)
        ca = _iota((cols, cols), 0)
        cb = _iota((cols, cols), 1)
        per_group = GROUP * tslots
        same = jnp.where((_div(ca, per_group) == _div(cb, per_group)) & (_mod(ca, tslots) == _mod(cb, tslots)),
                         1.0, 0.0).astype(BF16)
        i_hi = imp1.astype(BF16)
        i_mid = (imp1 - i_hi.astype(F32)).astype(BF16)
        i_lo = (imp1 - i_hi.astype(F32) - i_mid.astype(F32)).astype(BF16)
        imp = (jnp.dot(i_hi, same, preferred_element_type=F32) + jnp.dot(i_mid, same, preferred_element_type=F32)
               + jnp.dot(i_lo, same, preferred_element_type=F32))
        blk = _iota((n_blk_pad, cols), 0)
        cur = _div(qpos, SLC_BLOCK)
        forced = (blk == 0) | (blk == cur) | (blk == cur - 1)
        score = jnp.where(forced, FORCE_SCORE, imp)
        score_scr[...] = jnp.where(blk <= cur, score, -1.0)
        n_blk = n_past_blk + 1
        bias_scr[...] = _select_bias(score_scr, cnt_scr, n_blk, None, min(TOP_N, n_blk)).T
        m_scr[...] = jnp.full(m_scr.shape, NEG_INF, F32)
        l_scr[...] = jnp.zeros(l_scr.shape, F32)
        acc_scr[...] = jnp.zeros(acc_scr.shape, F32)

    def update(s, pv_fn):
        m_old = m_scr[...]
        m_new = jnp.maximum(m_old, jnp.max(s, axis=1, keepdims=True))
        alpha = jnp.exp(m_old - m_new)
        p = jnp.exp(s - m_new)
        l_scr[...] = alpha * l_scr[...] + jnp.sum(p, axis=1, keepdims=True)
        acc_scr[...] = alpha * acc_scr[...] + pv_fn(p.astype(BF16))
        m_scr[...] = m_new

    bias_b = bias_scr[...].astype(BF16)
    blk0 = st * (n_pages * per_page)
    key_blk = blk0 + _div(_iota((n_blk_pad, n_pages * PAGE_SIZE), 1), SLC_BLOCK)
    expand = jnp.where(_iota((n_blk_pad, n_pages * PAGE_SIZE), 0) == key_blk, 1.0, 0.0).astype(BF16)
    keys_t = jnp.concatenate([pages[j][0].astype(BF16) for j in range(n_pages)], axis=1)
    vals_t = jnp.concatenate([pages[j][1].astype(BF16) for j in range(n_pages)], axis=1)
    s = (jnp.dot(qbdt, keys_t, preferred_element_type=F32) + jnp.dot(bias_b, expand, preferred_element_type=F32))
    update(s, lambda p: lax.dot_general(p, vals_t, NT_DIMS, preferred_element_type=F32))

    @pl.when(st == n_steps - 1)
    def _():
        def pad_rows(a):
            return jnp.concatenate([a, jnp.zeros((LANE - a.shape[0], a.shape[1]), a.dtype)], axis=0)

        key_i = _iota((1, LANE), 1)
        new = pad_rows(newkv_ref[0])
        s_n = lax.dot_general(qbdt, new[:, 2 * KV_DIM:3 * KV_DIM].astype(BF16), NT_DIMS, preferred_element_type=F32)
        s_n = s_n + bias_scr[:, n_past_blk:n_past_blk + 1]
        s_n = jnp.where((key_i <= tok_c) & (key_i < dec_len), s_n, NEG_INF)
        v_n = new[:, 3 * KV_DIM:].astype(BF16)
        update(s_n, lambda p: jnp.dot(p, v_n, preferred_element_type=F32))
        o_s = acc_scr[...] / jnp.maximum(l_scr[...], 1e-30)

        w_buf = win_ref.shape[2]
        neww = pad_rows(newwin_ref[0])
        s_w = jnp.concatenate(
            [jnp.dot(qbdt, win_ref[0].astype(BF16), preferred_element_type=F32),
             lax.dot_general(qbdt, neww[:, :KV_DIM].astype(BF16), NT_DIMS, preferred_element_type=F32)], axis=1)
        idx = _iota((1, w_buf + LANE), 1)
        kpos = past_len - w_buf + idx
        dist = past_len + tok_c - kpos
        mask_w = (dist >= 0) & (dist < WINDOW) & (idx < w_buf + dec_len) & (kpos >= 0)
        s_w = jnp.where(mask_w, s_w, NEG_INF)
        e_w = jnp.where(mask_w, jnp.exp(s_w - jnp.max(s_w, axis=1, keepdims=True)), 0.0)
        p_w = (e_w / jnp.maximum(jnp.sum(e_w, axis=1, keepdims=True), 1e-30)).astype(BF16)
        o_w = (lax.dot_general(p_w[:, :w_buf], win_ref[1].astype(BF16), NT_DIMS, preferred_element_type=F32)
               + jnp.dot(p_w[:, w_buf:], neww[:, KV_DIM:].astype(BF16), preferred_element_type=F32))
        gate = gate_ref[0]
        o_ref[0] = gate[:, 0:1] * oc_scr[...] + gate[:, 1:2] * o_s + gate[:, 2:3] * o_w


def _attn_sample(cache_t, page_table, page_index, win_t, win_index, qbd, qbdt, kc_flat, vc_flat, newkv8, newwin8,
                 gates, *, n_pages, past_len, dec_len):
    n_seq, pages_per_seq = page_table.shape
    n_steps = pages_per_seq // n_pages
    n_cmp = kc_flat.shape[1]
    n_blk_pad = -(-(past_len // SLC_BLOCK + 1) // LANE) * LANE
    w_buf = win_t.shape[2]

    def page_spec(j):
        return pl.BlockSpec((2, KV_DIM, PAGE_SIZE), lambda b, s, pt: (page_index(pt[b, s * n_pages + j]), 0, 0))

    per_seq = lambda shape: pl.BlockSpec((1,) + shape, lambda b, s, pt: (b,) + (0,) * len(shape))
    grid_spec = pltpu.PrefetchScalarGridSpec(
        num_scalar_prefetch=1,
        grid=(n_seq, n_steps),
        in_specs=[page_spec(j) for j in range(n_pages)] + [
            per_seq((KV_DIM, LANE)), per_seq((LANE, KV_DIM)), per_seq((n_cmp, KV_DIM)), per_seq((n_cmp, KV_DIM)),
            per_seq((SUBLANE, 4 * KV_DIM)),
            pl.BlockSpec((2, KV_DIM, w_buf), lambda b, s, pt: (win_index(b), 0, 0)),
            per_seq((SUBLANE, 2 * KV_DIM)), per_seq((LANE, LANE))],
        out_specs=per_seq((LANE, KV_DIM)),
        scratch_shapes=[pltpu.VMEM((n_blk_pad, LANE), F32), pltpu.VMEM((n_blk_pad, LANE), F32),
                        pltpu.VMEM((LANE, n_blk_pad), F32),
                        pltpu.VMEM((LANE, 1), F32), pltpu.VMEM((LANE, 1), F32),
                        pltpu.VMEM((LANE, KV_DIM), F32), pltpu.VMEM((LANE, KV_DIM), F32)],
    )
    return pl.pallas_call(
        functools.partial(_attn_sample_kernel, n_pages=n_pages, past_len=past_len, dec_len=dec_len,
                          n_blk_pad=n_blk_pad),
        grid_spec=grid_spec,
        out_shape=jax.ShapeDtypeStruct((n_seq, LANE, KV_DIM), F32),
        compiler_params=_params(2),
        name="attention_sample",
    )(page_table, *([cache_t] * n_pages), qbd, qbdt, kc_flat, vc_flat, newkv8, win_t, newwin8, gates)


def _permute_w_in(w):
    lead = w.shape[:-1]
    c = np.cumsum([0, 4 * CONV_DIM, ATT_DIM, 4 * KV_DIM, 2 * KV_DIM, 3 * N_HEADS, ATT_DIM, 2 * D_MODEL])
    wa, wq, wkv, wwin, wgb, wzb, wgm = (w[..., c[i]:c[i + 1]] for i in range(7))
    wgb = jnp.pad(wgb.reshape(lead + (N_KV, 3 * GROUP)), ((0, 0),) * (len(lead) + 1) + ((0, LANE - 3 * GROUP),))
    return jnp.concatenate([wa, wq, wkv, wwin, wzb, wgm, wgb.reshape(lead + (N_KV * LANE,))], axis=-1).astype(BF16)


def _rope_tables(pos):
    half = HEAD_DIM // 2
    inv = jnp.power(ROPE_THETA, -jnp.arange(half, dtype=F32) / half)
    ang = pos.astype(F32)[:, None] * inv[None, :]
    cos, sin = jnp.cos(ang), jnp.sin(ang)
    reps = LANE // HEAD_DIM
    return jnp.tile(jnp.concatenate([cos, cos], axis=1), (1, reps)), jnp.tile(jnp.concatenate([-sin, sin], axis=1), (1, reps))


def _compress_weights(pos_k, w1_k, w2_k, pos_v, w1_v, w2_v, grouped):
    depth = pos_k.shape[0]
    flat = CMP_STRIDE * HEAD_DIM
    pos2 = jnp.stack([jnp.pad(p.reshape(depth, 2, flat), ((0, 0), (0, SUBLANE - 2), (0, 0))) for p in (pos_k, pos_v)],
                     axis=1)
    w1 = jnp.stack([w.reshape(depth, 2, flat, CMP_HID) for w in (w1_k, w1_v)], axis=1).astype(BF16)
    if grouped:
        w2 = jnp.stack([jnp.pad(w2_k, ((0, 0), (0, 0), (0, CMP_HID - HEAD_DIM)))[:, None],
                        jnp.pad(w2_v.transpose(0, 2, 1), ((0, 0), (0, CMP_HID - HEAD_DIM), (0, 0)))[:, None]], axis=1)
    else:
        w2 = jnp.stack([jnp.stack([jnp.pad(w, ((0, 0), (0, 0), (g * HEAD_DIM, KV_DIM - (g + 1) * HEAD_DIM)))
                                   for g in range(N_KV)], axis=1) for w in (w2_k, w2_v)], axis=1)
    return pos2, w1, w2.astype(BF16)


def kernel(x_prompt, x_sample, cache_kv, state_win, state_conv, page_table, p_prompt, p_sample, ln_g, w_in, conv_w,
           cmp_pos_k, cmp_w1_k, cmp_w2_k, cmp_pos_v, cmp_w1_v, cmp_w2_v, w_out_a, w_out_b, w_o, w_ple, w_ple_gate,
           final_g):
    depth = ln_g.shape[0]
    n_b, seq_len, _ = x_prompt.shape
    n_dec, dec_len, _ = x_sample.shape
    pages_per_seq = page_table.shape[1]
    past_len = pages_per_seq * PAGE_SIZE
    n_pool = cache_kv.shape[1]
    w_buf = state_win.shape[2]

    tm_p = min(256, seq_len)
    tq = min(256, seq_len)
    tm_merge = min(512, seq_len)
    pages_p = min(16, seq_len // PAGE_SIZE)
    pages_s = min(16, pages_per_seq)
    rows_s = n_dec * dec_len
    carry_p = SUBLANE
    carry_s = (CONV_W - 1) * n_dec
    assert seq_len % PAGE_SIZE == 0 and seq_len // SLC_BLOCK <= MAX_BLOCKS and seq_len >= WINDOW + tq
    assert CONV_W - 1 <= dec_len <= SUBLANE and rows_s % SUBLANE == 0
    assert w_buf == min(WINDOW, past_len) and N_HEADS * SUBLANE == LANE

    cos_p, sin_p = _rope_tables(jnp.arange(seq_len, dtype=jnp.int32))
    pos_s = past_len + jnp.repeat(jnp.arange(dec_len, dtype=jnp.int32), n_dec)
    cos_s, sin_s = _rope_tables(pos_s)
    prompt_pages = jnp.arange(n_b * (seq_len // PAGE_SIZE), dtype=jnp.int32).reshape(n_b, seq_len // PAGE_SIZE)
    eye_g = jnp.eye(N_KV, dtype=F32)
    tpad = SUBLANE - dec_len
    win_pad_keys = (jnp.arange(LANE) == HEAD_DIM).astype(BF16)
    cache_t = cache_kv.transpose(0, 1, 3, 4, 5, 2).reshape(depth * n_pool * 4, KV_DIM, PAGE_SIZE)
    win_t = state_win.transpose(0, 1, 3, 4, 5, 2).reshape(depth * n_dec * 2, KV_DIM, w_buf)

    xp = x_prompt.reshape(n_b * seq_len, D_MODEL)
    xs = x_sample.transpose(1, 0, 2).reshape(rows_s, D_MODEL)
    kv_s, win_new_s, conv_p, conv_s = [], [], [], []
    carried = None
    w_perm = _permute_w_in(w_in)
    g_rows = ln_g.reshape(depth, 1, D_MODEL)
    cw8 = jnp.pad(conv_w, ((0, 0), (0, SUBLANE - CONV_W), (0, 0)))
    merge_w = (w_out_a.astype(BF16), w_out_b.astype(BF16), w_o.astype(BF16), w_ple.astype(BF16),
               w_ple_gate.astype(BF16), final_g.reshape(1, D_MODEL))
    cmp_args = (cmp_pos_k, cmp_w1_k, cmp_w2_k, cmp_pos_v, cmp_w1_v, cmp_w2_v)
    cmp_w_prompt = _compress_weights(*cmp_args, True)
    cmp_w_sample = _compress_weights(*cmp_args, False)
    zero_prefix = jnp.zeros((n_b, carry_p, CONV_DIM), F32)
    for i in range(depth):
        last = i == depth - 1

        (a_act, newkv_all, win_all, zb_act, gmix, tail, q_t, ks_ext, vs_t, kw_ext, vw_t, gb_t, cmp_rows) = _project(
            xp, g_rows, w_perm, cos_p, sin_p, zero_prefix, cw8, carried, layer=i,
            n_seq=n_b, rows_per_seq=seq_len, tm=tm_p, stride=1, carry=carry_p, transposed=True, pos_per_seq=True)
        carried = (newkv_all, win_all)
        kc_ext, vc_t = _compress(cmp_rows, prompt_pages, lambda p: p, *cmp_w_prompt, layer=i, n_pages=pages_p,
                                 prompt=True)
        kw_pad = jnp.concatenate([jnp.broadcast_to(win_pad_keys, (n_b, N_KV, WINDOW, LANE)), kw_ext], axis=2)
        vw_pad = jnp.pad(vw_t, ((0, 0), (0, 0), (0, 0), (WINDOW, 0)))
        y_act = _attn_prompt(q_t, gb_t, zb_act.reshape(n_b, seq_len, ATT_DIM), kc_ext, vc_t, ks_ext, vs_t, kw_pad,
                             vw_pad, tq=tq)
        xp = _merge(xp, a_act, y_act.reshape(n_b * seq_len, ATT_DIM), None, gmix,
                    p_prompt[i].reshape(n_b * seq_len, PLE_DIM), *merge_w, layer=i, tm=tm_merge, final=last)
        conv_p.append(tail[:, carry_p - (CONV_W - 1):])

        prefix_s = state_conv[i].transpose(1, 0, 2).reshape(1, carry_s, CONV_DIM)
        (a_act, newkv, win, zb_act, gmix, tail, qp, gb) = _project(
            xs, g_rows, w_perm, cos_s, sin_s, prefix_s, cw8, None, layer=i,
            n_seq=1, rows_per_seq=rows_s, tm=rows_s, stride=n_dec, carry=carry_s, transposed=False, pos_per_seq=False)
        base = i * n_pool
        kc_flat, vc_flat = _compress(cache_t, page_table, lambda p: (base + p) * 2, *cmp_w_sample, layer=i,
                                     n_pages=pages_s, prompt=False)
        q5 = jnp.pad(qp.reshape(dec_len, n_dec, N_KV, GROUP, HEAD_DIM).astype(F32),
                     ((0, tpad), (0, 0), (0, 0), (0, 0), (0, 0)))
        qbd = q5.transpose(1, 4, 2, 3, 0)[:, None] * eye_g[None, :, None, :, None, None]
        qbd = qbd.reshape(n_dec, KV_DIM, LANE).astype(BF16)
        qbdt = q5.transpose(1, 2, 3, 0, 4)[:, :, :, :, None] * eye_g[None, :, None, None, :, None]
        qbdt = qbdt.reshape(n_dec, LANE, KV_DIM).astype(BF16)
        gates = gb.reshape(dec_len, n_dec, N_KV, LANE)[..., :3 * GROUP].reshape(dec_len, n_dec, N_KV, GROUP, 3)
        gates = jnp.pad(gates.transpose(1, 2, 3, 0, 4), ((0, 0), (0, 0), (0, 0), (0, tpad), (0, LANE - 3)))
        gates = gates.reshape(n_dec, LANE, LANE)
        newkv_b = newkv.reshape(dec_len, n_dec, 4 * KV_DIM).transpose(1, 0, 2)
        newwin_b = win.reshape(dec_len, n_dec, 2 * KV_DIM).transpose(1, 0, 2)
        pad_rows = lambda a: jnp.pad(a, ((0, 0), (0, tpad), (0, 0)))
        o_all = _attn_sample(cache_t, page_table, lambda p: (base + p) * 2 + 1, win_t, lambda b: i * n_dec + b, qbd,
                             qbdt, kc_flat, vc_flat, pad_rows(newkv_b), pad_rows(newwin_b), gates, n_pages=pages_s,
                             past_len=past_len, dec_len=dec_len)
        o6 = o_all.reshape(n_dec, N_KV, GROUP, SUBLANE, N_KV, HEAD_DIM)
        yb = jnp.stack([o6[:, g, :, :, g] for g in range(N_KV)], axis=1)
        yb = yb.transpose(3, 0, 1, 2, 4)[:dec_len].reshape(rows_s, ATT_DIM)
        xs = _merge(xs, a_act, yb, zb_act, gmix, p_sample[i].transpose(1, 0, 2).reshape(rows_s, PLE_DIM), *merge_w,
                    layer=i, tm=rows_s, final=last)
        kv_s.append(newkv_b.reshape(n_dec, dec_len, 4, N_KV, HEAD_DIM))
        win_new_s.append(newwin_b.reshape(n_dec, dec_len, 2, N_KV, HEAD_DIM))
        conv_s.append(tail.reshape(CONV_W - 1, n_dec, CONV_DIM).transpose(1, 0, 2))

    y_prompt = xp.reshape(n_b, seq_len, D_MODEL)
    y_sample = xs.reshape(dec_len, n_dec, D_MODEL).transpose(1, 0, 2)
    newkv_all, win_all = carried
    kv_p = newkv_all.reshape(depth, n_b, seq_len, 4, N_KV, HEAD_DIM)
    win_p = win_all.reshape(depth, n_b, min(WINDOW, seq_len), 2, N_KV, HEAD_DIM)
    win_s = jnp.concatenate([state_win, jnp.stack(win_new_s)], axis=2)[:, :, -w_buf:]
    return (y_prompt, y_sample, kv_p, jnp.stack(kv_s), win_p, win_s, jnp.stack(conv_p), jnp.stack(conv_s))
```

```python
import functools

import numpy as np
import jax
import jax.numpy as jnp
from jax import lax
from jax.experimental import pallas as pl
from jax.experimental.pallas import tpu as pltpu

D_MODEL = 1024
CONV_DIM = D_MODEL
CONV_W = 3
N_HEADS = 16
HEAD_DIM = 64
N_KV = 4
GROUP = N_HEADS // N_KV
ATT_DIM = N_HEADS * HEAD_DIM
KV_DIM = N_KV * HEAD_DIM
PAGE_SIZE = 128
CMP_STRIDE = 16
CMP_BLOCK = 2 * CMP_STRIDE
CMP_HID = 4 * HEAD_DIM
SLC_BLOCK = 64
TOP_N = 16
WINDOW = 512
PLE_DIM = 256
ROPE_THETA = 10000.0
RMS_EPS = 1e-6
NEG_INF = -1e30
FORCE_SCORE = 1e6

LANE = 128
SUBLANE = 8
VMEM_LIMIT = 56 * 1024 * 1024
KV_SLABS = KV_DIM // LANE
N_GATE_ROWS = 2 * SUBLANE
CMP_DEPTH = 2 * LANE
CHAIN_COLS = 4 * LANE

OFF_A = 0
OFF_Q = 4 * CONV_DIM
OFF_KV = OFF_Q + ATT_DIM
OFF_WIN = OFF_KV + 4 * KV_DIM
OFF_ZB = OFF_WIN + 2 * KV_DIM
OFF_GM = OFF_ZB + ATT_DIM
OFF_GB = OFF_GM + 2 * D_MODEL
N_PROJ = OFF_GB + N_KV * LANE
MAX_BLOCKS = LANE - HEAD_DIM

F32 = jnp.float32
BF16 = jnp.bfloat16
NT_DIMS = (((1,), (1,)), ((), ()))


def _params(n_axes):
    return pltpu.CompilerParams(dimension_semantics=("arbitrary",) * n_axes, vmem_limit_bytes=VMEM_LIMIT)


def _const_spec(shape):
    zeros = (0,) * len(shape)
    return pl.BlockSpec(shape, lambda *_: zeros, pipeline_mode=pl.Buffered(1))


def _layer_spec(shape, layer):
    index = (layer,) + (0,) * len(shape)
    return pl.BlockSpec((None,) + tuple(shape), lambda *_: index, pipeline_mode=pl.Buffered(1))


def _sigmoid(x):
    return 1.0 / (1.0 + jnp.exp(-x))


def _div(x, d):
    assert d & (d - 1) == 0
    return jnp.right_shift(x, d.bit_length() - 1)


def _mod(x, d):
    assert d & (d - 1) == 0
    return jnp.bitwise_and(x, d - 1)


def _iota(shape, dim):
    return lax.broadcasted_iota(jnp.int32, shape, dim)


def _rope_slab(x, cos, sin, lane):
    swapped = jnp.where(_mod(lane, HEAD_DIM) < HEAD_DIM // 2, pltpu.roll(x, LANE - HEAD_DIM // 2, 1),
                        pltpu.roll(x, HEAD_DIM // 2, 1))
    return x * cos + swapped * sin


def _overlap_t(n_blk, n_cmp):
    bj = _iota((n_blk, n_cmp), 0)
    bm = _iota((n_blk, n_cmp), 1)
    ratio = SLC_BLOCK // CMP_STRIDE
    return jnp.where((bm >= ratio * bj) & (bm <= ratio * bj + ratio) & (bm >= 1), 1.0, 0.0).astype(BF16)


def _proj_kernel(x_ref, g_ref, w_ref, cos_ref, sin_ref, pre_ref, cw_ref, *refs, tm, stride, carry, transposed,
                 n_carried):
    a_ref, kv_ref, win_ref, zb_ref, gm_ref, tail_ref, *rest = refs[n_carried:]
    if transposed:
        qt_ref, ks_ref, vst_ref, kw_ref, vwt_ref, gbt_ref, cmp_ref, u_scr = rest
    else:
        q_ref, gb_ref, u_scr = rest
    t = pl.program_id(1)
    x = x_ref[...]
    ms = jnp.mean(x * x, axis=-1, keepdims=True)
    h = (x * lax.rsqrt(ms + RMS_EPS) * g_ref[...]).astype(BF16)

    def seg(off, width):
        return jnp.dot(h, w_ref[:, off:off + width], preferred_element_type=F32)

    @pl.when(t == 0)
    def _():
        u_scr[0:carry, :] = pre_ref[0]

    u = seg(OFF_A + CONV_DIM, CONV_DIM) * seg(OFF_A + 2 * CONV_DIM, CONV_DIM)
    u_scr[carry:carry + tm, :] = u
    cw = cw_ref[...]
    conv = cw[0:1] * u_scr[carry - 2 * stride:carry - 2 * stride + tm, :]
    conv = conv + cw[1:2] * u_scr[carry - stride:carry - stride + tm, :]
    conv = conv + cw[2:3] * u
    ba = seg(OFF_A, CONV_DIM)
    za = seg(OFF_A + 3 * CONV_DIM, CONV_DIM)
    a_ref[...] = ((za * _sigmoid(za)) * (ba * conv)).astype(BF16)
    tail = u_scr[tm:tm + carry, :]
    tail_ref[0] = tail
    u_scr[0:carry, :] = tail

    lane = _iota((tm, LANE), 1)
    cos = cos_ref[...]
    sin = sin_ref[...]
    scale = HEAD_DIM ** -0.5 * (float(np.log2(np.e)) if transposed else 1.0)

    zq = seg(OFF_Q, ATT_DIM)
    for s in range(ATT_DIM // LANE):
        qs = _rope_slab(zq[:, s * LANE:(s + 1) * LANE], cos, sin, lane) * scale
        if transposed:
            qst = qs.T.astype(BF16)
            qt_ref[0, 2 * s] = qst[:HEAD_DIM]
            qt_ref[0, 2 * s + 1] = qst[HEAD_DIM:]
        else:
            q_ref[:, s * LANE:(s + 1) * LANE] = qs.astype(BF16)

    zkv = seg(OFF_KV, 4 * KV_DIM)
    zwin = seg(OFF_WIN, 2 * KV_DIM)
    kv_slabs = []
    for s in range(4 * KV_SLABS):
        v = zkv[:, s * LANE:(s + 1) * LANE]
        if (s // KV_SLABS) % 2 == 0:
            v = _rope_slab(v, cos, sin, lane)
        if transposed:
            slot, pair = divmod(s, KV_SLABS)
            kv_ref[:, slot, 2 * pair, :] = v[:, :HEAD_DIM]
            kv_ref[:, slot, 2 * pair + 1, :] = v[:, HEAD_DIM:]
        else:
            kv_ref[:, s * LANE:(s + 1) * LANE] = v
        kv_slabs.append(v)
    win_slabs = []
    for s in range(2 * KV_SLABS):
        v = zwin[:, s * LANE:(s + 1) * LANE]
        if s // KV_SLABS == 0:
            v = _rope_slab(v, cos, sin, lane)
        win_ref[:, s * LANE:(s + 1) * LANE] = v
        win_slabs.append(v)

    zb = seg(OFF_ZB, ATT_DIM)
    zb_ref[...] = zb * _sigmoid(zb)
    gm_ref[...] = _sigmoid(seg(OFF_GM, 2 * D_MODEL))
    gb = _sigmoid(seg(OFF_GB, N_KV * LANE))

    if not transposed:
        gb_ref[...] = gb
        return

    row = _iota((tm, LANE), 0)
    onehot = jnp.where(lane - HEAD_DIM == _div(t * tm + row, SLC_BLOCK), 1.0, 0.0)
    low = lane < HEAD_DIM
    ones_row = jnp.where(_iota((HEAD_DIM, tm), 0) == 0, 1.0, 0.0).astype(BF16)
    vs_t = [kv_slabs[3 * KV_SLABS + s].T.astype(BF16) for s in range(KV_SLABS)]
    vw_t = [win_slabs[KV_SLABS + s].T.astype(BF16) for s in range(KV_SLABS)]
    for s in range(2 * KV_SLABS):
        cmp_ref[s] = kv_slabs[s]
    for g in range(N_KV):
        s, hi = divmod(g, 2)

        def half(v):
            return v if hi == 0 else pltpu.roll(v, HEAD_DIM, 1)

        ks_ref[0, g] = jnp.where(low, half(kv_slabs[2 * KV_SLABS + s]), onehot).astype(BF16)
        kw_ref[0, g] = jnp.where(low, half(win_slabs[s]), 0.0).astype(BF16)
        vst_ref[0, g] = jnp.concatenate([vs_t[s][hi * HEAD_DIM:(hi + 1) * HEAD_DIM], ones_row], axis=0)
        vwt_ref[0, g] = jnp.concatenate([vw_t[s][hi * HEAD_DIM:(hi + 1) * HEAD_DIM], ones_row], axis=0)
        gbt_ref[0, g] = gb[:, g * LANE:(g + 1) * LANE].T[:N_GATE_ROWS]


def _project(x2d, ln_g, w_perm, cos, sin, prefix, conv_w8, carried, *, layer, n_seq, rows_per_seq, tm, stride, carry,
             transposed, pos_per_seq):
    nt = rows_per_seq // tm
    n_rows = n_seq * rows_per_seq
    depth = w_perm.shape[0]
    row_map = lambda b, t: (b * nt + t, 0)
    pos_map = (lambda b, t: (t, 0)) if pos_per_seq else row_map
    in_specs = [
        pl.BlockSpec((tm, D_MODEL), row_map),
        _layer_spec((1, D_MODEL), layer),
        _layer_spec((D_MODEL, N_PROJ), layer),
        pl.BlockSpec((tm, LANE), pos_map),
        pl.BlockSpec((tm, LANE), pos_map),
        pl.BlockSpec((1, carry, CONV_DIM), lambda b, t: (b, 0, 0)),
        _layer_spec((SUBLANE, CONV_DIM), layer),
    ]
    rows = lambda width, dtype: (jax.ShapeDtypeStruct((n_rows, width), dtype), pl.BlockSpec((tm, width), row_map))
    if transposed:
        keep = min(WINDOW, rows_per_seq)
        assert keep % tm == 0
        first_kept = (rows_per_seq - keep) // tm
        kv_out = (jax.ShapeDtypeStruct((depth, n_seq, rows_per_seq, 4, N_KV, HEAD_DIM), F32),
                  pl.BlockSpec((None, None, tm, 4, N_KV, HEAD_DIM), lambda b, t: (layer, b, t, 0, 0, 0)))
        win_out = (jax.ShapeDtypeStruct((depth, n_seq, keep, 2 * KV_DIM), F32),
                   pl.BlockSpec((None, None, tm, 2 * KV_DIM),
                                lambda b, t: (layer, b, jnp.maximum(t - first_kept, 0), 0)))
    else:
        kv_out, win_out = rows(4 * KV_DIM, F32), rows(2 * KV_DIM, F32)
    outs = [
        rows(CONV_DIM, BF16),
        kv_out,
        win_out,
        rows(ATT_DIM, F32),
        rows(2 * D_MODEL, F32),
        (jax.ShapeDtypeStruct((n_seq, carry, CONV_DIM), F32),
         pl.BlockSpec((1, carry, CONV_DIM), lambda b, t: (b, 0, 0))),
    ]
    if transposed:
        tok_sub = (jax.ShapeDtypeStruct((n_seq, N_KV, rows_per_seq, LANE), BF16),
                   pl.BlockSpec((1, N_KV, tm, LANE), lambda b, t: (b, 0, t, 0)))
        tok_lane = lambda n, rows_, dtype: (jax.ShapeDtypeStruct((n_seq, n, rows_, rows_per_seq), dtype),
                                            pl.BlockSpec((1, n, rows_, tm), lambda b, t: (b, 0, 0, t)))
        outs += [tok_lane(N_HEADS, HEAD_DIM, BF16),
                 tok_sub, tok_lane(N_KV, LANE, BF16),
                 tok_sub, tok_lane(N_KV, LANE, BF16),
                 tok_lane(N_KV, N_GATE_ROWS, F32),
                 (jax.ShapeDtypeStruct((2 * KV_SLABS, n_rows, LANE), F32),
                  pl.BlockSpec((2 * KV_SLABS, tm, LANE), lambda b, t: (0, b * nt + t, 0)))]
    else:
        outs += [rows(ATT_DIM, BF16), rows(N_KV * LANE, F32)]
    args = [x2d, ln_g, w_perm, cos, sin, prefix, conv_w8]
    aliases = {}
    if carried is not None:
        assert transposed
        aliases = {len(args): 1, len(args) + 1: 2}
        args += list(carried)
        in_specs += [pl.BlockSpec(memory_space=pl.ANY)] * 2
    return pl.pallas_call(
        functools.partial(_proj_kernel, tm=tm, stride=stride, carry=carry, transposed=transposed,
                          n_carried=len(aliases)),
        grid=(n_seq, nt),
        in_specs=in_specs,
        out_specs=[o[1] for o in outs],
        out_shape=[o[0] for o in outs],
        scratch_shapes=[pltpu.VMEM((carry + tm, CONV_DIM), F32)],
        input_output_aliases=aliases,
        compiler_params=_params(2),
        name="project_prompt" if transposed else "project_sample",
    )(*args)


def _merge_kernel(x_ref, a_ref, gm_ref, p_ref, woa_ref, wob_ref, wo_ref, wple_ref, wpg_ref, fg_ref, y_ref, *rest,
                  y_needs_gate, final):
    br_a = jnp.dot(a_ref[...], woa_ref[...], preferred_element_type=F32)
    if y_needs_gate:
        zb_ref, o_ref = rest
        y = (zb_ref[...] * y_ref[...]).astype(BF16)
    else:
        (o_ref,) = rest
        y = y_ref[...]
    br_b = jnp.dot(y, wob_ref[...], preferred_element_type=F32)
    gm = gm_ref[...]
    mix = gm[:, :D_MODEL] * br_a + gm[:, D_MODEL:] * br_b
    x1 = x_ref[...] + jnp.dot(mix.astype(BF16), wo_ref[...], preferred_element_type=F32)
    gate = _sigmoid(jnp.dot(x1.astype(BF16), wpg_ref[...], preferred_element_type=F32))
    x2 = x1 + gate * jnp.dot(p_ref[...].astype(BF16), wple_ref[...], preferred_element_type=F32)
    if final:
        ms = jnp.mean(x2 * x2, axis=-1, keepdims=True)
        x2 = x2 * lax.rsqrt(ms + RMS_EPS) * fg_ref[...]
    o_ref[...] = x2


def _merge(x2d, a_act, y, zb_act, gmix, p2d, w_out_a, w_out_b, w_o, w_ple, w_pg, final_g, *, layer, tm, final):
    n_rows = x2d.shape[0]
    row = lambda width: pl.BlockSpec((tm, width), lambda t: (t, 0))
    y_args = (y,) if zb_act is None else (y, zb_act)
    return pl.pallas_call(
        functools.partial(_merge_kernel, y_needs_gate=zb_act is not None, final=final),
        grid=(n_rows // tm,),
        in_specs=[row(D_MODEL), row(CONV_DIM), row(2 * D_MODEL),
                  pl.BlockSpec((None, tm, PLE_DIM), lambda t: (layer, t, 0)),
                  _layer_spec((CONV_DIM, D_MODEL), layer), _layer_spec((ATT_DIM, D_MODEL), layer),
                  _layer_spec((D_MODEL, D_MODEL), layer), _layer_spec((PLE_DIM, D_MODEL), layer),
                  _layer_spec((D_MODEL, D_MODEL), layer), _const_spec((1, D_MODEL))] + [row(ATT_DIM)] * len(y_args),
        out_specs=row(D_MODEL),
        out_shape=jax.ShapeDtypeStruct((n_rows, D_MODEL), F32),
        compiler_params=_params(1),
        name="merge_prompt" if zb_act is None else "merge_sample",
    )(x2d, a_act, gmix, p2d, w_out_a, w_out_b, w_o, w_ple, w_pg, final_g, *y_args)


def _compress_kernel(pt_ref, *refs, n_pages, prompt):
    pages = refs[:n_pages]
    pos_ref, w1_ref, w2_ref, ko_ref, vo_ref, rows_scr, x_scr, carry_scr = refs[n_pages:]
    st = pl.program_id(1)
    per_page = PAGE_SIZE // CMP_STRIDE
    n = n_pages * per_page
    low = _iota((n, LANE), 1) < HEAD_DIM

    @pl.when(st == 0)
    def _():
        carry_scr[...] = jnp.zeros_like(carry_scr)

    row0 = _iota((n, CMP_HID), 0) == 0
    def relayout(slot):
        for j in range(n_pages):
            for v in range(KV_SLABS):
                if prompt:
                    piece = pages[j][slot * KV_SLABS + v]
                else:
                    piece = pages[j][slot, v * LANE:(v + 1) * LANE, :].T
                rows_scr[slot, v, j * PAGE_SIZE:(j + 1) * PAGE_SIZE, :] = piece
        for pair in range(CMP_STRIDE // 2):
            for v in range(KV_SLABS):
                a = rows_scr[slot, v, pl.ds(2 * pair, n, stride=CMP_STRIDE), :]
                b = rows_scr[slot, v, pl.ds(2 * pair + 1, n, stride=CMP_STRIDE), :]
                x_scr[slot, 2 * v * n:(2 * v + 1) * n, pair * LANE:(pair + 1) * LANE] = jnp.where(
                    low, a, pltpu.roll(b, HEAD_DIM, 1))
                x_scr[slot, (2 * v + 1) * n:(2 * v + 2) * n, pair * LANE:(pair + 1) * LANE] = jnp.where(
                    low, pltpu.roll(a, HEAD_DIM, 1), b)
    for slot, out_ref in enumerate((ko_ref, vo_ref)):
        relayout(slot)
        pos = pos_ref[slot]
        f = s = None
        for k0 in range(0, x_scr.shape[2], CMP_DEPTH):
            ks = slice(k0, k0 + CMP_DEPTH)
            xs = x_scr[slot, :, ks]
            f_part = jnp.dot((xs + pos[0:1, ks]).astype(BF16), w1_ref[slot, 0, ks, :], preferred_element_type=F32)
            s_part = jnp.dot((xs + pos[1:2, ks]).astype(BF16), w1_ref[slot, 1, ks, :], preferred_element_type=F32)
            f = f_part if f is None else f + f_part
            s = s_part if s is None else s + s_part
        acc = None
        for g in range(N_KV):
            fg = f[g * n:(g + 1) * n]
            prev = carry_scr[slot, g, SUBLANE - 1:SUBLANE, :]
            shifted = jnp.where(row0, prev, pltpu.roll(fg, 1, 0))
            carry_scr[slot, g] = fg[n - SUBLANE:n]
            pre = shifted + s[g * n:(g + 1) * n]
            hid = (pre * _sigmoid(pre)).astype(BF16)
            if prompt and slot == 0:
                out_ref[0, g] = jnp.dot(hid, w2_ref[0, 0, :, :LANE], preferred_element_type=F32).astype(BF16)
            elif prompt:
                out_ref[0, g] = lax.dot_general(w2_ref[1, 0, :LANE, :], hid, NT_DIMS,
                                                preferred_element_type=F32).astype(BF16)
            else:
                part = jnp.dot(hid, w2_ref[slot, g], preferred_element_type=F32)
                acc = part if acc is None else acc + part
        if not prompt:
            out_ref[0] = acc.astype(BF16)


def _compress(pages_arr, page_table, page_index, pos2, w1, w2, *, layer, n_pages, prompt):
    n_seq, pages_per_seq = page_table.shape
    n_steps = pages_per_seq // n_pages
    n = n_pages * (PAGE_SIZE // CMP_STRIDE)
    n_chunk = pages_per_seq * (PAGE_SIZE // CMP_STRIDE)
    flat = CMP_STRIDE * HEAD_DIM

    def page_spec(j):
        if prompt:
            return pl.BlockSpec((2 * KV_SLABS, PAGE_SIZE, LANE),
                                lambda b, s, pt: (0, page_index(pt[b, s * n_pages + j]), 0))
        return pl.BlockSpec((2, KV_DIM, PAGE_SIZE), lambda b, s, pt: (page_index(pt[b, s * n_pages + j]), 0, 0))

    if prompt:
        out_sds = [jax.ShapeDtypeStruct((n_seq, N_KV, n_chunk, LANE), BF16),
                   jax.ShapeDtypeStruct((n_seq, N_KV, LANE, n_chunk), BF16)]
        out_specs = [pl.BlockSpec((1, N_KV, n, LANE), lambda b, s, pt: (b, 0, s, 0)),
                     pl.BlockSpec((1, N_KV, LANE, n), lambda b, s, pt: (b, 0, 0, s))]
    else:
        out_sds = [jax.ShapeDtypeStruct((n_seq, n_chunk, KV_DIM), BF16)] * 2
        out_specs = [pl.BlockSpec((1, n, KV_DIM), lambda b, s, pt: (b, s, 0))] * 2
    grid_spec = pltpu.PrefetchScalarGridSpec(
        num_scalar_prefetch=1,
        grid=(n_seq, n_steps),
        in_specs=[page_spec(j) for j in range(n_pages)] + [_layer_spec(a.shape[1:], layer) for a in (pos2, w1, w2)],
        out_specs=out_specs,
        scratch_shapes=[pltpu.VMEM((2, KV_SLABS, n_pages * PAGE_SIZE, LANE), F32),
                        pltpu.VMEM((2, N_KV * n, flat), F32),
                        pltpu.VMEM((2, N_KV, SUBLANE, CMP_HID), F32)],
    )
    return pl.pallas_call(
        functools.partial(_compress_kernel, n_pages=n_pages, prompt=prompt),
        grid_spec=grid_spec,
        out_shape=out_sds,
        compiler_params=_params(2),
        name="compress_prompt" if prompt else "compress_sample",
    )(page_table, *([pages_arr] * n_pages), pos2, w1, w2)


def _select_bias(score_ref, cnt_ref, n_real, n_live, n_sel):
    score = score_ref[...]
    n_rows = -(-n_real // SUBLANE) * SUBLANE
    sub = _iota((SUBLANE, score.shape[1]), 0)
    cnt_ref[...] = jnp.zeros(score.shape, F32)

    def group(i0):
        cnt = cnt_ref[0:n_rows, :]
        for i in range(i0, min(i0 + SUBLANE, n_real)):
            si = score_ref[i:i + 1, :]
            pieces = []
            for r0 in range(0, n_rows, SUBLANE):
                sv = score[r0:r0 + SUBLANE]
                if r0 > i:
                    pieces.append(jnp.where(si >= sv, 1.0, 0.0))
                elif r0 + SUBLANE - 1 < i:
                    pieces.append(jnp.where(si > sv, 1.0, 0.0))
                else:
                    pieces.append(jnp.where(sub > i - r0, jnp.where(si >= sv, 1.0, 0.0), jnp.where(si > sv, 1.0, 0.0)))
            cnt = cnt + jnp.concatenate(pieces, axis=0)
        cnt_ref[0:n_rows, :] = cnt

    for i0 in range(0, n_real, SUBLANE):
        if n_live is None:
            group(i0)
        else:
            pl.when(i0 < n_live)(functools.partial(group, i0))
    return jnp.where((cnt_ref[...] < n_sel) & (score >= 0.0), 0.0, NEG_INF)


def _attn_prompt_kernel(qt_ref, gbt_ref, zb_ref, tri_ref, kc_ref, vct_ref, ks_ref, vst_ref, kw_ref, vwt_ref, o_ref,
                        score_scr, cnt_scr, m_scr, acc_scr, mx_scr, s_even_scr, s_odd_scr, *, tq, n_cmp, n_blk):
    qi = pl.program_id(2)
    s0 = qi * tq
    rows = GROUP * tq
    cw = min(rows, CHAIN_COLS)
    col_groups = [slice(c, c + cw) for c in range(0, rows, cw)]
    qpos = s0 + _mod(_iota((1, rows), 1), tq)

    def stack_q(lower):
        return jnp.concatenate([jnp.concatenate([qt_ref[0, r], lower], axis=0) for r in range(GROUP)], axis=1)

    q0 = stack_q(jnp.where(_iota((LANE - HEAD_DIM, tq), 0) == 0, NEG_INF, 0.0).astype(BF16))

    s_c = jnp.dot(kc_ref[0, 0], q0, preferred_element_type=F32)

    def window_scores(cs, first, n_keys):
        keys = kw_ref[0, 0, pl.ds(pl.multiple_of(s0 + first, tq), n_keys), :]
        return jnp.dot(keys, q0[:, cs], preferred_element_type=F32)

    w_parts = []
    for cs in col_groups:
        parts = [window_scores(cs, 0, tq) + tri_ref[0, :, cs]]
        if WINDOW > tq:
            parts.append(window_scores(cs, tq, WINDOW - tq))
        parts.append(window_scores(cs, WINDOW, tq) + tri_ref[1, :, cs])
        w_parts.append(parts)

    m_idx = _iota((n_cmp, 1), 0)
    mask_c = (m_idx >= 1) & (CMP_STRIDE * m_idx + CMP_STRIDE - 1 <= qpos)
    s_c = jnp.where(mask_c, s_c, NEG_INF)
    e_c = jnp.where(mask_c, jnp.exp2(s_c - jnp.max(s_c, axis=0, keepdims=True)), 0.0)
    p_c = e_c * (1.0 / jnp.maximum(jnp.sum(e_c, axis=0, keepdims=True), 1e-30))
    o_c = jnp.dot(vct_ref[0, 0], p_c.astype(BF16), preferred_element_type=F32)

    p_sum = p_c[:, 0:tq]
    for r in range(1, GROUP):
        p_sum = p_sum + p_c[:, r * tq:(r + 1) * tq]
    p_hi = p_sum.astype(BF16)
    p_lo = (p_sum - p_hi.astype(F32)).astype(BF16)
    ov_t = _overlap_t(MAX_BLOCKS, n_cmp)
    imp = jnp.dot(ov_t, p_hi, preferred_element_type=F32) + jnp.dot(ov_t, p_lo, preferred_element_type=F32)

    w_out = []
    for cs, parts in zip(col_groups, w_parts):
        m_w = functools.reduce(jnp.maximum, [jnp.max(s, axis=0, keepdims=True) for s in parts])
        p_w = jnp.concatenate([jnp.exp2(s - m_w).astype(BF16) for s in parts], axis=0)
        acc_w = jnp.dot(vwt_ref[0, 0, :, pl.ds(pl.multiple_of(s0, tq), WINDOW + tq)], p_w, preferred_element_type=F32)
        w_out.append(acc_w[:HEAD_DIM] / jnp.maximum(acc_w[HEAD_DIM:HEAD_DIM + 1], 1e-30))
    o_w = jnp.concatenate(w_out, axis=1)

    blk = _iota((MAX_BLOCKS, tq), 0)
    cur = _div(s0 + _iota((MAX_BLOCKS, tq), 1), SLC_BLOCK)
    forced = (blk == 0) | (blk == cur) | (blk == cur - 1)
    score = jnp.where(forced, FORCE_SCORE, imp)
    score_scr[...] = jnp.where(blk <= cur, score, -1.0)
    n_live = jnp.minimum(n_blk, (s0 + tq) // SLC_BLOCK)
    bias_t = _select_bias(score_scr, cnt_scr, n_blk, n_live, min(TOP_N, n_blk))
    qx = stack_q(bias_t.astype(BF16))

    m_scr[...] = jnp.full(m_scr.shape, NEG_INF, F32)
    acc_scr[...] = jnp.zeros(acc_scr.shape, F32)
    s_bufs = (s_even_scr, s_odd_scr)

    def scores(k0, slot, causal):
        keys = ks_ref[0, 0, pl.ds(pl.multiple_of(k0, tq), tq), :]
        prods = [jnp.dot(keys, qx[:, cs], preferred_element_type=F32) for cs in col_groups]
        for cs, s in zip(col_groups, prods):
            if causal:
                s = s + tri_ref[1, :, cs]
            s_bufs[slot][:, cs] = s
            mx_scr[slot, 0:1, cs] = jnp.max(s, axis=0, keepdims=True)

    def accumulate(k0, slot):
        vals_t = vst_ref[0, 0, :, pl.ds(pl.multiple_of(k0, tq), tq)]
        for cs in col_groups:
            m_old = m_scr[0:1, cs]
            m_new = jnp.maximum(m_old, mx_scr[slot, 0:1, cs])
            p = jnp.exp2(s_bufs[slot][:, cs] - m_new).astype(BF16)
            acc_scr[:, cs] = jnp.exp2(m_old - m_new) * acc_scr[:, cs] + jnp.dot(vals_t, p, preferred_element_type=F32)
            m_scr[0:1, cs] = m_new

    n_pairs = qi // 2
    scores(s0, 0, True)

    def pair(j, _):
        scores(2 * j * tq, 1, False)
        accumulate(jnp.where(j == 0, s0, (2 * j - 1) * tq), 0)
        scores((2 * j + 1) * tq, 0, False)
        accumulate(2 * j * tq, 1)
        return 0

    lax.fori_loop(0, n_pairs, pair, 0)
    pending = jnp.where(n_pairs == 0, s0, (2 * n_pairs - 1) * tq)

    @pl.when(qi - 2 * n_pairs == 1)
    def _():
        scores(s0 - tq, 1, False)
        accumulate(pending, 0)
        accumulate(s0 - tq, 1)

    @pl.when(qi - 2 * n_pairs == 0)
    def _():
        accumulate(pending, 0)

    o_s = acc_scr[:HEAD_DIM, :] / jnp.maximum(acc_scr[HEAD_DIM:HEAD_DIM + 1, :], 1e-30)

    gb = gbt_ref[0, 0]
    zb = zb_ref[0]
    heads = []
    for r in range(GROUP):
        rs = slice(r * tq, (r + 1) * tq)
        heads.append(gb[3 * r:3 * r + 1] * o_c[:HEAD_DIM, rs] + gb[3 * r + 1:3 * r + 2] * o_s[:, rs]
                     + gb[3 * r + 2:3 * r + 3] * o_w[:, rs])
    for pair in range(GROUP // 2):
        slab = jnp.concatenate([heads[2 * pair], heads[2 * pair + 1]], axis=0).T
        o_ref[0, :, pair * LANE:(pair + 1) * LANE] = (zb[:, pair * LANE:(pair + 1) * LANE] * slab).astype(BF16)


def _attn_prompt(q_t, gb_t, zb_act, kc_ext, vc_t, ks_ext, vs_t, kw_pad, vw_pad, *, tq):
    n_b, seq_len = zb_act.shape[0], zb_act.shape[1]
    n_cmp = kc_ext.shape[2]
    n_blk = seq_len // SLC_BLOCK
    head_w = GROUP * HEAD_DIM
    rows = GROUP * tq
    assert WINDOW % tq == 0 and rows % min(rows, CHAIN_COLS) == 0
    off = _iota((tq, rows), 0) - _mod(_iota((tq, rows), 1), tq)
    tri = jnp.stack([jnp.where(off > 0, 0.0, NEG_INF), jnp.where(off <= 0, 0.0, NEG_INF)]).astype(F32)
    tok_sub = lambda n: pl.BlockSpec((1, 1, n, LANE), lambda b, g, i: (b, g, 0, 0))
    tok_lane = lambda n: pl.BlockSpec((1, 1, LANE, n), lambda b, g, i: (b, g, 0, 0))
    return pl.pallas_call(
        functools.partial(_attn_prompt_kernel, tq=tq, n_cmp=n_cmp, n_blk=n_blk),
        grid=(n_b, N_KV, seq_len // tq),
        in_specs=[pl.BlockSpec((1, GROUP, HEAD_DIM, tq), lambda b, g, i: (b, g, 0, i)),
                  pl.BlockSpec((1, 1, N_GATE_ROWS, tq), lambda b, g, i: (b, g, 0, i)),
                  pl.BlockSpec((1, tq, head_w), lambda b, g, i: (b, i, g)),
                  _const_spec((2, tq, rows)),
                  tok_sub(n_cmp), tok_lane(n_cmp), tok_sub(seq_len), tok_lane(seq_len),
                  tok_sub(seq_len + WINDOW), tok_lane(seq_len + WINDOW)],
        out_specs=pl.BlockSpec((1, tq, head_w), lambda b, g, i: (b, i, g)),
        out_shape=jax.ShapeDtypeStruct((n_b, seq_len, ATT_DIM), BF16),
        scratch_shapes=[pltpu.VMEM((MAX_BLOCKS, tq), F32), pltpu.VMEM((MAX_BLOCKS, tq), F32),
                        pltpu.VMEM((SUBLANE, rows), F32),
                        pltpu.VMEM((LANE, rows), F32), pltpu.VMEM((2, SUBLANE, rows), F32),
                        pltpu.VMEM((tq, rows), F32), pltpu.VMEM((tq, rows), F32)],
        compiler_params=_params(3),
        name="attention_prompt",
    )(q_t, gb_t, zb_act, tri, kc_ext, vc_t, ks_ext, vs_t, kw_pad, vw_pad)


def _attn_sample_kernel(pt_ref, *refs, n_pages, past_len, dec_len, n_blk_pad):
    pages = refs[:n_pages]
    (qbd_ref, qbdt_ref, kc_ref, vc_ref, newkv_ref, win_ref, newwin_ref, gate_ref, o_ref,
     score_scr, cnt_scr, bias_scr, m_scr, l_scr, acc_scr, oc_scr) = refs[n_pages:]
    st = pl.program_id(1)
    n_steps = pl.num_programs(1)
    cols = LANE
    tslots = cols // N_HEADS
    n_cmp = kc_ref.shape[1]
    n_past_blk = past_len // SLC_BLOCK
    per_page = PAGE_SIZE // SLC_BLOCK
    qbdt = qbdt_ref[0]
    tok_c = _mod(_iota((cols, 1), 0), tslots)

    @pl.when(st == 0)
    def _():
        qpos = past_len + _mod(_iota((1, cols), 1), tslots)
        s_c = jnp.dot(kc_ref[0], qbd_ref[0], preferred_element_type=F32)
        m_idx = _iota((n_cmp, 1), 0)
        mask_c = (m_idx >= 1) & (CMP_STRIDE * m_idx + CMP_STRIDE - 1 <= qpos)
        s_c = jnp.where(mask_c, s_c, NEG_INF)
        e_c = jnp.where(mask_c, jnp.exp(s_c - jnp.max(s_c, axis=0, keepdims=True)), 0.0)
        p_c = e_c / jnp.maximum(jnp.sum(e_c, axis=0, keepdims=True), 1e-30)
        p_hi = p_c.astype(BF16)
        p_lo = (p_c - p_hi.astype(F32)).astype(BF16)
        oc_scr[...] = jnp.dot(p_c.T.astype(BF16), vc_ref[0], preferred_element_type=F32)
        ov_t = _overlap_t(n_blk_pad, n_cmp)
        imp1 = jnp.dot(ov_t, p_hi, preferred_element_type=F32) + jnp.dot(ov_t, p_lo, preferred_element_type=F32)
        ca = _iota((cols, cols), 0)
        cb = _iota((cols, cols), 1)
        per_group = GROUP * tslots
        same = jnp.where((_div(ca, per_group) == _div(cb, per_group)) & (_mod(ca, tslots) == _mod(cb, tslots)),
                         1.0, 0.0).astype(BF16)
        i_hi = imp1.astype(BF16)
        i_mid = (imp1 - i_hi.astype(F32)).astype(BF16)
        i_lo = (imp1 - i_hi.astype(F32) - i_mid.astype(F32)).astype(BF16)
        imp = (jnp.dot(i_hi, same, preferred_element_type=F32) + jnp.dot(i_mid, same, preferred_element_type=F32)
               + jnp.dot(i_lo, same, preferred_element_type=F32))
        blk = _iota((n_blk_pad, cols), 0)
        cur = _div(qpos, SLC_BLOCK)
        forced = (blk == 0) | (blk == cur) | (blk == cur - 1)
        score = jnp.where(forced, FORCE_SCORE, imp)
        score_scr[...] = jnp.where(blk <= cur, score, -1.0)
        n_blk = n_past_blk + 1
        bias_scr[...] = _select_bias(score_scr, cnt_scr, n_blk, None, min(TOP_N, n_blk)).T
        m_scr[...] = jnp.full(m_scr.shape, NEG_INF, F32)
        l_scr[...] = jnp.zeros(l_scr.shape, F32)
        acc_scr[...] = jnp.zeros(acc_scr.shape, F32)

    def update(s, pv_fn):
        m_old = m_scr[...]
        m_new = jnp.maximum(m_old, jnp.max(s, axis=1, keepdims=True))
        alpha = jnp.exp(m_old - m_new)
        p = jnp.exp(s - m_new)
        l_scr[...] = alpha * l_scr[...] + jnp.sum(p, axis=1, keepdims=True)
        acc_scr[...] = alpha * acc_scr[...] + pv_fn(p.astype(BF16))
        m_scr[...] = m_new

    bias_b = bias_scr[...].astype(BF16)
    blk0 = st * (n_pages * per_page)
    key_blk = blk0 + _div(_iota((n_blk_pad, n_pages * PAGE_SIZE), 1), SLC_BLOCK)
    expand = jnp.where(_iota((n_blk_pad, n_pages * PAGE_SIZE), 0) == key_blk, 1.0, 0.0).astype(BF16)
    keys_t = jnp.concatenate([pages[j][0].astype(BF16) for j in range(n_pages)], axis=1)
    vals_t = jnp.concatenate([pages[j][1].astype(BF16) for j in range(n_pages)], axis=1)
    s = (jnp.dot(qbdt, keys_t, preferred_element_type=F32) + jnp.dot(bias_b, expand, preferred_element_type=F32))
    update(s, lambda p: lax.dot_general(p, vals_t, NT_DIMS, preferred_element_type=F32))

    @pl.when(st == n_steps - 1)
    def _():
        def pad_rows(a):
            return jnp.concatenate([a, jnp.zeros((LANE - a.shape[0], a.shape[1]), a.dtype)], axis=0)

        key_i = _iota((1, LANE), 1)
        new = pad_rows(newkv_ref[0])
        s_n = lax.dot_general(qbdt, new[:, 2 * KV_DIM:3 * KV_DIM].astype(BF16), NT_DIMS, preferred_element_type=F32)
        s_n = s_n + bias_scr[:, n_past_blk:n_past_blk + 1]
        s_n = jnp.where((key_i <= tok_c) & (key_i < dec_len), s_n, NEG_INF)
        v_n = new[:, 3 * KV_DIM:].astype(BF16)
        update(s_n, lambda p: jnp.dot(p, v_n, preferred_element_type=F32))
        o_s = acc_scr[...] / jnp.maximum(l_scr[...], 1e-30)

        w_buf = win_ref.shape[2]
        neww = pad_rows(newwin_ref[0])
        s_w = jnp.concatenate(
            [jnp.dot(qbdt, win_ref[0].astype(BF16), preferred_element_type=F32),
             lax.dot_general(qbdt, neww[:, :KV_DIM].astype(BF16), NT_DIMS, preferred_element_type=F32)], axis=1)
        idx = _iota((1, w_buf + LANE), 1)
        kpos = past_len - w_buf + idx
        dist = past_len + tok_c - kpos
        mask_w = (dist >= 0) & (dist < WINDOW) & (idx < w_buf + dec_len) & (kpos >= 0)
        s_w = jnp.where(mask_w, s_w, NEG_INF)
        e_w = jnp.where(mask_w, jnp.exp(s_w - jnp.max(s_w, axis=1, keepdims=True)), 0.0)
        p_w = (e_w / jnp.maximum(jnp.sum(e_w, axis=1, keepdims=True), 1e-30)).astype(BF16)
        o_w = (lax.dot_general(p_w[:, :w_buf], win_ref[1].astype(BF16), NT_DIMS, preferred_element_type=F32)
               + jnp.dot(p_w[:, w_buf:], neww[:, KV_DIM:].astype(BF16), preferred_element_type=F32))
        gate = gate_ref[0]
        o_ref[0] = gate[:, 0:1] * oc_scr[...] + gate[:, 1:2] * o_s + gate[:, 2:3] * o_w


def _attn_sample(cache_t, page_table, page_index, win_t, win_index, qbd, qbdt, kc_flat, vc_flat, newkv8, newwin8,
                 gates, *, n_pages, past_len, dec_len):
    n_seq, pages_per_seq = page_table.shape
    n_steps = pages_per_seq // n_pages
    n_cmp = kc_flat.shape[1]
    n_blk_pad = -(-(past_len // SLC_BLOCK + 1) // LANE) * LANE
    w_buf = win_t.shape[2]

    def page_spec(j):
        return pl.BlockSpec((2, KV_DIM, PAGE_SIZE), lambda b, s, pt: (page_index(pt[b, s * n_pages + j]), 0, 0))

    per_seq = lambda shape: pl.BlockSpec((1,) + shape, lambda b, s, pt: (b,) + (0,) * len(shape))
    grid_spec = pltpu.PrefetchScalarGridSpec(
        num_scalar_prefetch=1,
        grid=(n_seq, n_steps),
        in_specs=[page_spec(j) for j in range(n_pages)] + [
            per_seq((KV_DIM, LANE)), per_seq((LANE, KV_DIM)), per_seq((n_cmp, KV_DIM)), per_seq((n_cmp, KV_DIM)),
            per_seq((SUBLANE, 4 * KV_DIM)),
            pl.BlockSpec((2, KV_DIM, w_buf), lambda b, s, pt: (win_index(b), 0, 0)),
            per_seq((SUBLANE, 2 * KV_DIM)), per_seq((LANE, LANE))],
        out_specs=per_seq((LANE, KV_DIM)),
        scratch_shapes=[pltpu.VMEM((n_blk_pad, LANE), F32), pltpu.VMEM((n_blk_pad, LANE), F32),
                        pltpu.VMEM((LANE, n_blk_pad), F32),
                        pltpu.VMEM((LANE, 1), F32), pltpu.VMEM((LANE, 1), F32),
                        pltpu.VMEM((LANE, KV_DIM), F32), pltpu.VMEM((LANE, KV_DIM), F32)],
    )
    return pl.pallas_call(
        functools.partial(_attn_sample_kernel, n_pages=n_pages, past_len=past_len, dec_len=dec_len,
                          n_blk_pad=n_blk_pad),
        grid_spec=grid_spec,
        out_shape=jax.ShapeDtypeStruct((n_seq, LANE, KV_DIM), F32),
        compiler_params=_params(2),
        name="attention_sample",
    )(page_table, *([cache_t] * n_pages), qbd, qbdt, kc_flat, vc_flat, newkv8, win_t, newwin8, gates)


def _permute_w_in(w):
    lead = w.shape[:-1]
    c = np.cumsum([0, 4 * CONV_DIM, ATT_DIM, 4 * KV_DIM, 2 * KV_DIM, 3 * N_HEADS, ATT_DIM, 2 * D_MODEL])
    wa, wq, wkv, wwin, wgb, wzb, wgm = (w[..., c[i]:c[i + 1]] for i in range(7))
    wgb = jnp.pad(wgb.reshape(lead + (N_KV, 3 * GROUP)), ((0, 0),) * (len(lead) + 1) + ((0, LANE - 3 * GROUP),))
    return jnp.concatenate([wa, wq, wkv, wwin, wzb, wgm, wgb.reshape(lead + (N_KV * LANE,))], axis=-1).astype(BF16)


def _rope_tables(pos):
    half = HEAD_DIM // 2
    inv = jnp.power(ROPE_THETA, -jnp.arange(half, dtype=F32) / half)
    ang = pos.astype(F32)[:, None] * inv[None, :]
    cos, sin = jnp.cos(ang), jnp.sin(ang)
    reps = LANE // HEAD_DIM
    return jnp.tile(jnp.concatenate([cos, cos], axis=1), (1, reps)), jnp.tile(jnp.concatenate([-sin, sin], axis=1), (1, reps))


def _compress_weights(pos_k, w1_k, w2_k, pos_v, w1_v, w2_v, grouped):
    depth = pos_k.shape[0]
    flat = CMP_STRIDE * HEAD_DIM
    pos2 = jnp.stack([jnp.pad(p.reshape(depth, 2, flat), ((0, 0), (0, SUBLANE - 2), (0, 0))) for p in (pos_k, pos_v)],
                     axis=1)
    w1 = jnp.stack([w.reshape(depth, 2, flat, CMP_HID) for w in (w1_k, w1_v)], axis=1).astype(BF16)
    if grouped:
        w2 = jnp.stack([jnp.pad(w2_k, ((0, 0), (0, 0), (0, CMP_HID - HEAD_DIM)))[:, None],
                        jnp.pad(w2_v.transpose(0, 2, 1), ((0, 0), (0, CMP_HID - HEAD_DIM), (0, 0)))[:, None]], axis=1)
    else:
        w2 = jnp.stack([jnp.stack([jnp.pad(w, ((0, 0), (0, 0), (g * HEAD_DIM, KV_DIM - (g + 1) * HEAD_DIM)))
                                   for g in range(N_KV)], axis=1) for w in (w2_k, w2_v)], axis=1)
    return pos2, w1, w2.astype(BF16)


def kernel(x_prompt, x_sample, cache_kv, state_win, state_conv, page_table, p_prompt, p_sample, ln_g, w_in, conv_w,
           cmp_pos_k, cmp_w1_k, cmp_w2_k, cmp_pos_v, cmp_w1_v, cmp_w2_v, w_out_a, w_out_b, w_o, w_ple, w_ple_gate,
           final_g):
    depth = ln_g.shape[0]
    n_b, seq_len, _ = x_prompt.shape
    n_dec, dec_len, _ = x_sample.shape
    pages_per_seq = page_table.shape[1]
    past_len = pages_per_seq * PAGE_SIZE
    n_pool = cache_kv.shape[1]
    w_buf = state_win.shape[2]

    tm_p = min(256, seq_len)
    tq = min(256, seq_len)
    tm_merge = min(512, seq_len)
    pages_p = min(16, seq_len // PAGE_SIZE)
    pages_s = min(16, pages_per_seq)
    rows_s = n_dec * dec_len
    carry_p = SUBLANE
    carry_s = (CONV_W - 1) * n_dec
    assert seq_len % PAGE_SIZE == 0 and seq_len // SLC_BLOCK <= MAX_BLOCKS and seq_len >= WINDOW + tq
    assert CONV_W - 1 <= dec_len <= SUBLANE and rows_s % SUBLANE == 0
    assert w_buf == min(WINDOW, past_len) and N_HEADS * SUBLANE == LANE

    cos_p, sin_p = _rope_tables(jnp.arange(seq_len, dtype=jnp.int32))
    pos_s = past_len + jnp.repeat(jnp.arange(dec_len, dtype=jnp.int32), n_dec)
    cos_s, sin_s = _rope_tables(pos_s)
    prompt_pages = jnp.arange(n_b * (seq_len // PAGE_SIZE), dtype=jnp.int32).reshape(n_b, seq_len // PAGE_SIZE)
    eye_g = jnp.eye(N_KV, dtype=F32)
    tpad = SUBLANE - dec_len
    win_pad_keys = (jnp.arange(LANE) == HEAD_DIM).astype(BF16)
    cache_t = cache_kv.transpose(0, 1, 3, 4, 5, 2).reshape(depth * n_pool * 4, KV_DIM, PAGE_SIZE)
    win_t = state_win.transpose(0, 1, 3, 4, 5, 2).reshape(depth * n_dec * 2, KV_DIM, w_buf)

    xp = x_prompt.reshape(n_b * seq_len, D_MODEL)
    xs = x_sample.transpose(1, 0, 2).reshape(rows_s, D_MODEL)
    kv_s, win_new_s, conv_p, conv_s = [], [], [], []
    carried = None
    w_perm = _permute_w_in(w_in)
    g_rows = ln_g.reshape(depth, 1, D_MODEL)
    cw8 = jnp.pad(conv_w, ((0, 0), (0, SUBLANE - CONV_W), (0, 0)))
    merge_w = (w_out_a.astype(BF16), w_out_b.astype(BF16), w_o.astype(BF16), w_ple.astype(BF16),
               w_ple_gate.astype(BF16), final_g.reshape(1, D_MODEL))
    cmp_args = (cmp_pos_k, cmp_w1_k, cmp_w2_k, cmp_pos_v, cmp_w1_v, cmp_w2_v)
    cmp_w_prompt = _compress_weights(*cmp_args, True)
    cmp_w_sample = _compress_weights(*cmp_args, False)
    zero_prefix = jnp.zeros((n_b, carry_p, CONV_DIM), F32)
    p_sample_rows = p_sample.transpose(0, 2, 1, 3).reshape(depth, rows_s, PLE_DIM)
    for i in range(depth):
        last = i == depth - 1

        (a_act, newkv_all, win_all, zb_act, gmix, tail, q_t, ks_ext, vs_t, kw_ext, vw_t, gb_t, cmp_rows) = _project(
            xp, g_rows, w_perm, cos_p, sin_p, zero_prefix, cw8, carried, layer=i,
            n_seq=n_b, rows_per_seq=seq_len, tm=tm_p, stride=1, carry=carry_p, transposed=True, pos_per_seq=True)
        carried = (newkv_all, win_all)
        kc_ext, vc_t = _compress(cmp_rows, prompt_pages, lambda p: p, *cmp_w_prompt, layer=i, n_pages=pages_p,
                                 prompt=True)
        kw_pad = jnp.concatenate([jnp.broadcast_to(win_pad_keys, (n_b, N_KV, WINDOW, LANE)), kw_ext], axis=2)
        vw_pad = jnp.pad(vw_t, ((0, 0), (0, 0), (0, 0), (WINDOW, 0)))
        y_act = _attn_prompt(q_t, gb_t, zb_act.reshape(n_b, seq_len, ATT_DIM), kc_ext, vc_t, ks_ext, vs_t, kw_pad,
                             vw_pad, tq=tq)
        xp = _merge(xp, a_act, y_act.reshape(n_b * seq_len, ATT_DIM), None, gmix,
                    p_prompt.reshape(depth, n_b * seq_len, PLE_DIM), *merge_w, layer=i, tm=tm_merge, final=last)
        conv_p.append(tail[:, carry_p - (CONV_W - 1):])

        prefix_s = state_conv[i].transpose(1, 0, 2).reshape(1, carry_s, CONV_DIM)
        (a_act, newkv, win, zb_act, gmix, tail, qp, gb) = _project(
            xs, g_rows, w_perm, cos_s, sin_s, prefix_s, cw8, None, layer=i,
            n_seq=1, rows_per_seq=rows_s, tm=rows_s, stride=n_dec, carry=carry_s, transposed=False, pos_per_seq=False)
        base = i * n_pool
        kc_flat, vc_flat = _compress(cache_t, page_table, lambda p: (base + p) * 2, *cmp_w_sample, layer=i,
                                     n_pages=pages_s, prompt=False)
        q5 = jnp.pad(qp.reshape(dec_len, n_dec, N_KV, GROUP, HEAD_DIM).astype(F32),
                     ((0, tpad), (0, 0), (0, 0), (0, 0), (0, 0)))
        qbd = q5.transpose(1, 4, 2, 3, 0)[:, None] * eye_g[None, :, None, :, None, None]
        qbd = qbd.reshape(n_dec, KV_DIM, LANE).astype(BF16)
        qbdt = q5.transpose(1, 2, 3, 0, 4)[:, :, :, :, None] * eye_g[None, :, None, None, :, None]
        qbdt = qbdt.reshape(n_dec, LANE, KV_DIM).astype(BF16)
        gates = gb.reshape(dec_len, n_dec, N_KV, LANE)[..., :3 * GROUP].reshape(dec_len, n_dec, N_KV, GROUP, 3)
        gates = jnp.pad(gates.transpose(1, 2, 3, 0, 4), ((0, 0), (0, 0), (0, 0), (0, tpad), (0, LANE - 3)))
        gates = gates.reshape(n_dec, LANE, LANE)
        newkv_b = newkv.reshape(dec_len, n_dec, 4 * KV_DIM).transpose(1, 0, 2)
        newwin_b = win.reshape(dec_len, n_dec, 2 * KV_DIM).transpose(1, 0, 2)
        pad_rows = lambda a: jnp.pad(a, ((0, 0), (0, tpad), (0, 0)))
        o_all = _attn_sample(cache_t, page_table, lambda p: (base + p) * 2 + 1, win_t, lambda b: i * n_dec + b, qbd,
                             qbdt, kc_flat, vc_flat, pad_rows(newkv_b), pad_rows(newwin_b), gates, n_pages=pages_s,
                             past_len=past_len, dec_len=dec_len)
        o6 = o_all.reshape(n_dec, N_KV, GROUP, SUBLANE, N_KV, HEAD_DIM)
        yb = jnp.stack([o6[:, g, :, :, g] for g in range(N_KV)], axis=1)
        yb = yb.transpose(3, 0, 1, 2, 4)[:dec_len].reshape(rows_s, ATT_DIM)
        xs = _merge(xs, a_act, yb, zb_act, gmix, p_sample_rows, *merge_w, layer=i, tm=rows_s, final=last)
        kv_s.append(newkv_b.reshape(n_dec, dec_len, 4, N_KV, HEAD_DIM))
        win_new_s.append(newwin_b.reshape(n_dec, dec_len, 2, N_KV, HEAD_DIM))
        conv_s.append(tail.reshape(CONV_W - 1, n_dec, CONV_DIM).transpose(1, 0, 2))

    y_prompt = xp.reshape(n_b, seq_len, D_MODEL)
    y_sample = xs.reshape(dec_len, n_dec, D_MODEL).transpose(1, 0, 2)
    newkv_all, win_all = carried
    kv_p = newkv_all
    win_p = win_all.reshape(depth, n_b, min(WINDOW, seq_len), 2, N_KV, HEAD_DIM)
    win_s = jnp.concatenate([state_win, jnp.stack(win_new_s)], axis=2)[:, :, -w_buf:]
    return (y_prompt, y_sample, kv_p, jnp.stack(kv_s), win_p, win_s, jnp.stack(conv_p), jnp.stack(conv_s))
```

```python
import functools

import numpy as np
import jax
import jax.numpy as jnp
from jax import lax
from jax.experimental import pallas as pl
from jax.experimental.pallas import tpu as pltpu

D_MODEL = 1024
CONV_DIM = D_MODEL
CONV_W = 3
N_HEADS = 16
HEAD_DIM = 64
N_KV = 4
GROUP = N_HEADS // N_KV
ATT_DIM = N_HEADS * HEAD_DIM
KV_DIM = N_KV * HEAD_DIM
PAGE_SIZE = 128
CMP_STRIDE = 16
CMP_BLOCK = 2 * CMP_STRIDE
CMP_HID = 4 * HEAD_DIM
SLC_BLOCK = 64
TOP_N = 16
WINDOW = 512
PLE_DIM = 256
ROPE_THETA = 10000.0
RMS_EPS = 1e-6
NEG_INF = -1e30
FORCE_SCORE = 1e6

LANE = 128
SUBLANE = 8
VMEM_LIMIT = 56 * 1024 * 1024
KV_SLABS = KV_DIM // LANE
N_GATE_ROWS = 2 * SUBLANE
CMP_DEPTH = 2 * LANE
CHAIN_COLS = 4 * LANE

OFF_A = 0
OFF_Q = 4 * CONV_DIM
OFF_KV = OFF_Q + ATT_DIM
OFF_WIN = OFF_KV + 4 * KV_DIM
OFF_ZB = OFF_WIN + 2 * KV_DIM
OFF_GM = OFF_ZB + ATT_DIM
OFF_GB = OFF_GM + 2 * D_MODEL
N_PROJ = OFF_GB + N_KV * LANE
MAX_BLOCKS = LANE - HEAD_DIM

F32 = jnp.float32
BF16 = jnp.bfloat16
NT_DIMS = (((1,), (1,)), ((), ()))


def _params(n_axes):
    return pltpu.CompilerParams(dimension_semantics=("arbitrary",) * n_axes, vmem_limit_bytes=VMEM_LIMIT)


def _const_spec(shape):
    zeros = (0,) * len(shape)
    return pl.BlockSpec(shape, lambda *_: zeros, pipeline_mode=pl.Buffered(1))


def _layer_spec(shape, layer):
    index = (layer,) + (0,) * len(shape)
    return pl.BlockSpec((None,) + tuple(shape), lambda *_: index, pipeline_mode=pl.Buffered(1))


def _sigmoid(x):
    return 1.0 / (1.0 + jnp.exp(-x))


def _div(x, d):
    assert d & (d - 1) == 0
    return jnp.right_shift(x, d.bit_length() - 1)


def _mod(x, d):
    assert d & (d - 1) == 0
    return jnp.bitwise_and(x, d - 1)


def _iota(shape, dim):
    return lax.broadcasted_iota(jnp.int32, shape, dim)


def _rope_slab(x, cos, sin, lane):
    swapped = jnp.where(_mod(lane, HEAD_DIM) < HEAD_DIM // 2, pltpu.roll(x, LANE - HEAD_DIM // 2, 1),
                        pltpu.roll(x, HEAD_DIM // 2, 1))
    return x * cos + swapped * sin


def _overlap_t(n_blk, n_cmp):
    bj = _iota((n_blk, n_cmp), 0)
    bm = _iota((n_blk, n_cmp), 1)
    ratio = SLC_BLOCK // CMP_STRIDE
    return jnp.where((bm >= ratio * bj) & (bm <= ratio * bj + ratio) & (bm >= 1), 1.0, 0.0).astype(BF16)


def _proj_kernel(x_ref, g_ref, w_ref, cos_ref, sin_ref, pre_ref, cw_ref, *refs, tm, stride, carry, transposed,
                 n_carried):
    a_ref, kv_ref, win_ref, zb_ref, gm_ref, tail_ref, *rest = refs[n_carried:]
    if transposed:
        qt_ref, ks_ref, vst_ref, kw_ref, vwt_ref, gbt_ref, cmp_ref, u_scr = rest
    else:
        q_ref, gb_ref, u_scr = rest
    t = pl.program_id(1)
    x = x_ref[...]
    ms = jnp.mean(x * x, axis=-1, keepdims=True)
    h = (x * lax.rsqrt(ms + RMS_EPS) * g_ref[...]).astype(BF16)

    def seg(off, width):
        return jnp.dot(h, w_ref[:, off:off + width], preferred_element_type=F32)

    @pl.when(t == 0)
    def _():
        u_scr[0:carry, :] = pre_ref[0]

    u = seg(OFF_A + CONV_DIM, CONV_DIM) * seg(OFF_A + 2 * CONV_DIM, CONV_DIM)
    u_scr[carry:carry + tm, :] = u
    cw = cw_ref[...]
    conv = cw[0:1] * u_scr[carry - 2 * stride:carry - 2 * stride + tm, :]
    conv = conv + cw[1:2] * u_scr[carry - stride:carry - stride + tm, :]
    conv = conv + cw[2:3] * u
    ba = seg(OFF_A, CONV_DIM)
    za = seg(OFF_A + 3 * CONV_DIM, CONV_DIM)
    a_ref[...] = ((za * _sigmoid(za)) * (ba * conv)).astype(BF16)
    tail = u_scr[tm:tm + carry, :]
    tail_ref[0] = tail
    u_scr[0:carry, :] = tail

    lane = _iota((tm, LANE), 1)
    cos = cos_ref[...]
    sin = sin_ref[...]
    scale = HEAD_DIM ** -0.5 * (float(np.log2(np.e)) if transposed else 1.0)

    zq = seg(OFF_Q, ATT_DIM)
    for s in range(ATT_DIM // LANE):
        qs = _rope_slab(zq[:, s * LANE:(s + 1) * LANE], cos, sin, lane) * scale
        if transposed:
            qst = qs.T.astype(BF16)
            qt_ref[0, 2 * s] = qst[:HEAD_DIM]
            qt_ref[0, 2 * s + 1] = qst[HEAD_DIM:]
        else:
            q_ref[:, s * LANE:(s + 1) * LANE] = qs.astype(BF16)

    zkv = seg(OFF_KV, 4 * KV_DIM)
    zwin = seg(OFF_WIN, 2 * KV_DIM)
    kv_slabs, kv_t = [], []
    for s in range(4 * KV_SLABS):
        v = zkv[:, s * LANE:(s + 1) * LANE]
        if (s // KV_SLABS) % 2 == 0:
            v = _rope_slab(v, cos, sin, lane)
        if transposed:
            kv_t.append(v.T)
            kv_ref[s * LANE:(s + 1) * LANE, :] = kv_t[-1]
        else:
            kv_ref[:, s * LANE:(s + 1) * LANE] = v
        kv_slabs.append(v)
    win_slabs, win_t = [], []
    for s in range(2 * KV_SLABS):
        v = zwin[:, s * LANE:(s + 1) * LANE]
        if s // KV_SLABS == 0:
            v = _rope_slab(v, cos, sin, lane)
        if transposed:
            win_t.append(v.T)
            win_ref[s * LANE:(s + 1) * LANE, :] = win_t[-1]
        else:
            win_ref[:, s * LANE:(s + 1) * LANE] = v
        win_slabs.append(v)

    zb = seg(OFF_ZB, ATT_DIM)
    zb_ref[...] = zb * _sigmoid(zb)
    gm_ref[...] = _sigmoid(seg(OFF_GM, 2 * D_MODEL))
    gb = _sigmoid(seg(OFF_GB, N_KV * LANE))

    if not transposed:
        gb_ref[...] = gb
        return

    row = _iota((tm, LANE), 0)
    onehot = jnp.where(lane - HEAD_DIM == _div(t * tm + row, SLC_BLOCK), 1.0, 0.0)
    low = lane < HEAD_DIM
    ones_row = jnp.where(_iota((HEAD_DIM, tm), 0) == 0, 1.0, 0.0).astype(BF16)
    vs_t = [kv_t[3 * KV_SLABS + s].astype(BF16) for s in range(KV_SLABS)]
    vw_t = [win_t[KV_SLABS + s].astype(BF16) for s in range(KV_SLABS)]
    for s in range(2 * KV_SLABS):
        cmp_ref[s] = kv_slabs[s]
    for g in range(N_KV):
        s, hi = divmod(g, 2)

        def half(v):
            return v if hi == 0 else pltpu.roll(v, HEAD_DIM, 1)

        ks_ref[0, g] = jnp.where(low, half(kv_slabs[2 * KV_SLABS + s]), onehot).astype(BF16)
        kw_ref[0, g] = jnp.where(low, half(win_slabs[s]), 0.0).astype(BF16)
        vst_ref[0, g] = jnp.concatenate([vs_t[s][hi * HEAD_DIM:(hi + 1) * HEAD_DIM], ones_row], axis=0)
        vwt_ref[0, g] = jnp.concatenate([vw_t[s][hi * HEAD_DIM:(hi + 1) * HEAD_DIM], ones_row], axis=0)
        gbt_ref[0, g] = gb[:, g * LANE:(g + 1) * LANE].T[:N_GATE_ROWS]


def _project(x2d, ln_g, w_perm, cos, sin, prefix, conv_w8, carried, *, layer, n_seq, rows_per_seq, tm, stride, carry,
             transposed, pos_per_seq):
    nt = rows_per_seq // tm
    n_rows = n_seq * rows_per_seq
    depth = w_perm.shape[0]
    row_map = lambda b, t: (b * nt + t, 0)
    pos_map = (lambda b, t: (t, 0)) if pos_per_seq else row_map
    in_specs = [
        pl.BlockSpec((tm, D_MODEL), row_map),
        _layer_spec((1, D_MODEL), layer),
        _layer_spec((D_MODEL, N_PROJ), layer),
        pl.BlockSpec((tm, LANE), pos_map),
        pl.BlockSpec((tm, LANE), pos_map),
        pl.BlockSpec((1, carry, CONV_DIM), lambda b, t: (b, 0, 0)),
        _layer_spec((SUBLANE, CONV_DIM), layer),
    ]
    rows = lambda width, dtype: (jax.ShapeDtypeStruct((n_rows, width), dtype), pl.BlockSpec((tm, width), row_map))
    if transposed:
        keep = min(WINDOW, rows_per_seq)
        assert keep % tm == 0
        first_kept = (rows_per_seq - keep) // tm
        kv_out = (jax.ShapeDtypeStruct((depth, n_seq, 4 * KV_DIM, rows_per_seq), F32),
                  pl.BlockSpec((None, None, 4 * KV_DIM, tm), lambda b, t: (layer, b, 0, t)))
        win_out = (jax.ShapeDtypeStruct((depth, n_seq, 2 * KV_DIM, keep), F32),
                   pl.BlockSpec((None, None, 2 * KV_DIM, tm),
                                lambda b, t: (layer, b, 0, jnp.maximum(t - first_kept, 0))))
    else:
        kv_out, win_out = rows(4 * KV_DIM, F32), rows(2 * KV_DIM, F32)
    outs = [
        rows(CONV_DIM, BF16),
        kv_out,
        win_out,
        rows(ATT_DIM, F32),
        rows(2 * D_MODEL, F32),
        (jax.ShapeDtypeStruct((n_seq, carry, CONV_DIM), F32),
         pl.BlockSpec((1, carry, CONV_DIM), lambda b, t: (b, 0, 0))),
    ]
    if transposed:
        tok_sub = (jax.ShapeDtypeStruct((n_seq, N_KV, rows_per_seq, LANE), BF16),
                   pl.BlockSpec((1, N_KV, tm, LANE), lambda b, t: (b, 0, t, 0)))
        tok_lane = lambda n, rows_, dtype: (jax.ShapeDtypeStruct((n_seq, n, rows_, rows_per_seq), dtype),
                                            pl.BlockSpec((1, n, rows_, tm), lambda b, t: (b, 0, 0, t)))
        outs += [tok_lane(N_HEADS, HEAD_DIM, BF16),
                 tok_sub, tok_lane(N_KV, LANE, BF16),
                 tok_sub, tok_lane(N_KV, LANE, BF16),
                 tok_lane(N_KV, N_GATE_ROWS, F32),
                 (jax.ShapeDtypeStruct((2 * KV_SLABS, n_rows, LANE), F32),
                  pl.BlockSpec((2 * KV_SLABS, tm, LANE), lambda b, t: (0, b * nt + t, 0)))]
    else:
        outs += [rows(ATT_DIM, BF16), rows(N_KV * LANE, F32)]
    args = [x2d, ln_g, w_perm, cos, sin, prefix, conv_w8]
    aliases = {}
    if carried is not None:
        assert transposed
        aliases = {len(args): 1, len(args) + 1: 2}
        args += list(carried)
        in_specs += [pl.BlockSpec(memory_space=pl.ANY)] * 2
    return pl.pallas_call(
        functools.partial(_proj_kernel, tm=tm, stride=stride, carry=carry, transposed=transposed,
                          n_carried=len(aliases)),
        grid=(n_seq, nt),
        in_specs=in_specs,
        out_specs=[o[1] for o in outs],
        out_shape=[o[0] for o in outs],
        scratch_shapes=[pltpu.VMEM((carry + tm, CONV_DIM), F32)],
        input_output_aliases=aliases,
        compiler_params=_params(2),
        name="project_prompt" if transposed else "project_sample",
    )(*args)


def _merge_kernel(x_ref, a_ref, gm_ref, p_ref, woa_ref, wob_ref, wo_ref, wple_ref, wpg_ref, fg_ref, y_ref, *rest,
                  y_needs_gate, final):
    br_a = jnp.dot(a_ref[...], woa_ref[...], preferred_element_type=F32)
    if y_needs_gate:
        zb_ref, o_ref = rest
        y = (zb_ref[...] * y_ref[...]).astype(BF16)
    else:
        (o_ref,) = rest
        y = y_ref[...]
    br_b = jnp.dot(y, wob_ref[...], preferred_element_type=F32)
    gm = gm_ref[...]
    mix = gm[:, :D_MODEL] * br_a + gm[:, D_MODEL:] * br_b
    x1 = x_ref[...] + jnp.dot(mix.astype(BF16), wo_ref[...], preferred_element_type=F32)
    gate = _sigmoid(jnp.dot(x1.astype(BF16), wpg_ref[...], preferred_element_type=F32))
    x2 = x1 + gate * jnp.dot(p_ref[...].astype(BF16), wple_ref[...], preferred_element_type=F32)
    if final:
        ms = jnp.mean(x2 * x2, axis=-1, keepdims=True)
        x2 = x2 * lax.rsqrt(ms + RMS_EPS) * fg_ref[...]
    o_ref[...] = x2


def _merge(x2d, a_act, y, zb_act, gmix, p2d, w_out_a, w_out_b, w_o, w_ple, w_pg, final_g, *, layer, tm, final):
    n_rows = x2d.shape[0]
    row = lambda width: pl.BlockSpec((tm, width), lambda t: (t, 0))
    y_args = (y,) if zb_act is None else (y, zb_act)
    return pl.pallas_call(
        functools.partial(_merge_kernel, y_needs_gate=zb_act is not None, final=final),
        grid=(n_rows // tm,),
        in_specs=[row(D_MODEL), row(CONV_DIM), row(2 * D_MODEL),
                  pl.BlockSpec((None, tm, PLE_DIM), lambda t: (layer, t, 0)),
                  _layer_spec((CONV_DIM, D_MODEL), layer), _layer_spec((ATT_DIM, D_MODEL), layer),
                  _layer_spec((D_MODEL, D_MODEL), layer), _layer_spec((PLE_DIM, D_MODEL), layer),
                  _layer_spec((D_MODEL, D_MODEL), layer), _const_spec((1, D_MODEL))] + [row(ATT_DIM)] * len(y_args),
        out_specs=row(D_MODEL),
        out_shape=jax.ShapeDtypeStruct((n_rows, D_MODEL), F32),
        compiler_params=_params(1),
        name="merge_prompt" if zb_act is None else "merge_sample",
    )(x2d, a_act, gmix, p2d, w_out_a, w_out_b, w_o, w_ple, w_pg, final_g, *y_args)


def _compress_kernel(pt_ref, *refs, n_pages, prompt):
    pages = refs[:n_pages]
    pos_ref, w1_ref, w2_ref, ko_ref, vo_ref, rows_scr, x_scr, carry_scr = refs[n_pages:]
    st = pl.program_id(1)
    per_page = PAGE_SIZE // CMP_STRIDE
    n = n_pages * per_page
    low = _iota((n, LANE), 1) < HEAD_DIM

    @pl.when(st == 0)
    def _():
        carry_scr[...] = jnp.zeros_like(carry_scr)

    row0 = _iota((n, CMP_HID), 0) == 0
    def relayout(slot):
        for j in range(n_pages):
            for v in range(KV_SLABS):
                if prompt:
                    piece = pages[j][slot * KV_SLABS + v]
                else:
                    piece = pages[j][slot, v * LANE:(v + 1) * LANE, :].T
                rows_scr[slot, v, j * PAGE_SIZE:(j + 1) * PAGE_SIZE, :] = piece
        for pair in range(CMP_STRIDE // 2):
            for v in range(KV_SLABS):
                a = rows_scr[slot, v, pl.ds(2 * pair, n, stride=CMP_STRIDE), :]
                b = rows_scr[slot, v, pl.ds(2 * pair + 1, n, stride=CMP_STRIDE), :]
                x_scr[slot, 2 * v * n:(2 * v + 1) * n, pair * LANE:(pair + 1) * LANE] = jnp.where(
                    low, a, pltpu.roll(b, HEAD_DIM, 1))
                x_scr[slot, (2 * v + 1) * n:(2 * v + 2) * n, pair * LANE:(pair + 1) * LANE] = jnp.where(
                    low, pltpu.roll(a, HEAD_DIM, 1), b)
    for slot, out_ref in enumerate((ko_ref, vo_ref)):
        relayout(slot)
        pos = pos_ref[slot]
        f = s = None
        for k0 in range(0, x_scr.shape[2], CMP_DEPTH):
            ks = slice(k0, k0 + CMP_DEPTH)
            xs = x_scr[slot, :, ks]
            f_part = jnp.dot((xs + pos[0:1, ks]).astype(BF16), w1_ref[slot, 0, ks, :], preferred_element_type=F32)
            s_part = jnp.dot((xs + pos[1:2, ks]).astype(BF16), w1_ref[slot, 1, ks, :], preferred_element_type=F32)
            f = f_part if f is None else f + f_part
            s = s_part if s is None else s + s_part
        acc = None
        for g in range(N_KV):
            fg = f[g * n:(g + 1) * n]
            prev = carry_scr[slot, g, SUBLANE - 1:SUBLANE, :]
            shifted = jnp.where(row0, prev, pltpu.roll(fg, 1, 0))
            carry_scr[slot, g] = fg[n - SUBLANE:n]
            pre = shifted + s[g * n:(g + 1) * n]
            hid = (pre * _sigmoid(pre)).astype(BF16)
            if prompt and slot == 0:
                out_ref[0, g] = jnp.dot(hid, w2_ref[0, 0, :, :LANE], preferred_element_type=F32).astype(BF16)
            elif prompt:
                out_ref[0, g] = lax.dot_general(w2_ref[1, 0, :LANE, :], hid, NT_DIMS,
                                                preferred_element_type=F32).astype(BF16)
            else:
                part = jnp.dot(hid, w2_ref[slot, g], preferred_element_type=F32)
                acc = part if acc is None else acc + part
        if not prompt:
            out_ref[0] = acc.astype(BF16)


def _compress(pages_arr, page_table, page_index, pos2, w1, w2, *, layer, n_pages, prompt):
    n_seq, pages_per_seq = page_table.shape
    n_steps = pages_per_seq // n_pages
    n = n_pages * (PAGE_SIZE // CMP_STRIDE)
    n_chunk = pages_per_seq * (PAGE_SIZE // CMP_STRIDE)
    flat = CMP_STRIDE * HEAD_DIM

    def page_spec(j):
        if prompt:
            return pl.BlockSpec((2 * KV_SLABS, PAGE_SIZE, LANE),
                                lambda b, s, pt: (0, page_index(pt[b, s * n_pages + j]), 0))
        return pl.BlockSpec((2, KV_DIM, PAGE_SIZE), lambda b, s, pt: (page_index(pt[b, s * n_pages + j]), 0, 0))

    if prompt:
        out_sds = [jax.ShapeDtypeStruct((n_seq, N_KV, n_chunk, LANE), BF16),
                   jax.ShapeDtypeStruct((n_seq, N_KV, LANE, n_chunk), BF16)]
        out_specs = [pl.BlockSpec((1, N_KV, n, LANE), lambda b, s, pt: (b, 0, s, 0)),
                     pl.BlockSpec((1, N_KV, LANE, n), lambda b, s, pt: (b, 0, 0, s))]
    else:
        out_sds = [jax.ShapeDtypeStruct((n_seq, n_chunk, KV_DIM), BF16)] * 2
        out_specs = [pl.BlockSpec((1, n, KV_DIM), lambda b, s, pt: (b, s, 0))] * 2
    grid_spec = pltpu.PrefetchScalarGridSpec(
        num_scalar_prefetch=1,
        grid=(n_seq, n_steps),
        in_specs=[page_spec(j) for j in range(n_pages)] + [_layer_spec(a.shape[1:], layer) for a in (pos2, w1, w2)],
        out_specs=out_specs,
        scratch_shapes=[pltpu.VMEM((2, KV_SLABS, n_pages * PAGE_SIZE, LANE), F32),
                        pltpu.VMEM((2, N_KV * n, flat), F32),
                        pltpu.VMEM((2, N_KV, SUBLANE, CMP_HID), F32)],
    )
    return pl.pallas_call(
        functools.partial(_compress_kernel, n_pages=n_pages, prompt=prompt),
        grid_spec=grid_spec,
        out_shape=out_sds,
        compiler_params=_params(2),
        name="compress_prompt" if prompt else "compress_sample",
    )(page_table, *([pages_arr] * n_pages), pos2, w1, w2)


def _select_bias(score_ref, cnt_ref, n_real, n_live, n_sel):
    score = score_ref[...]
    n_rows = -(-n_real // SUBLANE) * SUBLANE
    sub = _iota((SUBLANE, score.shape[1]), 0)
    cnt_ref[...] = jnp.zeros(score.shape, F32)

    def group(i0):
        cnt = cnt_ref[0:n_rows, :]
        for i in range(i0, min(i0 + SUBLANE, n_real)):
            si = score_ref[i:i + 1, :]
            pieces = []
            for r0 in range(0, n_rows, SUBLANE):
                sv = score[r0:r0 + SUBLANE]
                if r0 > i:
                    pieces.append(jnp.where(si >= sv, 1.0, 0.0))
                elif r0 + SUBLANE - 1 < i:
                    pieces.append(jnp.where(si > sv, 1.0, 0.0))
                else:
                    pieces.append(jnp.where(sub > i - r0, jnp.where(si >= sv, 1.0, 0.0), jnp.where(si > sv, 1.0, 0.0)))
            cnt = cnt + jnp.concatenate(pieces, axis=0)
        cnt_ref[0:n_rows, :] = cnt

    for i0 in range(0, n_real, SUBLANE):
        if n_live is None:
            group(i0)
        else:
            pl.when(i0 < n_live)(functools.partial(group, i0))
    return jnp.where((cnt_ref[...] < n_sel) & (score >= 0.0), 0.0, NEG_INF)


def _attn_prompt_kernel(qt_ref, gbt_ref, zb_ref, tri_ref, kc_ref, vct_ref, ks_ref, vst_ref, kw_ref, vwt_ref, o_ref,
                        score_scr, cnt_scr, m_scr, acc_scr, mx_scr, s_even_scr, s_odd_scr, *, tq, n_cmp, n_blk):
    qi = pl.program_id(2)
    s0 = qi * tq
    rows = GROUP * tq
    cw = min(rows, CHAIN_COLS)
    col_groups = [slice(c, c + cw) for c in range(0, rows, cw)]
    qpos = s0 + _mod(_iota((1, rows), 1), tq)

    def stack_q(lower):
        return jnp.concatenate([jnp.concatenate([qt_ref[0, r], lower], axis=0) for r in range(GROUP)], axis=1)

    q0 = stack_q(jnp.where(_iota((LANE - HEAD_DIM, tq), 0) == 0, NEG_INF, 0.0).astype(BF16))

    s_c = jnp.dot(kc_ref[0, 0], q0, preferred_element_type=F32)

    def window_scores(cs, first, n_keys):
        keys = kw_ref[0, 0, pl.ds(pl.multiple_of(s0 + first, tq), n_keys), :]
        return jnp.dot(keys, q0[:, cs], preferred_element_type=F32)

    w_parts = []
    for cs in col_groups:
        parts = [window_scores(cs, 0, tq) + tri_ref[0, :, cs]]
        if WINDOW > tq:
            parts.append(window_scores(cs, tq, WINDOW - tq))
        parts.append(window_scores(cs, WINDOW, tq) + tri_ref[1, :, cs])
        w_parts.append(parts)

    m_idx = _iota((n_cmp, 1), 0)
    mask_c = (m_idx >= 1) & (CMP_STRIDE * m_idx + CMP_STRIDE - 1 <= qpos)
    s_c = jnp.where(mask_c, s_c, NEG_INF)
    e_c = jnp.where(mask_c, jnp.exp2(s_c - jnp.max(s_c, axis=0, keepdims=True)), 0.0)
    p_c = e_c * (1.0 / jnp.maximum(jnp.sum(e_c, axis=0, keepdims=True), 1e-30))
    o_c = jnp.dot(vct_ref[0, 0], p_c.astype(BF16), preferred_element_type=F32)

    p_sum = p_c[:, 0:tq]
    for r in range(1, GROUP):
        p_sum = p_sum + p_c[:, r * tq:(r + 1) * tq]
    p_hi = p_sum.astype(BF16)
    p_lo = (p_sum - p_hi.astype(F32)).astype(BF16)
    ov_t = _overlap_t(MAX_BLOCKS, n_cmp)
    imp = jnp.dot(ov_t, p_hi, preferred_element_type=F32) + jnp.dot(ov_t, p_lo, preferred_element_type=F32)

    w_out = []
    for cs, parts in zip(col_groups, w_parts):
        m_w = functools.reduce(jnp.maximum, [jnp.max(s, axis=0, keepdims=True) for s in parts])
        p_w = jnp.concatenate([jnp.exp2(s - m_w).astype(BF16) for s in parts], axis=0)
        acc_w = jnp.dot(vwt_ref[0, 0, :, pl.ds(pl.multiple_of(s0, tq), WINDOW + tq)], p_w, preferred_element_type=F32)
        w_out.append(acc_w[:HEAD_DIM] / jnp.maximum(acc_w[HEAD_DIM:HEAD_DIM + 1], 1e-30))
    o_w = jnp.concatenate(w_out, axis=1)

    blk = _iota((MAX_BLOCKS, tq), 0)
    cur = _div(s0 + _iota((MAX_BLOCKS, tq), 1), SLC_BLOCK)
    forced = (blk == 0) | (blk == cur) | (blk == cur - 1)
    score = jnp.where(forced, FORCE_SCORE, imp)
    score_scr[...] = jnp.where(blk <= cur, score, -1.0)
    n_live = jnp.minimum(n_blk, (s0 + tq) // SLC_BLOCK)
    bias_t = _select_bias(score_scr, cnt_scr, n_blk, n_live, min(TOP_N, n_blk))
    qx = stack_q(bias_t.astype(BF16))

    m_scr[...] = jnp.full(m_scr.shape, NEG_INF, F32)
    acc_scr[...] = jnp.zeros(acc_scr.shape, F32)
    s_bufs = (s_even_scr, s_odd_scr)

    def scores(k0, slot, causal):
        keys = ks_ref[0, 0, pl.ds(pl.multiple_of(k0, tq), tq), :]
        prods = [jnp.dot(keys, qx[:, cs], preferred_element_type=F32) for cs in col_groups]
        for cs, s in zip(col_groups, prods):
            if causal:
                s = s + tri_ref[1, :, cs]
            s_bufs[slot][:, cs] = s
            mx_scr[slot, 0:1, cs] = jnp.max(s, axis=0, keepdims=True)

    def accumulate(k0, slot):
        vals_t = vst_ref[0, 0, :, pl.ds(pl.multiple_of(k0, tq), tq)]
        for cs in col_groups:
            m_old = m_scr[0:1, cs]
            m_new = jnp.maximum(m_old, mx_scr[slot, 0:1, cs])
            p = jnp.exp2(s_bufs[slot][:, cs] - m_new).astype(BF16)
            acc_scr[:, cs] = jnp.exp2(m_old - m_new) * acc_scr[:, cs] + jnp.dot(vals_t, p, preferred_element_type=F32)
            m_scr[0:1, cs] = m_new

    n_pairs = qi // 2
    scores(s0, 0, True)

    def pair(j, _):
        scores(2 * j * tq, 1, False)
        accumulate(jnp.where(j == 0, s0, (2 * j - 1) * tq), 0)
        scores((2 * j + 1) * tq, 0, False)
        accumulate(2 * j * tq, 1)
        return 0

    lax.fori_loop(0, n_pairs, pair, 0)
    pending = jnp.where(n_pairs == 0, s0, (2 * n_pairs - 1) * tq)

    @pl.when(qi - 2 * n_pairs == 1)
    def _():
        scores(s0 - tq, 1, False)
        accumulate(pending, 0)
        accumulate(s0 - tq, 1)

    @pl.when(qi - 2 * n_pairs == 0)
    def _():
        accumulate(pending, 0)

    o_s = acc_scr[:HEAD_DIM, :] / jnp.maximum(acc_scr[HEAD_DIM:HEAD_DIM + 1, :], 1e-30)

    gb = gbt_ref[0, 0]
    zb = zb_ref[0]
    heads = []
    for r in range(GROUP):
        rs = slice(r * tq, (r + 1) * tq)
        heads.append(gb[3 * r:3 * r + 1] * o_c[:HEAD_DIM, rs] + gb[3 * r + 1:3 * r + 2] * o_s[:, rs]
                     + gb[3 * r + 2:3 * r + 3] * o_w[:, rs])
    for pair in range(GROUP // 2):
        slab = jnp.concatenate([heads[2 * pair], heads[2 * pair + 1]], axis=0).T
        o_ref[0, :, pair * LANE:(pair + 1) * LANE] = (zb[:, pair * LANE:(pair + 1) * LANE] * slab).astype(BF16)


def _attn_prompt(q_t, gb_t, zb_act, kc_ext, vc_t, ks_ext, vs_t, kw_pad, vw_pad, *, tq):
    n_b, seq_len = zb_act.shape[0], zb_act.shape[1]
    n_cmp = kc_ext.shape[2]
    n_blk = seq_len // SLC_BLOCK
    head_w = GROUP * HEAD_DIM
    rows = GROUP * tq
    assert WINDOW % tq == 0 and rows % min(rows, CHAIN_COLS) == 0
    off = _iota((tq, rows), 0) - _mod(_iota((tq, rows), 1), tq)
    tri = jnp.stack([jnp.where(off > 0, 0.0, NEG_INF), jnp.where(off <= 0, 0.0, NEG_INF)]).astype(F32)
    tok_sub = lambda n: pl.BlockSpec((1, 1, n, LANE), lambda b, g, i: (b, g, 0, 0))
    tok_lane = lambda n: pl.BlockSpec((1, 1, LANE, n), lambda b, g, i: (b, g, 0, 0))
    return pl.pallas_call(
        functools.partial(_attn_prompt_kernel, tq=tq, n_cmp=n_cmp, n_blk=n_blk),
        grid=(n_b, N_KV, seq_len // tq),
        in_specs=[pl.BlockSpec((1, GROUP, HEAD_DIM, tq), lambda b, g, i: (b, g, 0, i)),
                  pl.BlockSpec((1, 1, N_GATE_ROWS, tq), lambda b, g, i: (b, g, 0, i)),
                  pl.BlockSpec((1, tq, head_w), lambda b, g, i: (b, i, g)),
                  _const_spec((2, tq, rows)),
                  tok_sub(n_cmp), tok_lane(n_cmp), tok_sub(seq_len), tok_lane(seq_len),
                  tok_sub(seq_len + WINDOW), tok_lane(seq_len + WINDOW)],
        out_specs=pl.BlockSpec((1, tq, head_w), lambda b, g, i: (b, i, g)),
        out_shape=jax.ShapeDtypeStruct((n_b, seq_len, ATT_DIM), BF16),
        scratch_shapes=[pltpu.VMEM((MAX_BLOCKS, tq), F32), pltpu.VMEM((MAX_BLOCKS, tq), F32),
                        pltpu.VMEM((SUBLANE, rows), F32),
                        pltpu.VMEM((LANE, rows), F32), pltpu.VMEM((2, SUBLANE, rows), F32),
                        pltpu.VMEM((tq, rows), F32), pltpu.VMEM((tq, rows), F32)],
        compiler_params=_params(3),
        name="attention_prompt",
    )(q_t, gb_t, zb_act, tri, kc_ext, vc_t, ks_ext, vs_t, kw_pad, vw_pad)


def _attn_sample_kernel(pt_ref, *refs, n_pages, past_len, dec_len, n_blk_pad):
    pages = refs[:n_pages]
    (qbd_ref, qbdt_ref, kc_ref, vc_ref, newkv_ref, win_ref, newwin_ref, gate_ref, o_ref,
     score_scr, cnt_scr, bias_scr, m_scr, l_scr, acc_scr, oc_scr) = refs[n_pages:]
    st = pl.program_id(1)
    n_steps = pl.num_programs(1)
    cols = LANE
    tslots = cols // N_HEADS
    n_cmp = kc_ref.shape[1]
    n_past_blk = past_len // SLC_BLOCK
    per_page = PAGE_SIZE // SLC_BLOCK
    qbdt = qbdt_ref[0]
    tok_c = _mod(_iota((cols, 1), 0), tslots)

    @pl.when(st == 0)
    def _():
        qpos = past_len + _mod(_iota((1, cols), 1), tslots)
        s_c = jnp.dot(kc_ref[0], qbd_ref[0], preferred_element_type=F32)
        m_idx = _iota((n_cmp, 1), 0)
        mask_c = (m_idx >= 1) & (CMP_STRIDE * m_idx + CMP_STRIDE - 1 <= qpos)
        s_c = jnp.where(mask_c, s_c, NEG_INF)
        e_c = jnp.where(mask_c, jnp.exp(s_c - jnp.max(s_c, axis=0, keepdims=True)), 0.0)
        p_c = e_c / jnp.maximum(jnp.sum(e_c, axis=0, keepdims=True), 1e-30)
        p_hi = p_c.astype(BF16)
        p_lo = (p_c - p_hi.astype(F32)).astype(BF16)
        oc_scr[...] = jnp.dot(p_c.T.astype(BF16), vc_ref[0], preferred_element_type=F32)
        ov_t = _overlap_t(n_blk_pad, n_cmp)
        imp1 = jnp.dot(ov_t, p_hi, preferred_element_type=F32) + jnp.dot(ov_t, p_lo, preferred_element_type=F32)
        ca = _iota((cols, cols), 0)
        cb = _iota((cols, cols), 1)
        per_group = GROUP * tslots
        same = jnp.where((_div(ca, per_group) == _div(cb, per_group)) & (_mod(ca, tslots) == _mod(cb, tslots)),
                         1.0, 0.0).astype(BF16)
        i_hi = imp1.astype(BF16)
        i_mid = (imp1 - i_hi.astype(F32)).astype(BF16)
        i_lo = (imp1 - i_hi.astype(F32) - i_mid.astype(F32)).astype(BF16)
        imp = (jnp.dot(i_hi, same, preferred_element_type=F32) + jnp.dot(i_mid, same, preferred_element_type=F32)
               + jnp.dot(i_lo, same, preferred_element_type=F32))
        blk = _iota((n_blk_pad, cols), 0)
        cur = _div(qpos, SLC_BLOCK)
        forced = (blk == 0) | (blk == cur) | (blk == cur - 1)
        score = jnp.where(forced, FORCE_SCORE, imp)
        score_scr[...] = jnp.where(blk <= cur, score, -1.0)
        n_blk = n_past_blk + 1
        bias_scr[...] = _select_bias(score_scr, cnt_scr, n_blk, None, min(TOP_N, n_blk)).T
        m_scr[...] = jnp.full(m_scr.shape, NEG_INF, F32)
        l_scr[...] = jnp.zeros(l_scr.shape, F32)
        acc_scr[...] = jnp.zeros(acc_scr.shape, F32)

    def update(s, pv_fn):
        m_old = m_scr[...]
        m_new = jnp.maximum(m_old, jnp.max(s, axis=1, keepdims=True))
        alpha = jnp.exp(m_old - m_new)
        p = jnp.exp(s - m_new)
        l_scr[...] = alpha * l_scr[...] + jnp.sum(p, axis=1, keepdims=True)
        acc_scr[...] = alpha * acc_scr[...] + pv_fn(p.astype(BF16))
        m_scr[...] = m_new

    bias_b = bias_scr[...].astype(BF16)
    blk0 = st * (n_pages * per_page)
    key_blk = blk0 + _div(_iota((n_blk_pad, n_pages * PAGE_SIZE), 1), SLC_BLOCK)
    expand = jnp.where(_iota((n_blk_pad, n_pages * PAGE_SIZE), 0) == key_blk, 1.0, 0.0).astype(BF16)
    keys_t = jnp.concatenate([pages[j][0].astype(BF16) for j in range(n_pages)], axis=1)
    vals_t = jnp.concatenate([pages[j][1].astype(BF16) for j in range(n_pages)], axis=1)
    s = (jnp.dot(qbdt, keys_t, preferred_element_type=F32) + jnp.dot(bias_b, expand, preferred_element_type=F32))
    update(s, lambda p: lax.dot_general(p, vals_t, NT_DIMS, preferred_element_type=F32))

    @pl.when(st == n_steps - 1)
    def _():
        def pad_rows(a):
            return jnp.concatenate([a, jnp.zeros((LANE - a.shape[0], a.shape[1]), a.dtype)], axis=0)

        key_i = _iota((1, LANE), 1)
        new = pad_rows(newkv_ref[0])
        s_n = lax.dot_general(qbdt, new[:, 2 * KV_DIM:3 * KV_DIM].astype(BF16), NT_DIMS, preferred_element_type=F32)
        s_n = s_n + bias_scr[:, n_past_blk:n_past_blk + 1]
        s_n = jnp.where((key_i <= tok_c) & (key_i < dec_len), s_n, NEG_INF)
        v_n = new[:, 3 * KV_DIM:].astype(BF16)
        update(s_n, lambda p: jnp.dot(p, v_n, preferred_element_type=F32))
        o_s = acc_scr[...] / jnp.maximum(l_scr[...], 1e-30)

        w_buf = win_ref.shape[2]
        neww = pad_rows(newwin_ref[0])
        s_w = jnp.concatenate(
            [jnp.dot(qbdt, win_ref[0].astype(BF16), preferred_element_type=F32),
             lax.dot_general(qbdt, neww[:, :KV_DIM].astype(BF16), NT_DIMS, preferred_element_type=F32)], axis=1)
        idx = _iota((1, w_buf + LANE), 1)
        kpos = past_len - w_buf + idx
        dist = past_len + tok_c - kpos
        mask_w = (dist >= 0) & (dist < WINDOW) & (idx < w_buf + dec_len) & (kpos >= 0)
        s_w = jnp.where(mask_w, s_w, NEG_INF)
        e_w = jnp.where(mask_w, jnp.exp(s_w - jnp.max(s_w, axis=1, keepdims=True)), 0.0)
        p_w = (e_w / jnp.maximum(jnp.sum(e_w, axis=1, keepdims=True), 1e-30)).astype(BF16)
        o_w = (lax.dot_general(p_w[:, :w_buf], win_ref[1].astype(BF16), NT_DIMS, preferred_element_type=F32)
               + jnp.dot(p_w[:, w_buf:], neww[:, KV_DIM:].astype(BF16), preferred_element_type=F32))
        gate = gate_ref[0]
        o_ref[0] = gate[:, 0:1] * oc_scr[...] + gate[:, 1:2] * o_s + gate[:, 2:3] * o_w


def _attn_sample(cache_t, page_table, page_index, win_t, win_index, qbd, qbdt, kc_flat, vc_flat, newkv8, newwin8,
                 gates, *, n_pages, past_len, dec_len):
    n_seq, pages_per_seq = page_table.shape
    n_steps = pages_per_seq // n_pages
    n_cmp = kc_flat.shape[1]
    n_blk_pad = -(-(past_len // SLC_BLOCK + 1) // LANE) * LANE
    w_buf = win_t.shape[2]

    def page_spec(j):
        return pl.BlockSpec((2, KV_DIM, PAGE_SIZE), lambda b, s, pt: (page_index(pt[b, s * n_pages + j]), 0, 0))

    per_seq = lambda shape: pl.BlockSpec((1,) + shape, lambda b, s, pt: (b,) + (0,) * len(shape))
    grid_spec = pltpu.PrefetchScalarGridSpec(
        num_scalar_prefetch=1,
        grid=(n_seq, n_steps),
        in_specs=[page_spec(j) for j in range(n_pages)] + [
            per_seq((KV_DIM, LANE)), per_seq((LANE, KV_DIM)), per_seq((n_cmp, KV_DIM)), per_seq((n_cmp, KV_DIM)),
            per_seq((SUBLANE, 4 * KV_DIM)),
            pl.BlockSpec((2, KV_DIM, w_buf), lambda b, s, pt: (win_index(b), 0, 0)),
            per_seq((SUBLANE, 2 * KV_DIM)), per_seq((LANE, LANE))],
        out_specs=per_seq((LANE, KV_DIM)),
        scratch_shapes=[pltpu.VMEM((n_blk_pad, LANE), F32), pltpu.VMEM((n_blk_pad, LANE), F32),
                        pltpu.VMEM((LANE, n_blk_pad), F32),
                        pltpu.VMEM((LANE, 1), F32), pltpu.VMEM((LANE, 1), F32),
                        pltpu.VMEM((LANE, KV_DIM), F32), pltpu.VMEM((LANE, KV_DIM), F32)],
    )
    return pl.pallas_call(
        functools.partial(_attn_sample_kernel, n_pages=n_pages, past_len=past_len, dec_len=dec_len,
                          n_blk_pad=n_blk_pad),
        grid_spec=grid_spec,
        out_shape=jax.ShapeDtypeStruct((n_seq, LANE, KV_DIM), F32),
        compiler_params=_params(2),
        name="attention_sample",
    )(page_table, *([cache_t] * n_pages), qbd, qbdt, kc_flat, vc_flat, newkv8, win_t, newwin8, gates)


def _permute_w_in(w):
    lead = w.shape[:-1]
    c = np.cumsum([0, 4 * CONV_DIM, ATT_DIM, 4 * KV_DIM, 2 * KV_DIM, 3 * N_HEADS, ATT_DIM, 2 * D_MODEL])
    wa, wq, wkv, wwin, wgb, wzb, wgm = (w[..., c[i]:c[i + 1]] for i in range(7))
    wgb = jnp.pad(wgb.reshape(lead + (N_KV, 3 * GROUP)), ((0, 0),) * (len(lead) + 1) + ((0, LANE - 3 * GROUP),))
    return jnp.concatenate([wa, wq, wkv, wwin, wzb, wgm, wgb.reshape(lead + (N_KV * LANE,))], axis=-1).astype(BF16)


def _rope_tables(pos):
    half = HEAD_DIM // 2
    inv = jnp.power(ROPE_THETA, -jnp.arange(half, dtype=F32) / half)
    ang = pos.astype(F32)[:, None] * inv[None, :]
    cos, sin = jnp.cos(ang), jnp.sin(ang)
    reps = LANE // HEAD_DIM
    return jnp.tile(jnp.concatenate([cos, cos], axis=1), (1, reps)), jnp.tile(jnp.concatenate([-sin, sin], axis=1), (1, reps))


def _compress_weights(pos_k, w1_k, w2_k, pos_v, w1_v, w2_v, grouped):
    depth = pos_k.shape[0]
    flat = CMP_STRIDE * HEAD_DIM
    pos2 = jnp.stack([jnp.pad(p.reshape(depth, 2, flat), ((0, 0), (0, SUBLANE - 2), (0, 0))) for p in (pos_k, pos_v)],
                     axis=1)
    w1 = jnp.stack([w.reshape(depth, 2, flat, CMP_HID) for w in (w1_k, w1_v)], axis=1).astype(BF16)
    if grouped:
        w2 = jnp.stack([jnp.pad(w2_k, ((0, 0), (0, 0), (0, CMP_HID - HEAD_DIM)))[:, None],
                        jnp.pad(w2_v.transpose(0, 2, 1), ((0, 0), (0, CMP_HID - HEAD_DIM), (0, 0)))[:, None]], axis=1)
    else:
        w2 = jnp.stack([jnp.stack([jnp.pad(w, ((0, 0), (0, 0), (g * HEAD_DIM, KV_DIM - (g + 1) * HEAD_DIM)))
                                   for g in range(N_KV)], axis=1) for w in (w2_k, w2_v)], axis=1)
    return pos2, w1, w2.astype(BF16)


def kernel(x_prompt, x_sample, cache_kv, state_win, state_conv, page_table, p_prompt, p_sample, ln_g, w_in, conv_w,
           cmp_pos_k, cmp_w1_k, cmp_w2_k, cmp_pos_v, cmp_w1_v, cmp_w2_v, w_out_a, w_out_b, w_o, w_ple, w_ple_gate,
           final_g):
    depth = ln_g.shape[0]
    n_b, seq_len, _ = x_prompt.shape
    n_dec, dec_len, _ = x_sample.shape
    pages_per_seq = page_table.shape[1]
    past_len = pages_per_seq * PAGE_SIZE
    n_pool = cache_kv.shape[1]
    w_buf = state_win.shape[2]

    tm_p = min(256, seq_len)
    tq = min(256, seq_len)
    tm_merge = min(512, seq_len)
    pages_p = min(16, seq_len // PAGE_SIZE)
    pages_s = min(16, pages_per_seq)
    rows_s = n_dec * dec_len
    carry_p = SUBLANE
    carry_s = (CONV_W - 1) * n_dec
    assert seq_len % PAGE_SIZE == 0 and seq_len // SLC_BLOCK <= MAX_BLOCKS and seq_len >= WINDOW + tq
    assert CONV_W - 1 <= dec_len <= SUBLANE and rows_s % SUBLANE == 0
    assert w_buf == min(WINDOW, past_len) and N_HEADS * SUBLANE == LANE

    cos_p, sin_p = _rope_tables(jnp.arange(seq_len, dtype=jnp.int32))
    pos_s = past_len + jnp.repeat(jnp.arange(dec_len, dtype=jnp.int32), n_dec)
    cos_s, sin_s = _rope_tables(pos_s)
    prompt_pages = jnp.arange(n_b * (seq_len // PAGE_SIZE), dtype=jnp.int32).reshape(n_b, seq_len // PAGE_SIZE)
    eye_g = jnp.eye(N_KV, dtype=F32)
    tpad = SUBLANE - dec_len
    win_pad_keys = (jnp.arange(LANE) == HEAD_DIM).astype(BF16)
    cache_t = cache_kv.transpose(0, 1, 3, 4, 5, 2).reshape(depth * n_pool * 4, KV_DIM, PAGE_SIZE)
    win_t = state_win.transpose(0, 1, 3, 4, 5, 2).reshape(depth * n_dec * 2, KV_DIM, w_buf)

    xp = x_prompt.reshape(n_b * seq_len, D_MODEL)
    xs = x_sample.transpose(1, 0, 2).reshape(rows_s, D_MODEL)
    kv_s, win_new_s, conv_p, conv_s = [], [], [], []
    carried = None
    w_perm = _permute_w_in(w_in)
    g_rows = ln_g.reshape(depth, 1, D_MODEL)
    cw8 = jnp.pad(conv_w, ((0, 0), (0, SUBLANE - CONV_W), (0, 0)))
    merge_w = (w_out_a.astype(BF16), w_out_b.astype(BF16), w_o.astype(BF16), w_ple.astype(BF16),
               w_ple_gate.astype(BF16), final_g.reshape(1, D_MODEL))
    cmp_args = (cmp_pos_k, cmp_w1_k, cmp_w2_k, cmp_pos_v, cmp_w1_v, cmp_w2_v)
    cmp_w_prompt = _compress_weights(*cmp_args, True)
    cmp_w_sample = _compress_weights(*cmp_args, False)
    zero_prefix = jnp.zeros((n_b, carry_p, CONV_DIM), F32)
    p_sample_rows = p_sample.transpose(0, 2, 1, 3).reshape(depth, rows_s, PLE_DIM)
    for i in range(depth):
        last = i == depth - 1

        (a_act, newkv_all, win_all, zb_act, gmix, tail, q_t, ks_ext, vs_t, kw_ext, vw_t, gb_t, cmp_rows) = _project(
            xp, g_rows, w_perm, cos_p, sin_p, zero_prefix, cw8, carried, layer=i,
            n_seq=n_b, rows_per_seq=seq_len, tm=tm_p, stride=1, carry=carry_p, transposed=True, pos_per_seq=True)
        carried = (newkv_all, win_all)
        kc_ext, vc_t = _compress(cmp_rows, prompt_pages, lambda p: p, *cmp_w_prompt, layer=i, n_pages=pages_p,
                                 prompt=True)
        kw_pad = jnp.concatenate([jnp.broadcast_to(win_pad_keys, (n_b, N_KV, WINDOW, LANE)), kw_ext], axis=2)
        vw_pad = jnp.pad(vw_t, ((0, 0), (0, 0), (0, 0), (WINDOW, 0)))
        y_act = _attn_prompt(q_t, gb_t, zb_act.reshape(n_b, seq_len, ATT_DIM), kc_ext, vc_t, ks_ext, vs_t, kw_pad,
                             vw_pad, tq=tq)
        xp = _merge(xp, a_act, y_act.reshape(n_b * seq_len, ATT_DIM), None, gmix,
                    p_prompt.reshape(depth, n_b * seq_len, PLE_DIM), *merge_w, layer=i, tm=tm_merge, final=last)
        conv_p.append(tail[:, carry_p - (CONV_W - 1):])

        prefix_s = state_conv[i].transpose(1, 0, 2).reshape(1, carry_s, CONV_DIM)
        (a_act, newkv, win, zb_act, gmix, tail, qp, gb) = _project(
            xs, g_rows, w_perm, cos_s, sin_s, prefix_s, cw8, None, layer=i,
            n_seq=1, rows_per_seq=rows_s, tm=rows_s, stride=n_dec, carry=carry_s, transposed=False, pos_per_seq=False)
        base = i * n_pool
        kc_flat, vc_flat = _compress(cache_t, page_table, lambda p: (base + p) * 2, *cmp_w_sample, layer=i,
                                     n_pages=pages_s, prompt=False)
        q5 = jnp.pad(qp.reshape(dec_len, n_dec, N_KV, GROUP, HEAD_DIM).astype(F32),
                     ((0, tpad), (0, 0), (0, 0), (0, 0), (0, 0)))
        qbd = q5.transpose(1, 4, 2, 3, 0)[:, None] * eye_g[None, :, None, :, None, None]
        qbd = qbd.reshape(n_dec, KV_DIM, LANE).astype(BF16)
        qbdt = q5.transpose(1, 2, 3, 0, 4)[:, :, :, :, None] * eye_g[None, :, None, None, :, None]
        qbdt = qbdt.reshape(n_dec, LANE, KV_DIM).astype(BF16)
        gates = gb.reshape(dec_len, n_dec, N_KV, LANE)[..., :3 * GROUP].reshape(dec_len, n_dec, N_KV, GROUP, 3)
        gates = jnp.pad(gates.transpose(1, 2, 3, 0, 4), ((0, 0), (0, 0), (0, 0), (0, tpad), (0, LANE - 3)))
        gates = gates.reshape(n_dec, LANE, LANE)
        newkv_b = newkv.reshape(dec_len, n_dec, 4 * KV_DIM).transpose(1, 0, 2)
        newwin_b = win.reshape(dec_len, n_dec, 2 * KV_DIM).transpose(1, 0, 2)
        pad_rows = lambda a: jnp.pad(a, ((0, 0), (0, tpad), (0, 0)))
        o_all = _attn_sample(cache_t, page_table, lambda p: (base + p) * 2 + 1, win_t, lambda b: i * n_dec + b, qbd,
                             qbdt, kc_flat, vc_flat, pad_rows(newkv_b), pad_rows(newwin_b), gates, n_pages=pages_s,
                             past_len=past_len, dec_len=dec_len)
        o6 = o_all.reshape(n_dec, N_KV, GROUP, SUBLANE, N_KV, HEAD_DIM)
        yb = jnp.stack([o6[:, g, :, :, g] for g in range(N_KV)], axis=1)
        yb = yb.transpose(3, 0, 1, 2, 4)[:dec_len].reshape(rows_s, ATT_DIM)
        xs = _merge(xs, a_act, yb, zb_act, gmix, p_sample_rows, *merge_w, layer=i, tm=rows_s, final=last)
        kv_s.append(newkv_b.reshape(n_dec, dec_len, 4, N_KV, HEAD_DIM))
        win_new_s.append(newwin_b.reshape(n_dec, dec_len, 2, N_KV, HEAD_DIM))
        conv_s.append(tail.reshape(CONV_W - 1, n_dec, CONV_DIM).transpose(1, 0, 2))

    y_prompt = xp.reshape(n_b, seq_len, D_MODEL)
    y_sample = xs.reshape(dec_len, n_dec, D_MODEL).transpose(1, 0, 2)
    newkv_all, win_all = carried
    kv_p = newkv_all.reshape(depth, n_b, 4, N_KV, HEAD_DIM, seq_len).transpose(0, 1, 5, 2, 3, 4)
    win_p = win_all.reshape(depth, n_b, 2, N_KV, HEAD_DIM, min(WINDOW, seq_len)).transpose(0, 1, 5, 2, 3, 4)
    win_s = jnp.concatenate([state_win, jnp.stack(win_new_s)], axis=2)[:, :, -w_buf:]
    return (y_prompt, y_sample, kv_p, jnp.stack(kv_s), win_p, win_s, jnp.stack(conv_p), jnp.stack(conv_s))
```

```python
import functools

import numpy as np
import jax
import jax.numpy as jnp
from jax import lax
from jax.experimental import pallas as pl
from jax.experimental.pallas import tpu as pltpu

D_MODEL = 1024
CONV_DIM = D_MODEL
CONV_W = 3
N_HEADS = 16
HEAD_DIM = 64
N_KV = 4
GROUP = N_HEADS // N_KV
ATT_DIM = N_HEADS * HEAD_DIM
KV_DIM = N_KV * HEAD_DIM
PAGE_SIZE = 128
CMP_STRIDE = 16
CMP_BLOCK = 2 * CMP_STRIDE
CMP_HID = 4 * HEAD_DIM
SLC_BLOCK = 64
TOP_N = 16
WINDOW = 512
PLE_DIM = 256
ROPE_THETA = 10000.0
RMS_EPS = 1e-6
NEG_INF = -1e30
FORCE_SCORE = 1e6

LANE = 128
SUBLANE = 8
VMEM_LIMIT = 56 * 1024 * 1024
KV_SLABS = KV_DIM // LANE
N_GATE_ROWS = 2 * SUBLANE
CMP_DEPTH = 2 * LANE
CHAIN_COLS = 4 * LANE

OFF_A = 0
OFF_Q = 4 * CONV_DIM
OFF_KV = OFF_Q + ATT_DIM
OFF_WIN = OFF_KV + 4 * KV_DIM
OFF_ZB = OFF_WIN + 2 * KV_DIM
OFF_GM = OFF_ZB + ATT_DIM
OFF_GB = OFF_GM + 2 * D_MODEL
N_PROJ = OFF_GB + N_KV * LANE
MAX_BLOCKS = LANE - HEAD_DIM

F32 = jnp.float32
BF16 = jnp.bfloat16
NT_DIMS = (((1,), (1,)), ((), ()))


def _params(n_axes):
    return pltpu.CompilerParams(dimension_semantics=("arbitrary",) * n_axes, vmem_limit_bytes=VMEM_LIMIT)


def _const_spec(shape):
    zeros = (0,) * len(shape)
    return pl.BlockSpec(shape, lambda *_: zeros, pipeline_mode=pl.Buffered(1))


def _layer_spec(shape, layer):
    index = (layer,) + (0,) * len(shape)
    return pl.BlockSpec((None,) + tuple(shape), lambda *_: index, pipeline_mode=pl.Buffered(1))


def _sigmoid(x):
    return 1.0 / (1.0 + jnp.exp(-x))


def _div(x, d):
    assert d & (d - 1) == 0
    return jnp.right_shift(x, d.bit_length() - 1)


def _mod(x, d):
    assert d & (d - 1) == 0
    return jnp.bitwise_and(x, d - 1)


def _iota(shape, dim):
    return lax.broadcasted_iota(jnp.int32, shape, dim)


def _rope_slab(x, cos, sin, lane):
    swapped = jnp.where(_mod(lane, HEAD_DIM) < HEAD_DIM // 2, pltpu.roll(x, LANE - HEAD_DIM // 2, 1),
                        pltpu.roll(x, HEAD_DIM // 2, 1))
    return x * cos + swapped * sin


def _overlap_t(n_blk, n_cmp):
    bj = _iota((n_blk, n_cmp), 0)
    bm = _iota((n_blk, n_cmp), 1)
    ratio = SLC_BLOCK // CMP_STRIDE
    return jnp.where((bm >= ratio * bj) & (bm <= ratio * bj + ratio) & (bm >= 1), 1.0, 0.0).astype(BF16)


def _proj_kernel(x_ref, g_ref, w_ref, cos_ref, sin_ref, pre_ref, cw_ref, *refs, tm, stride, carry, transposed,
                 n_carried):
    a_ref, kv_ref, win_ref, zb_ref, gm_ref, tail_ref, *rest = refs[n_carried:]
    if transposed:
        qt_ref, ks_ref, vst_ref, kw_ref, vwt_ref, gbt_ref, cmp_ref, u_scr = rest
    else:
        q_ref, gb_ref, u_scr = rest
    t = pl.program_id(1)
    x = x_ref[...]
    ms = jnp.mean(x * x, axis=-1, keepdims=True)
    h = (x * lax.rsqrt(ms + RMS_EPS) * g_ref[...]).astype(BF16)

    def seg(off, width):
        return jnp.dot(h, w_ref[:, off:off + width], preferred_element_type=F32)

    @pl.when(t == 0)
    def _():
        u_scr[0:carry, :] = pre_ref[0]

    u = seg(OFF_A + CONV_DIM, CONV_DIM) * seg(OFF_A + 2 * CONV_DIM, CONV_DIM)
    u_scr[carry:carry + tm, :] = u
    cw = cw_ref[...]
    conv = cw[0:1] * u_scr[carry - 2 * stride:carry - 2 * stride + tm, :]
    conv = conv + cw[1:2] * u_scr[carry - stride:carry - stride + tm, :]
    conv = conv + cw[2:3] * u
    ba = seg(OFF_A, CONV_DIM)
    za = seg(OFF_A + 3 * CONV_DIM, CONV_DIM)
    a_ref[...] = ((za * _sigmoid(za)) * (ba * conv)).astype(BF16)
    tail = u_scr[tm:tm + carry, :]
    tail_ref[0] = tail
    u_scr[0:carry, :] = tail

    lane = _iota((tm, LANE), 1)
    cos = cos_ref[...]
    sin = sin_ref[...]
    scale = HEAD_DIM ** -0.5 * (float(np.log2(np.e)) if transposed else 1.0)

    zq = seg(OFF_Q, ATT_DIM)
    for s in range(ATT_DIM // LANE):
        qs = _rope_slab(zq[:, s * LANE:(s + 1) * LANE], cos, sin, lane) * scale
        if transposed:
            qst = qs.T.astype(BF16)
            qt_ref[0, 2 * s] = qst[:HEAD_DIM]
            qt_ref[0, 2 * s + 1] = qst[HEAD_DIM:]
        else:
            q_ref[:, s * LANE:(s + 1) * LANE] = qs.astype(BF16)

    zkv = seg(OFF_KV, 4 * KV_DIM)
    zwin = seg(OFF_WIN, 2 * KV_DIM)
    kv_slabs, kv_t = [], []
    for s in range(4 * KV_SLABS):
        v = zkv[:, s * LANE:(s + 1) * LANE]
        if (s // KV_SLABS) % 2 == 0:
            v = _rope_slab(v, cos, sin, lane)
        if transposed:
            kv_t.append(v.T)
            kv_ref[s * LANE:(s + 1) * LANE, :] = kv_t[-1]
        else:
            kv_ref[:, s * LANE:(s + 1) * LANE] = v
        kv_slabs.append(v)
    win_slabs, win_t = [], []
    for s in range(2 * KV_SLABS):
        v = zwin[:, s * LANE:(s + 1) * LANE]
        if s // KV_SLABS == 0:
            v = _rope_slab(v, cos, sin, lane)
        if transposed:
            win_t.append(v.T)
            win_ref[s * LANE:(s + 1) * LANE, :] = win_t[-1]
        else:
            win_ref[:, s * LANE:(s + 1) * LANE] = v
        win_slabs.append(v)

    zb = seg(OFF_ZB, ATT_DIM)
    zb_ref[...] = zb * _sigmoid(zb)
    gb = _sigmoid(seg(OFF_GB, N_KV * LANE))

    def merge_gates():
        gm_ref[...] = _sigmoid(seg(OFF_GM, 2 * D_MODEL))

    if not transposed:
        gb_ref[...] = gb
        merge_gates()
        return

    row = _iota((tm, LANE), 0)
    onehot = jnp.where(lane - HEAD_DIM == _div(t * tm + row, SLC_BLOCK), 1.0, 0.0)
    low = lane < HEAD_DIM
    ones_row = jnp.where(_iota((HEAD_DIM, tm), 0) == 0, 1.0, 0.0).astype(BF16)
    vs_t = [kv_t[3 * KV_SLABS + s].astype(BF16) for s in range(KV_SLABS)]
    vw_t = [win_t[KV_SLABS + s].astype(BF16) for s in range(KV_SLABS)]
    for s in range(2 * KV_SLABS):
        cmp_ref[s] = kv_slabs[s]
    for g in range(N_KV):
        s, hi = divmod(g, 2)

        def half(v):
            return v if hi == 0 else pltpu.roll(v, HEAD_DIM, 1)

        ks_ref[0, g] = jnp.where(low, half(kv_slabs[2 * KV_SLABS + s]), onehot).astype(BF16)
        kw_ref[0, g] = jnp.where(low, half(win_slabs[s]), 0.0).astype(BF16)
        vst_ref[0, g] = jnp.concatenate([vs_t[s][hi * HEAD_DIM:(hi + 1) * HEAD_DIM], ones_row], axis=0)
        vwt_ref[0, g] = jnp.concatenate([vw_t[s][hi * HEAD_DIM:(hi + 1) * HEAD_DIM], ones_row], axis=0)
        gbt_ref[0, g] = gb[:, g * LANE:(g + 1) * LANE].T[:N_GATE_ROWS]
    merge_gates()


def _window_pads_kernel(kw_ref, vwt_ref):
    flag = jnp.where(_iota((WINDOW, LANE), 1) == HEAD_DIM, 1.0, 0.0).astype(BF16)
    for g in range(N_KV):
        kw_ref[0, g] = flag
        vwt_ref[0, g] = jnp.zeros((LANE, WINDOW), BF16)


def _window_pads(n_seq, rows_per_seq):
    padded = rows_per_seq + WINDOW
    return pl.pallas_call(
        _window_pads_kernel,
        grid=(n_seq,),
        out_specs=[pl.BlockSpec((1, N_KV, WINDOW, LANE), lambda b: (b, 0, 0, 0)),
                   pl.BlockSpec((1, N_KV, LANE, WINDOW), lambda b: (b, 0, 0, 0))],
        out_shape=[jax.ShapeDtypeStruct((n_seq, N_KV, padded, LANE), BF16),
                   jax.ShapeDtypeStruct((n_seq, N_KV, LANE, padded), BF16)],
        compiler_params=_params(1),
        name="window_pads",
    )()


def _project(x2d, ln_g, w_perm, cos, sin, prefix, conv_w8, carried, window_pads, *, layer, n_seq, rows_per_seq, tm,
             stride, carry, transposed, pos_per_seq):
    nt = rows_per_seq // tm
    n_rows = n_seq * rows_per_seq
    depth = w_perm.shape[0]
    row_map = lambda b, t: (b * nt + t, 0)
    pos_map = (lambda b, t: (t, 0)) if pos_per_seq else row_map
    in_specs = [
        pl.BlockSpec((tm, D_MODEL), row_map),
        _layer_spec((1, D_MODEL), layer),
        _layer_spec((D_MODEL, N_PROJ), layer),
        pl.BlockSpec((tm, LANE), pos_map),
        pl.BlockSpec((tm, LANE), pos_map),
        pl.BlockSpec((1, carry, CONV_DIM), lambda b, t: (b, 0, 0)),
        _layer_spec((SUBLANE, CONV_DIM), layer),
    ]
    rows = lambda width, dtype: (jax.ShapeDtypeStruct((n_rows, width), dtype), pl.BlockSpec((tm, width), row_map))
    if transposed:
        keep = min(WINDOW, rows_per_seq)
        assert keep % tm == 0
        first_kept = (rows_per_seq - keep) // tm
        kv_out = (jax.ShapeDtypeStruct((depth, n_seq, 4 * KV_DIM, rows_per_seq), F32),
                  pl.BlockSpec((None, None, 4 * KV_DIM, tm), lambda b, t: (layer, b, 0, t)))
        win_out = (jax.ShapeDtypeStruct((depth, n_seq, 2 * KV_DIM, keep), F32),
                   pl.BlockSpec((None, None, 2 * KV_DIM, tm),
                                lambda b, t: (layer, b, 0, jnp.maximum(t - first_kept, 0))))
    else:
        kv_out, win_out = rows(4 * KV_DIM, F32), rows(2 * KV_DIM, F32)
    outs = [
        rows(CONV_DIM, BF16),
        kv_out,
        win_out,
        rows(ATT_DIM, F32),
        rows(2 * D_MODEL, F32),
        (jax.ShapeDtypeStruct((n_seq, carry, CONV_DIM), F32),
         pl.BlockSpec((1, carry, CONV_DIM), lambda b, t: (b, 0, 0))),
    ]
    if transposed:
        tok_sub = (jax.ShapeDtypeStruct((n_seq, N_KV, rows_per_seq, LANE), BF16),
                   pl.BlockSpec((1, N_KV, tm, LANE), lambda b, t: (b, 0, t, 0)))
        tok_lane = lambda n, rows_, dtype: (jax.ShapeDtypeStruct((n_seq, n, rows_, rows_per_seq), dtype),
                                            pl.BlockSpec((1, n, rows_, tm), lambda b, t: (b, 0, 0, t)))
        assert WINDOW % tm == 0
        padded = rows_per_seq + WINDOW
        win_sub = (jax.ShapeDtypeStruct((n_seq, N_KV, padded, LANE), BF16),
                   pl.BlockSpec((1, N_KV, tm, LANE), lambda b, t: (b, 0, t + WINDOW // tm, 0)))
        win_lane = (jax.ShapeDtypeStruct((n_seq, N_KV, LANE, padded), BF16),
                    pl.BlockSpec((1, N_KV, LANE, tm), lambda b, t: (b, 0, 0, t + WINDOW // tm)))
        outs += [tok_lane(N_HEADS, HEAD_DIM, BF16),
                 tok_sub, tok_lane(N_KV, LANE, BF16),
                 win_sub, win_lane,
                 tok_lane(N_KV, N_GATE_ROWS, F32),
                 (jax.ShapeDtypeStruct((2 * KV_SLABS, n_rows, LANE), F32),
                  pl.BlockSpec((2 * KV_SLABS, tm, LANE), lambda b, t: (0, b * nt + t, 0)))]
    else:
        outs += [rows(ATT_DIM, BF16), rows(N_KV * LANE, F32)]
    args = [x2d, ln_g, w_perm, cos, sin, prefix, conv_w8]
    aliases = {}
    if carried is not None:
        assert transposed
        aliases = {len(args): 1, len(args) + 1: 2}
        args += list(carried)
    if window_pads is not None:
        aliases.update({len(args): 9, len(args) + 1: 10})
        args += list(window_pads)
    in_specs += [pl.BlockSpec(memory_space=pl.ANY)] * len(aliases)
    return pl.pallas_call(
        functools.partial(_proj_kernel, tm=tm, stride=stride, carry=carry, transposed=transposed,
                          n_carried=len(aliases)),
        grid=(n_seq, nt),
        in_specs=in_specs,
        out_specs=[o[1] for o in outs],
        out_shape=[o[0] for o in outs],
        scratch_shapes=[pltpu.VMEM((carry + tm, CONV_DIM), F32)],
        input_output_aliases=aliases,
        compiler_params=_params(2),
        name="project_prompt" if transposed else "project_sample",
    )(*args)


def _merge_kernel(x_ref, a_ref, gm_ref, p_ref, woa_ref, wob_ref, wo_ref, wple_ref, wpg_ref, fg_ref, y_ref, *rest,
                  y_needs_gate, final):
    br_a = jnp.dot(a_ref[...], woa_ref[...], preferred_element_type=F32)
    if y_needs_gate:
        zb_ref, o_ref = rest
        y = (zb_ref[...] * y_ref[...]).astype(BF16)
    else:
        (o_ref,) = rest
        y = y_ref[...]
    br_b = jnp.dot(y, wob_ref[...], preferred_element_type=F32)
    gm = gm_ref[...]
    mix = gm[:, :D_MODEL] * br_a + gm[:, D_MODEL:] * br_b
    x1 = x_ref[...] + jnp.dot(mix.astype(BF16), wo_ref[...], preferred_element_type=F32)
    gate = _sigmoid(jnp.dot(x1.astype(BF16), wpg_ref[...], preferred_element_type=F32))
    x2 = x1 + gate * jnp.dot(p_ref[...].astype(BF16), wple_ref[...], preferred_element_type=F32)
    if final:
        ms = jnp.mean(x2 * x2, axis=-1, keepdims=True)
        x2 = x2 * lax.rsqrt(ms + RMS_EPS) * fg_ref[...]
    o_ref[...] = x2


def _merge(x2d, a_act, y, zb_act, gmix, p2d, w_out_a, w_out_b, w_o, w_ple, w_pg, final_g, *, layer, tm, final):
    n_rows = x2d.shape[0]
    row = lambda width: pl.BlockSpec((tm, width), lambda t: (t, 0))
    y_args = (y,) if zb_act is None else (y, zb_act)
    return pl.pallas_call(
        functools.partial(_merge_kernel, y_needs_gate=zb_act is not None, final=final),
        grid=(n_rows // tm,),
        in_specs=[row(D_MODEL), row(CONV_DIM), row(2 * D_MODEL),
                  pl.BlockSpec((None, tm, PLE_DIM), lambda t: (layer, t, 0)),
                  _layer_spec((CONV_DIM, D_MODEL), layer), _layer_spec((ATT_DIM, D_MODEL), layer),
                  _layer_spec((D_MODEL, D_MODEL), layer), _layer_spec((PLE_DIM, D_MODEL), layer),
                  _layer_spec((D_MODEL, D_MODEL), layer), _const_spec((1, D_MODEL))] + [row(ATT_DIM)] * len(y_args),
        out_specs=row(D_MODEL),
        out_shape=jax.ShapeDtypeStruct((n_rows, D_MODEL), F32),
        compiler_params=_params(1),
        name="merge_prompt" if zb_act is None else "merge_sample",
    )(x2d, a_act, gmix, p2d, w_out_a, w_out_b, w_o, w_ple, w_pg, final_g, *y_args)


def _compress_kernel(pt_ref, *refs, n_pages, prompt):
    pages = refs[:n_pages]
    pos_ref, w1_ref, w2_ref, ko_ref, vo_ref, rows_scr, x_scr, carry_scr = refs[n_pages:]
    st = pl.program_id(1)
    per_page = PAGE_SIZE // CMP_STRIDE
    n = n_pages * per_page
    low = _iota((n, LANE), 1) < HEAD_DIM

    @pl.when(st == 0)
    def _():
        carry_scr[...] = jnp.zeros_like(carry_scr)

    row0 = _iota((n, CMP_HID), 0) == 0
    def relayout(slot):
        for j in range(n_pages):
            for v in range(KV_SLABS):
                if prompt:
                    piece = pages[j][slot * KV_SLABS + v]
                else:
                    piece = pages[j][slot, v * LANE:(v + 1) * LANE, :].T
                rows_scr[slot, v, j * PAGE_SIZE:(j + 1) * PAGE_SIZE, :] = piece
        for pair in range(CMP_STRIDE // 2):
            for v in range(KV_SLABS):
                a = rows_scr[slot, v, pl.ds(2 * pair, n, stride=CMP_STRIDE), :]
                b = rows_scr[slot, v, pl.ds(2 * pair + 1, n, stride=CMP_STRIDE), :]
                x_scr[slot, 2 * v * n:(2 * v + 1) * n, pair * LANE:(pair + 1) * LANE] = jnp.where(
                    low, a, pltpu.roll(b, HEAD_DIM, 1))
                x_scr[slot, (2 * v + 1) * n:(2 * v + 2) * n, pair * LANE:(pair + 1) * LANE] = jnp.where(
                    low, pltpu.roll(a, HEAD_DIM, 1), b)
    for slot, out_ref in enumerate((ko_ref, vo_ref)):
        relayout(slot)
        pos = pos_ref[slot]
        f = s = None
        for k0 in range(0, x_scr.shape[2], CMP_DEPTH):
            ks = slice(k0, k0 + CMP_DEPTH)
            xs = x_scr[slot, :, ks]
            f_part = jnp.dot((xs + pos[0:1, ks]).astype(BF16), w1_ref[slot, 0, ks, :], preferred_element_type=F32)
            s_part = jnp.dot((xs + pos[1:2, ks]).astype(BF16), w1_ref[slot, 1, ks, :], preferred_element_type=F32)
            f = f_part if f is None else f + f_part
            s = s_part if s is None else s + s_part
        acc = None
        for g in range(N_KV):
            fg = f[g * n:(g + 1) * n]
            prev = carry_scr[slot, g, SUBLANE - 1:SUBLANE, :]
            shifted = jnp.where(row0, prev, pltpu.roll(fg, 1, 0))
            carry_scr[slot, g] = fg[n - SUBLANE:n]
            pre = shifted + s[g * n:(g + 1) * n]
            hid = (pre * _sigmoid(pre)).astype(BF16)
            if prompt and slot == 0:
                out_ref[0, g] = jnp.dot(hid, w2_ref[0, 0, :, :LANE], preferred_element_type=F32).astype(BF16)
            elif prompt:
                out_ref[0, g] = lax.dot_general(w2_ref[1, 0, :LANE, :], hid, NT_DIMS,
                                                preferred_element_type=F32).astype(BF16)
            else:
                part = jnp.dot(hid, w2_ref[slot, g], preferred_element_type=F32)
                acc = part if acc is None else acc + part
        if not prompt:
            out_ref[0] = acc.astype(BF16)


def _compress(pages_arr, page_table, page_index, pos2, w1, w2, *, layer, n_pages, prompt):
    n_seq, pages_per_seq = page_table.shape
    n_steps = pages_per_seq // n_pages
    n = n_pages * (PAGE_SIZE // CMP_STRIDE)
    n_chunk = pages_per_seq * (PAGE_SIZE // CMP_STRIDE)
    flat = CMP_STRIDE * HEAD_DIM

    def page_spec(j):
        if prompt:
            return pl.BlockSpec((2 * KV_SLABS, PAGE_SIZE, LANE),
                                lambda b, s, pt: (0, page_index(pt[b, s * n_pages + j]), 0))
        return pl.BlockSpec((2, KV_DIM, PAGE_SIZE), lambda b, s, pt: (page_index(pt[b, s * n_pages + j]), 0, 0))

    if prompt:
        out_sds = [jax.ShapeDtypeStruct((n_seq, N_KV, n_chunk, LANE), BF16),
                   jax.ShapeDtypeStruct((n_seq, N_KV, LANE, n_chunk), BF16)]
        out_specs = [pl.BlockSpec((1, N_KV, n, LANE), lambda b, s, pt: (b, 0, s, 0)),
                     pl.BlockSpec((1, N_KV, LANE, n), lambda b, s, pt: (b, 0, 0, s))]
    else:
        out_sds = [jax.ShapeDtypeStruct((n_seq, n_chunk, KV_DIM), BF16)] * 2
        out_specs = [pl.BlockSpec((1, n, KV_DIM), lambda b, s, pt: (b, s, 0))] * 2
    grid_spec = pltpu.PrefetchScalarGridSpec(
        num_scalar_prefetch=1,
        grid=(n_seq, n_steps),
        in_specs=[page_spec(j) for j in range(n_pages)] + [_layer_spec(a.shape[1:], layer) for a in (pos2, w1, w2)],
        out_specs=out_specs,
        scratch_shapes=[pltpu.VMEM((2, KV_SLABS, n_pages * PAGE_SIZE, LANE), F32),
                        pltpu.VMEM((2, N_KV * n, flat), F32),
                        pltpu.VMEM((2, N_KV, SUBLANE, CMP_HID), F32)],
    )
    return pl.pallas_call(
        functools.partial(_compress_kernel, n_pages=n_pages, prompt=prompt),
        grid_spec=grid_spec,
        out_shape=out_sds,
        compiler_params=_params(2),
        name="compress_prompt" if prompt else "compress_sample",
    )(page_table, *([pages_arr] * n_pages), pos2, w1, w2)


def _select_bias(score_ref, cnt_ref, n_real, n_live, n_sel):
    score = score_ref[...]
    n_rows = -(-n_real // SUBLANE) * SUBLANE
    sub = _iota((SUBLANE, score.shape[1]), 0)
    cnt_ref[...] = jnp.zeros(score.shape, F32)

    def group(i0):
        cnt = cnt_ref[0:n_rows, :]
        for i in range(i0, min(i0 + SUBLANE, n_real)):
            si = score_ref[i:i + 1, :]
            pieces = []
            for r0 in range(0, n_rows, SUBLANE):
                sv = score[r0:r0 + SUBLANE]
                if r0 > i:
                    pieces.append(jnp.where(si >= sv, 1.0, 0.0))
                elif r0 + SUBLANE - 1 < i:
                    pieces.append(jnp.where(si > sv, 1.0, 0.0))
                else:
                    pieces.append(jnp.where(sub > i - r0, jnp.where(si >= sv, 1.0, 0.0), jnp.where(si > sv, 1.0, 0.0)))
            cnt = cnt + jnp.concatenate(pieces, axis=0)
        cnt_ref[0:n_rows, :] = cnt

    for i0 in range(0, n_real, SUBLANE):
        if n_live is None:
            group(i0)
        else:
            pl.when(i0 < n_live)(functools.partial(group, i0))
    return jnp.where((cnt_ref[...] < n_sel) & (score >= 0.0), 0.0, NEG_INF)


def _attn_prompt_kernel(qt_ref, gbt_ref, zb_ref, tri_ref, kc_ref, vct_ref, ks_ref, vst_ref, kw_ref, vwt_ref, o_ref,
                        score_scr, cnt_scr, m_scr, acc_scr, mx_scr, s_even_scr, s_odd_scr, *, tq, n_cmp, n_blk):
    qi = pl.program_id(2)
    s0 = qi * tq
    rows = GROUP * tq
    cw = min(rows, CHAIN_COLS)
    col_groups = [slice(c, c + cw) for c in range(0, rows, cw)]
    qpos = s0 + _mod(_iota((1, rows), 1), tq)

    def stack_q(lower):
        return jnp.concatenate([jnp.concatenate([qt_ref[0, r], lower], axis=0) for r in range(GROUP)], axis=1)

    q0 = stack_q(jnp.where(_iota((LANE - HEAD_DIM, tq), 0) == 0, NEG_INF, 0.0).astype(BF16))

    s_c = jnp.dot(kc_ref[0, 0], q0, preferred_element_type=F32)

    def window_scores(cs, first, n_keys):
        keys = kw_ref[0, 0, pl.ds(pl.multiple_of(s0 + first, tq), n_keys), :]
        return jnp.dot(keys, q0[:, cs], preferred_element_type=F32)

    w_parts = []
    for cs in col_groups:
        parts = [window_scores(cs, 0, tq) + tri_ref[0, :, cs]]
        if WINDOW > tq:
            parts.append(window_scores(cs, tq, WINDOW - tq))
        parts.append(window_scores(cs, WINDOW, tq) + tri_ref[1, :, cs])
        w_parts.append(parts)

    m_idx = _iota((n_cmp, 1), 0)
    mask_c = (m_idx >= 1) & (CMP_STRIDE * m_idx + CMP_STRIDE - 1 <= qpos)
    s_c = jnp.where(mask_c, s_c, NEG_INF)
    e_c = jnp.where(mask_c, jnp.exp2(s_c - jnp.max(s_c, axis=0, keepdims=True)), 0.0)
    p_c = e_c * (1.0 / jnp.maximum(jnp.sum(e_c, axis=0, keepdims=True), 1e-30))
    o_c = jnp.dot(vct_ref[0, 0], p_c.astype(BF16), preferred_element_type=F32)

    p_sum = p_c[:, 0:tq]
    for r in range(1, GROUP):
        p_sum = p_sum + p_c[:, r * tq:(r + 1) * tq]
    p_hi = p_sum.astype(BF16)
    p_lo = (p_sum - p_hi.astype(F32)).astype(BF16)
    ov_t = _overlap_t(MAX_BLOCKS, n_cmp)
    imp = jnp.dot(ov_t, p_hi, preferred_element_type=F32) + jnp.dot(ov_t, p_lo, preferred_element_type=F32)

    w_out = []
    for cs, parts in zip(col_groups, w_parts):
        m_w = functools.reduce(jnp.maximum, [jnp.max(s, axis=0, keepdims=True) for s in parts])
        p_w = jnp.concatenate([jnp.exp2(s - m_w).astype(BF16) for s in parts], axis=0)
        acc_w = jnp.dot(vwt_ref[0, 0, :, pl.ds(pl.multiple_of(s0, tq), WINDOW + tq)], p_w, preferred_element_type=F32)
        w_out.append(acc_w[:HEAD_DIM] / jnp.maximum(acc_w[HEAD_DIM:HEAD_DIM + 1], 1e-30))
    o_w = jnp.concatenate(w_out, axis=1)

    blk = _iota((MAX_BLOCKS, tq), 0)
    cur = _div(s0 + _iota((MAX_BLOCKS, tq), 1), SLC_BLOCK)
    forced = (blk == 0) | (blk == cur) | (blk == cur - 1)
    score = jnp.where(forced, FORCE_SCORE, imp)
    score_scr[...] = jnp.where(blk <= cur, score, -1.0)
    n_live = jnp.minimum(n_blk, (s0 + tq) // SLC_BLOCK)
    bias_t = _select_bias(score_scr, cnt_scr, n_blk, n_live, min(TOP_N, n_blk))
    qx = stack_q(bias_t.astype(BF16))

    m_scr[...] = jnp.full(m_scr.shape, NEG_INF, F32)
    acc_scr[...] = jnp.zeros(acc_scr.shape, F32)
    s_bufs = (s_even_scr, s_odd_scr)

    def scores(k0, slot, causal):
        keys = ks_ref[0, 0, pl.ds(pl.multiple_of(k0, tq), tq), :]
        prods = [jnp.dot(keys, qx[:, cs], preferred_element_type=F32) for cs in col_groups]
        for cs, s in zip(col_groups, prods):
            if causal:
                s = s + tri_ref[1, :, cs]
            s_bufs[slot][:, cs] = s
            mx_scr[slot, 0:1, cs] = jnp.max(s, axis=0, keepdims=True)

    def accumulate(k0, slot):
        vals_t = vst_ref[0, 0, :, pl.ds(pl.multiple_of(k0, tq), tq)]
        for cs in col_groups:
            m_old = m_scr[0:1, cs]
            m_new = jnp.maximum(m_old, mx_scr[slot, 0:1, cs])
            p = jnp.exp2(s_bufs[slot][:, cs] - m_new).astype(BF16)
            acc_scr[:, cs] = jnp.exp2(m_old - m_new) * acc_scr[:, cs] + jnp.dot(vals_t, p, preferred_element_type=F32)
            m_scr[0:1, cs] = m_new

    n_pairs = qi // 2
    scores(s0, 0, True)

    def pair(j, _):
        scores(2 * j * tq, 1, False)
        accumulate(jnp.where(j == 0, s0, (2 * j - 1) * tq), 0)
        scores((2 * j + 1) * tq, 0, False)
        accumulate(2 * j * tq, 1)
        return 0

    lax.fori_loop(0, n_pairs, pair, 0)
    pending = jnp.where(n_pairs == 0, s0, (2 * n_pairs - 1) * tq)

    @pl.when(qi - 2 * n_pairs == 1)
    def _():
        scores(s0 - tq, 1, False)
        accumulate(pending, 0)
        accumulate(s0 - tq, 1)

    @pl.when(qi - 2 * n_pairs == 0)
    def _():
        accumulate(pending, 0)

    o_s = acc_scr[:HEAD_DIM, :] / jnp.maximum(acc_scr[HEAD_DIM:HEAD_DIM + 1, :], 1e-30)

    gb = gbt_ref[0, 0]
    zb = zb_ref[0]
    heads = []
    for r in range(GROUP):
        rs = slice(r * tq, (r + 1) * tq)
        heads.append(gb[3 * r:3 * r + 1] * o_c[:HEAD_DIM, rs] + gb[3 * r + 1:3 * r + 2] * o_s[:, rs]
                     + gb[3 * r + 2:3 * r + 3] * o_w[:, rs])
    for pair in range(GROUP // 2):
        slab = jnp.concatenate([heads[2 * pair], heads[2 * pair + 1]], axis=0).T
        o_ref[0, :, pair * LANE:(pair + 1) * LANE] = (zb[:, pair * LANE:(pair + 1) * LANE] * slab).astype(BF16)


def _attn_prompt(q_t, gb_t, zb_act, kc_ext, vc_t, ks_ext, vs_t, kw_pad, vw_pad, *, tq):
    n_b, seq_len = zb_act.shape[0], zb_act.shape[1]
    n_cmp = kc_ext.shape[2]
    n_blk = seq_len // SLC_BLOCK
    head_w = GROUP * HEAD_DIM
    rows = GROUP * tq
    assert WINDOW % tq == 0 and rows % min(rows, CHAIN_COLS) == 0
    off = _iota((tq, rows), 0) - _mod(_iota((tq, rows), 1), tq)
    tri = jnp.stack([jnp.where(off > 0, 0.0, NEG_INF), jnp.where(off <= 0, 0.0, NEG_INF)]).astype(F32)
    tok_sub = lambda n: pl.BlockSpec((1, 1, n, LANE), lambda b, g, i: (b, g, 0, 0))
    tok_lane = lambda n: pl.BlockSpec((1, 1, LANE, n), lambda b, g, i: (b, g, 0, 0))
    return pl.pallas_call(
        functools.partial(_attn_prompt_kernel, tq=tq, n_cmp=n_cmp, n_blk=n_blk),
        grid=(n_b, N_KV, seq_len // tq),
        in_specs=[pl.BlockSpec((1, GROUP, HEAD_DIM, tq), lambda b, g, i: (b, g, 0, i)),
                  pl.BlockSpec((1, 1, N_GATE_ROWS, tq), lambda b, g, i: (b, g, 0, i)),
                  pl.BlockSpec((1, tq, head_w), lambda b, g, i: (b, i, g)),
                  _const_spec((2, tq, rows)),
                  tok_sub(n_cmp), tok_lane(n_cmp), tok_sub(seq_len), tok_lane(seq_len),
                  tok_sub(seq_len + WINDOW), tok_lane(seq_len + WINDOW)],
        out_specs=pl.BlockSpec((1, tq, head_w), lambda b, g, i: (b, i, g)),
        out_shape=jax.ShapeDtypeStruct((n_b, seq_len, ATT_DIM), BF16),
        scratch_shapes=[pltpu.VMEM((MAX_BLOCKS, tq), F32), pltpu.VMEM((MAX_BLOCKS, tq), F32),
                        pltpu.VMEM((SUBLANE, rows), F32),
                        pltpu.VMEM((LANE, rows), F32), pltpu.VMEM((2, SUBLANE, rows), F32),
                        pltpu.VMEM((tq, rows), F32), pltpu.VMEM((tq, rows), F32)],
        compiler_params=_params(3),
        name="attention_prompt",
    )(q_t, gb_t, zb_act, tri, kc_ext, vc_t, ks_ext, vs_t, kw_pad, vw_pad)


def _attn_sample_kernel(pt_ref, *refs, n_pages, past_len, dec_len, n_blk_pad):
    pages = refs[:n_pages]
    (qbd_ref, qbdt_ref, kc_ref, vc_ref, newkv_ref, win_ref, newwin_ref, gate_ref, o_ref,
     score_scr, cnt_scr, bias_scr, m_scr, l_scr, acc_scr, oc_scr) = refs[n_pages:]
    st = pl.program_id(1)
    n_steps = pl.num_programs(1)
    cols = LANE
    tslots = cols // N_HEADS
    n_cmp = kc_ref.shape[1]
    n_past_blk = past_len // SLC_BLOCK
    per_page = PAGE_SIZE // SLC_BLOCK
    qbdt = qbdt_ref[0]
    tok_c = _mod(_iota((cols, 1), 0), tslots)

    @pl.when(st == 0)
    def _():
        qpos = past_len + _mod(_iota((1, cols), 1), tslots)
        s_c = jnp.dot(kc_ref[0], qbd_ref[0], preferred_element_type=F32)
        m_idx = _iota((n_cmp, 1), 0)
        mask_c = (m_idx >= 1) & (CMP_STRIDE * m_idx + CMP_STRIDE - 1 <= qpos)
        s_c = jnp.where(mask_c, s_c, NEG_INF)
        e_c = jnp.where(mask_c, jnp.exp(s_c - jnp.max(s_c, axis=0, keepdims=True)), 0.0)
        p_c = e_c / jnp.maximum(jnp.sum(e_c, axis=0, keepdims=True), 1e-30)
        p_hi = p_c.astype(BF16)
        p_lo = (p_c - p_hi.astype(F32)).astype(BF16)
        oc_scr[...] = jnp.dot(p_c.T.astype(BF16), vc_ref[0], preferred_element_type=F32)
        ov_t = _overlap_t(n_blk_pad, n_cmp)
        imp1 = jnp.dot(ov_t, p_hi, preferred_element_type=F32) + jnp.dot(ov_t, p_lo, preferred_element_type=F32)
        ca = _iota((cols, cols), 0)
        cb = _iota((cols, cols), 1)
        per_group = GROUP * tslots
        same = jnp.where((_div(ca, per_group) == _div(cb, per_group)) & (_mod(ca, tslots) == _mod(cb, tslots)),
                         1.0, 0.0).astype(BF16)
        i_hi = imp1.astype(BF16)
        i_mid = (imp1 - i_hi.astype(F32)).astype(BF16)
        i_lo = (imp1 - i_hi.astype(F32) - i_mid.astype(F32)).astype(BF16)
        imp = (jnp.dot(i_hi, same, preferred_element_type=F32) + jnp.dot(i_mid, same, preferred_element_type=F32)
               + jnp.dot(i_lo, same, preferred_element_type=F32))
        blk = _iota((n_blk_pad, cols), 0)
        cur = _div(qpos, SLC_BLOCK)
        forced = (blk == 0) | (blk == cur) | (blk == cur - 1)
        score = jnp.where(forced, FORCE_SCORE, imp)
        score_scr[...] = jnp.where(blk <= cur, score, -1.0)
        n_blk = n_past_blk + 1
        bias_scr[...] = _select_bias(score_scr, cnt_scr, n_blk, None, min(TOP_N, n_blk)).T
        m_scr[...] = jnp.full(m_scr.shape, NEG_INF, F32)
        l_scr[...] = jnp.zeros(l_scr.shape, F32)
        acc_scr[...] = jnp.zeros(acc_scr.shape, F32)

    def update(s, pv_fn):
        m_old = m_scr[...]
        m_new = jnp.maximum(m_old, jnp.max(s, axis=1, keepdims=True))
        alpha = jnp.exp(m_old - m_new)
        p = jnp.exp(s - m_new)
        l_scr[...] = alpha * l_scr[...] + jnp.sum(p, axis=1, keepdims=True)
        acc_scr[...] = alpha * acc_scr[...] + pv_fn(p.astype(BF16))
        m_scr[...] = m_new

    bias_b = bias_scr[...].astype(BF16)
    blk0 = st * (n_pages * per_page)
    key_blk = blk0 + _div(_iota((n_blk_pad, n_pages * PAGE_SIZE), 1), SLC_BLOCK)
    expand = jnp.where(_iota((n_blk_pad, n_pages * PAGE_SIZE), 0) == key_blk, 1.0, 0.0).astype(BF16)
    keys_t = jnp.concatenate([pages[j][0].astype(BF16) for j in range(n_pages)], axis=1)
    vals_t = jnp.concatenate([pages[j][1].astype(BF16) for j in range(n_pages)], axis=1)
    s = (jnp.dot(qbdt, keys_t, preferred_element_type=F32) + jnp.dot(bias_b, expand, preferred_element_type=F32))
    update(s, lambda p: lax.dot_general(p, vals_t, NT_DIMS, preferred_element_type=F32))

    @pl.when(st == n_steps - 1)
    def _():
        def pad_rows(a):
            return jnp.concatenate([a, jnp.zeros((LANE - a.shape[0], a.shape[1]), a.dtype)], axis=0)

        key_i = _iota((1, LANE), 1)
        new = pad_rows(newkv_ref[0])
        s_n = lax.dot_general(qbdt, new[:, 2 * KV_DIM:3 * KV_DIM].astype(BF16), NT_DIMS, preferred_element_type=F32)
        s_n = s_n + bias_scr[:, n_past_blk:n_past_blk + 1]
        s_n = jnp.where((key_i <= tok_c) & (key_i < dec_len), s_n, NEG_INF)
        v_n = new[:, 3 * KV_DIM:].astype(BF16)
        update(s_n, lambda p: jnp.dot(p, v_n, preferred_element_type=F32))
        o_s = acc_scr[...] / jnp.maximum(l_scr[...], 1e-30)

        w_buf = win_ref.shape[2]
        neww = pad_rows(newwin_ref[0])
        s_w = jnp.concatenate(
            [jnp.dot(qbdt, win_ref[0].astype(BF16), preferred_element_type=F32),
             lax.dot_general(qbdt, neww[:, :KV_DIM].astype(BF16), NT_DIMS, preferred_element_type=F32)], axis=1)
        idx = _iota((1, w_buf + LANE), 1)
        kpos = past_len - w_buf + idx
        dist = past_len + tok_c - kpos
        mask_w = (dist >= 0) & (dist < WINDOW) & (idx < w_buf + dec_len) & (kpos >= 0)
        s_w = jnp.where(mask_w, s_w, NEG_INF)
        e_w = jnp.where(mask_w, jnp.exp(s_w - jnp.max(s_w, axis=1, keepdims=True)), 0.0)
        p_w = (e_w / jnp.maximum(jnp.sum(e_w, axis=1, keepdims=True), 1e-30)).astype(BF16)
        o_w = (lax.dot_general(p_w[:, :w_buf], win_ref[1].astype(BF16), NT_DIMS, preferred_element_type=F32)
               + jnp.dot(p_w[:, w_buf:], neww[:, KV_DIM:].astype(BF16), preferred_element_type=F32))
        gate = gate_ref[0]
        o_ref[0] = gate[:, 0:1] * oc_scr[...] + gate[:, 1:2] * o_s + gate[:, 2:3] * o_w


def _attn_sample(cache_t, page_table, page_index, win_t, win_index, qbd, qbdt, kc_flat, vc_flat, newkv8, newwin8,
                 gates, *, n_pages, past_len, dec_len):
    n_seq, pages_per_seq = page_table.shape
    n_steps = pages_per_seq // n_pages
    n_cmp = kc_flat.shape[1]
    n_blk_pad = -(-(past_len // SLC_BLOCK + 1) // LANE) * LANE
    w_buf = win_t.shape[2]

    def page_spec(j):
        return pl.BlockSpec((2, KV_DIM, PAGE_SIZE), lambda b, s, pt: (page_index(pt[b, s * n_pages + j]), 0, 0))

    per_seq = lambda shape: pl.BlockSpec((1,) + shape, lambda b, s, pt: (b,) + (0,) * len(shape))
    grid_spec = pltpu.PrefetchScalarGridSpec(
        num_scalar_prefetch=1,
        grid=(n_seq, n_steps),
        in_specs=[page_spec(j) for j in range(n_pages)] + [
            per_seq((KV_DIM, LANE)), per_seq((LANE, KV_DIM)), per_seq((n_cmp, KV_DIM)), per_seq((n_cmp, KV_DIM)),
            per_seq((SUBLANE, 4 * KV_DIM)),
            pl.BlockSpec((2, KV_DIM, w_buf), lambda b, s, pt: (win_index(b), 0, 0)),
            per_seq((SUBLANE, 2 * KV_DIM)), per_seq((LANE, LANE))],
        out_specs=per_seq((LANE, KV_DIM)),
        scratch_shapes=[pltpu.VMEM((n_blk_pad, LANE), F32), pltpu.VMEM((n_blk_pad, LANE), F32),
                        pltpu.VMEM((LANE, n_blk_pad), F32),
                        pltpu.VMEM((LANE, 1), F32), pltpu.VMEM((LANE, 1), F32),
                        pltpu.VMEM((LANE, KV_DIM), F32), pltpu.VMEM((LANE, KV_DIM), F32)],
    )
    return pl.pallas_call(
        functools.partial(_attn_sample_kernel, n_pages=n_pages, past_len=past_len, dec_len=dec_len,
                          n_blk_pad=n_blk_pad),
        grid_spec=grid_spec,
        out_shape=jax.ShapeDtypeStruct((n_seq, LANE, KV_DIM), F32),
        compiler_params=_params(2),
        name="attention_sample",
    )(page_table, *([cache_t] * n_pages), qbd, qbdt, kc_flat, vc_flat, newkv8, win_t, newwin8, gates)


def _permute_w_in(w):
    lead = w.shape[:-1]
    c = np.cumsum([0, 4 * CONV_DIM, ATT_DIM, 4 * KV_DIM, 2 * KV_DIM, 3 * N_HEADS, ATT_DIM, 2 * D_MODEL])
    wa, wq, wkv, wwin, wgb, wzb, wgm = (w[..., c[i]:c[i + 1]] for i in range(7))
    wgb = jnp.pad(wgb.reshape(lead + (N_KV, 3 * GROUP)), ((0, 0),) * (len(lead) + 1) + ((0, LANE - 3 * GROUP),))
    return jnp.concatenate([wa, wq, wkv, wwin, wzb, wgm, wgb.reshape(lead + (N_KV * LANE,))], axis=-1).astype(BF16)


def _rope_tables(pos):
    half = HEAD_DIM // 2
    inv = jnp.power(ROPE_THETA, -jnp.arange(half, dtype=F32) / half)
    ang = pos.astype(F32)[:, None] * inv[None, :]
    cos, sin = jnp.cos(ang), jnp.sin(ang)
    reps = LANE // HEAD_DIM
    return jnp.tile(jnp.concatenate([cos, cos], axis=1), (1, reps)), jnp.tile(jnp.concatenate([-sin, sin], axis=1), (1, reps))


def _compress_weights(pos_k, w1_k, w2_k, pos_v, w1_v, w2_v, grouped):
    depth = pos_k.shape[0]
    flat = CMP_STRIDE * HEAD_DIM
    pos2 = jnp.stack([jnp.pad(p.reshape(depth, 2, flat), ((0, 0), (0, SUBLANE - 2), (0, 0))) for p in (pos_k, pos_v)],
                     axis=1)
    w1 = jnp.stack([w.reshape(depth, 2, flat, CMP_HID) for w in (w1_k, w1_v)], axis=1).astype(BF16)
    if grouped:
        w2 = jnp.stack([jnp.pad(w2_k, ((0, 0), (0, 0), (0, CMP_HID - HEAD_DIM)))[:, None],
                        jnp.pad(w2_v.transpose(0, 2, 1), ((0, 0), (0, CMP_HID - HEAD_DIM), (0, 0)))[:, None]], axis=1)
    else:
        w2 = jnp.stack([jnp.stack([jnp.pad(w, ((0, 0), (0, 0), (g * HEAD_DIM, KV_DIM - (g + 1) * HEAD_DIM)))
                                   for g in range(N_KV)], axis=1) for w in (w2_k, w2_v)], axis=1)
    return pos2, w1, w2.astype(BF16)


def kernel(x_prompt, x_sample, cache_kv, state_win, state_conv, page_table, p_prompt, p_sample, ln_g, w_in, conv_w,
           cmp_pos_k, cmp_w1_k, cmp_w2_k, cmp_pos_v, cmp_w1_v, cmp_w2_v, w_out_a, w_out_b, w_o, w_ple, w_ple_gate,
           final_g):
    depth = ln_g.shape[0]
    n_b, seq_len, _ = x_prompt.shape
    n_dec, dec_len, _ = x_sample.shape
    pages_per_seq = page_table.shape[1]
    past_len = pages_per_seq * PAGE_SIZE
    n_pool = cache_kv.shape[1]
    w_buf = state_win.shape[2]

    tm_p = min(256, seq_len)
    tq = min(256, seq_len)
    tm_merge = min(512, seq_len)
    pages_p = min(16, seq_len // PAGE_SIZE)
    pages_s = min(16, pages_per_seq)
    rows_s = n_dec * dec_len
    carry_p = SUBLANE
    carry_s = (CONV_W - 1) * n_dec
    assert seq_len % PAGE_SIZE == 0 and seq_len // SLC_BLOCK <= MAX_BLOCKS and seq_len >= WINDOW + tq
    assert CONV_W - 1 <= dec_len <= SUBLANE and rows_s % SUBLANE == 0
    assert w_buf == min(WINDOW, past_len) and N_HEADS * SUBLANE == LANE

    cos_p, sin_p = _rope_tables(jnp.arange(seq_len, dtype=jnp.int32))
    pos_s = past_len + jnp.repeat(jnp.arange(dec_len, dtype=jnp.int32), n_dec)
    cos_s, sin_s = _rope_tables(pos_s)
    prompt_pages = jnp.arange(n_b * (seq_len // PAGE_SIZE), dtype=jnp.int32).reshape(n_b, seq_len // PAGE_SIZE)
    eye_g = jnp.eye(N_KV, dtype=F32)
    tpad = SUBLANE - dec_len
    cache_t = cache_kv.transpose(0, 1, 3, 4, 5, 2).reshape(depth * n_pool * 4, KV_DIM, PAGE_SIZE)
    win_t = state_win.transpose(0, 1, 3, 4, 5, 2).reshape(depth * n_dec * 2, KV_DIM, w_buf)

    xp = x_prompt.reshape(n_b * seq_len, D_MODEL)
    xs = x_sample.transpose(1, 0, 2).reshape(rows_s, D_MODEL)
    kv_s, win_new_s, conv_p, conv_s = [], [], [], []
    carried = None
    w_perm = _permute_w_in(w_in)
    g_rows = ln_g.reshape(depth, 1, D_MODEL)
    cw8 = jnp.pad(conv_w, ((0, 0), (0, SUBLANE - CONV_W), (0, 0)))
    merge_w = (w_out_a.astype(BF16), w_out_b.astype(BF16), w_o.astype(BF16), w_ple.astype(BF16),
               w_ple_gate.astype(BF16), final_g.reshape(1, D_MODEL))
    cmp_args = (cmp_pos_k, cmp_w1_k, cmp_w2_k, cmp_pos_v, cmp_w1_v, cmp_w2_v)
    cmp_w_prompt = _compress_weights(*cmp_args, True)
    cmp_w_sample = _compress_weights(*cmp_args, False)
    zero_prefix = jnp.zeros((n_b, carry_p, CONV_DIM), F32)
    p_sample_rows = p_sample.transpose(0, 2, 1, 3).reshape(depth, rows_s, PLE_DIM)
    for i in range(depth):
        last = i == depth - 1

        (a_act, newkv_all, win_all, zb_act, gmix, tail, q_t, ks_ext, vs_t, kw_pad, vw_pad, gb_t, cmp_rows) = _project(
            xp, g_rows, w_perm, cos_p, sin_p, zero_prefix, cw8, carried, _window_pads(n_b, seq_len), layer=i,
            n_seq=n_b, rows_per_seq=seq_len, tm=tm_p, stride=1, carry=carry_p, transposed=True, pos_per_seq=True)
        carried = (newkv_all, win_all)
        kc_ext, vc_t = _compress(cmp_rows, prompt_pages, lambda p: p, *cmp_w_prompt, layer=i, n_pages=pages_p,
                                 prompt=True)
        y_act = _attn_prompt(q_t, gb_t, zb_act.reshape(n_b, seq_len, ATT_DIM), kc_ext, vc_t, ks_ext, vs_t, kw_pad,
                             vw_pad, tq=tq)
        xp = _merge(xp, a_act, y_act.reshape(n_b * seq_len, ATT_DIM), None, gmix,
                    p_prompt.reshape(depth, n_b * seq_len, PLE_DIM), *merge_w, layer=i, tm=tm_merge, final=last)
        conv_p.append(tail[:, carry_p - (CONV_W - 1):])

        prefix_s = state_conv[i].transpose(1, 0, 2).reshape(1, carry_s, CONV_DIM)
        (a_act, newkv, win, zb_act, gmix, tail, qp, gb) = _project(
            xs, g_rows, w_perm, cos_s, sin_s, prefix_s, cw8, None, None, layer=i,
            n_seq=1, rows_per_seq=rows_s, tm=rows_s, stride=n_dec, carry=carry_s, transposed=False, pos_per_seq=False)
        base = i * n_pool
        kc_flat, vc_flat = _compress(cache_t, page_table, lambda p: (base + p) * 2, *cmp_w_sample, layer=i,
                                     n_pages=pages_s, prompt=False)
        q5 = jnp.pad(qp.reshape(dec_len, n_dec, N_KV, GROUP, HEAD_DIM).astype(F32),
                     ((0, tpad), (0, 0), (0, 0), (0, 0), (0, 0)))
        qbd = q5.transpose(1, 4, 2, 3, 0)[:, None] * eye_g[None, :, None, :, None, None]
        qbd = qbd.reshape(n_dec, KV_DIM, LANE).astype(BF16)
        qbdt = q5.transpose(1, 2, 3, 0, 4)[:, :, :, :, None] * eye_g[None, :, None, None, :, None]
        qbdt = qbdt.reshape(n_dec, LANE, KV_DIM).astype(BF16)
        gates = gb.reshape(dec_len, n_dec, N_KV, LANE)[..., :3 * GROUP].reshape(dec_len, n_dec, N_KV, GROUP, 3)
        gates = jnp.pad(gates.transpose(1, 2, 3, 0, 4), ((0, 0), (0, 0), (0, 0), (0, tpad), (0, LANE - 3)))
        gates = gates.reshape(n_dec, LANE, LANE)
        newkv_b = newkv.reshape(dec_len, n_dec, 4 * KV_DIM).transpose(1, 0, 2)
        newwin_b = win.reshape(dec_len, n_dec, 2 * KV_DIM).transpose(1, 0, 2)
        pad_rows = lambda a: jnp.pad(a, ((0, 0), (0, tpad), (0, 0)))
        o_all = _attn_sample(cache_t, page_table, lambda p: (base + p) * 2 + 1, win_t, lambda b: i * n_dec + b, qbd,
                             qbdt, kc_flat, vc_flat, pad_rows(newkv_b), pad_rows(newwin_b), gates, n_pages=pages_s,
                             past_len=past_len, dec_len=dec_len)
        o6 = o_all.reshape(n_dec, N_KV, GROUP, SUBLANE, N_KV, HEAD_DIM)
        yb = jnp.stack([o6[:, g, :, :, g] for g in range(N_KV)], axis=1)
        yb = yb.transpose(3, 0, 1, 2, 4)[:dec_len].reshape(rows_s, ATT_DIM)
        xs = _merge(xs, a_act, yb, zb_act, gmix, p_sample_rows, *merge_w, layer=i, tm=rows_s, final=last)
        kv_s.append(newkv_b.reshape(n_dec, dec_len, 4, N_KV, HEAD_DIM))
        win_new_s.append(newwin_b.reshape(n_dec, dec_len, 2, N_KV, HEAD_DIM))
        conv_s.append(tail.reshape(CONV_W - 1, n_dec, CONV_DIM).transpose(1, 0, 2))

    y_prompt = xp.reshape(n_b, seq_len, D_MODEL)
    y_sample = xs.reshape(dec_len, n_dec, D_MODEL).transpose(1, 0, 2)
    newkv_all, win_all = carried
    kv_p = newkv_all.reshape(depth, n_b, 4, N_KV, HEAD_DIM, seq_len).transpose(0, 1, 5, 2, 3, 4)
    win_p = win_all.reshape(depth, n_b, 2, N_KV, HEAD_DIM, min(WINDOW, seq_len)).transpose(0, 1, 5, 2, 3, 4)
    win_s = jnp.concatenate([state_win, jnp.stack(win_new_s)], axis=2)[:, :, -w_buf:]
    return (y_prompt, y_sample, kv_p, jnp.stack(kv_s), win_p, win_s, jnp.stack(conv_p), jnp.stack(conv_s))
```

```python
import functools

import numpy as np
import jax
import jax.numpy as jnp
from jax import lax
from jax.experimental import pallas as pl
from jax.experimental.pallas import tpu as pltpu

D_MODEL = 1024
CONV_DIM = D_MODEL
CONV_W = 3
N_HEADS = 16
HEAD_DIM = 64
N_KV = 4
GROUP = N_HEADS // N_KV
ATT_DIM = N_HEADS * HEAD_DIM
KV_DIM = N_KV * HEAD_DIM
PAGE_SIZE = 128
CMP_STRIDE = 16
CMP_BLOCK = 2 * CMP_STRIDE
CMP_HID = 4 * HEAD_DIM
SLC_BLOCK = 64
TOP_N = 16
WINDOW = 512
PLE_DIM = 256
ROPE_THETA = 10000.0
RMS_EPS = 1e-6
NEG_INF = -1e30
FORCE_SCORE = 1e6

LANE = 128
SUBLANE = 8
VMEM_LIMIT = 56 * 1024 * 1024
KV_SLABS = KV_DIM // LANE
CMP_DEPTH = 2 * LANE
CHAIN_COLS = 4 * LANE

OFF_A = 0
OFF_Q = 4 * CONV_DIM
OFF_KV = OFF_Q + ATT_DIM
OFF_WIN = OFF_KV + 4 * KV_DIM
OFF_ZB = OFF_WIN + 2 * KV_DIM
OFF_GM = OFF_ZB + ATT_DIM
OFF_GB = OFF_GM + 2 * D_MODEL
N_PROJ = OFF_GB + LANE
MAX_BLOCKS = LANE - HEAD_DIM

F32 = jnp.float32
BF16 = jnp.bfloat16
NT_DIMS = (((1,), (1,)), ((), ()))


def _params(n_axes):
    return pltpu.CompilerParams(dimension_semantics=("arbitrary",) * n_axes, vmem_limit_bytes=VMEM_LIMIT)


def _const_spec(shape):
    zeros = (0,) * len(shape)
    return pl.BlockSpec(shape, lambda *_: zeros, pipeline_mode=pl.Buffered(1))


def _layer_spec(shape, layer):
    index = (layer,) + (0,) * len(shape)
    return pl.BlockSpec((None,) + tuple(shape), lambda *_: index, pipeline_mode=pl.Buffered(1))


def _sigmoid(x):
    return 1.0 / (1.0 + jnp.exp(-x))


def _div(x, d):
    assert d & (d - 1) == 0
    return jnp.right_shift(x, d.bit_length() - 1)


def _mod(x, d):
    assert d & (d - 1) == 0
    return jnp.bitwise_and(x, d - 1)


def _iota(shape, dim):
    return lax.broadcasted_iota(jnp.int32, shape, dim)


def _rope_slab(x, cos, sin, lane):
    swapped = jnp.where(_mod(lane, HEAD_DIM) < HEAD_DIM // 2, pltpu.roll(x, LANE - HEAD_DIM // 2, 1),
                        pltpu.roll(x, HEAD_DIM // 2, 1))
    return x * cos + swapped * sin


def _overlap_t(n_blk, n_cmp):
    bj = _iota((n_blk, n_cmp), 0)
    bm = _iota((n_blk, n_cmp), 1)
    ratio = SLC_BLOCK // CMP_STRIDE
    return jnp.where((bm >= ratio * bj) & (bm <= ratio * bj + ratio) & (bm >= 1), 1.0, 0.0).astype(BF16)


def _proj_kernel(x_ref, g_ref, w_ref, cos_ref, sin_ref, pre_ref, cw_ref, *refs, tm, stride, carry, transposed,
                 n_carried):
    a_ref, kv_ref, win_ref, zb_ref, gm_ref, tail_ref, *rest = refs[n_carried:]
    if transposed:
        qt_ref, ks_ref, vst_ref, kw_ref, vwt_ref, gbt_ref, cmp_ref, u_scr = rest
    else:
        q_ref, gb_ref, u_scr = rest
    t = pl.program_id(1)
    x = x_ref[...]
    ms = jnp.mean(x * x, axis=-1, keepdims=True)
    h = (x * lax.rsqrt(ms + RMS_EPS) * g_ref[...]).astype(BF16)

    def seg(off, width):
        return jnp.dot(h, w_ref[:, off:off + width], preferred_element_type=F32)

    @pl.when(t == 0)
    def _():
        u_scr[0:carry, :] = pre_ref[0]

    u = seg(OFF_A + CONV_DIM, CONV_DIM) * seg(OFF_A + 2 * CONV_DIM, CONV_DIM)
    u_scr[carry:carry + tm, :] = u
    cw = cw_ref[...]
    conv = cw[0:1] * u_scr[carry - 2 * stride:carry - 2 * stride + tm, :]
    conv = conv + cw[1:2] * u_scr[carry - stride:carry - stride + tm, :]
    conv = conv + cw[2:3] * u
    ba = seg(OFF_A, CONV_DIM)
    za = seg(OFF_A + 3 * CONV_DIM, CONV_DIM)
    a_ref[...] = ((za * _sigmoid(za)) * (ba * conv)).astype(BF16)
    tail = u_scr[tm:tm + carry, :]
    tail_ref[0] = tail
    u_scr[0:carry, :] = tail

    lane = _iota((tm, LANE), 1)
    cos = cos_ref[...]
    sin = sin_ref[...]
    scale = HEAD_DIM ** -0.5 * (float(np.log2(np.e)) if transposed else 1.0)

    zq = seg(OFF_Q, ATT_DIM)
    for s in range(ATT_DIM // LANE):
        qs = _rope_slab(zq[:, s * LANE:(s + 1) * LANE], cos, sin, lane) * scale
        if transposed:
            qst = qs.T.astype(BF16)
            qt_ref[0, 2 * s] = qst[:HEAD_DIM]
            qt_ref[0, 2 * s + 1] = qst[HEAD_DIM:]
        else:
            q_ref[:, s * LANE:(s + 1) * LANE] = qs.astype(BF16)

    zkv = seg(OFF_KV, 4 * KV_DIM)
    zwin = seg(OFF_WIN, 2 * KV_DIM)
    kv_slabs, kv_t = [], []
    for s in range(4 * KV_SLABS):
        v = zkv[:, s * LANE:(s + 1) * LANE]
        if (s // KV_SLABS) % 2 == 0:
            v = _rope_slab(v, cos, sin, lane)
        if transposed:
            kv_t.append(v.T)
            kv_ref[s * LANE:(s + 1) * LANE, :] = kv_t[-1]
        else:
            kv_ref[:, s * LANE:(s + 1) * LANE] = v
        kv_slabs.append(v)
    win_slabs, win_t = [], []
    for s in range(2 * KV_SLABS):
        v = zwin[:, s * LANE:(s + 1) * LANE]
        if s // KV_SLABS == 0:
            v = _rope_slab(v, cos, sin, lane)
        if transposed:
            win_t.append(v.T)
            win_ref[s * LANE:(s + 1) * LANE, :] = win_t[-1]
        else:
            win_ref[:, s * LANE:(s + 1) * LANE] = v
        win_slabs.append(v)

    zb = seg(OFF_ZB, ATT_DIM)
    zb_ref[...] = zb * _sigmoid(zb)
    gb = _sigmoid(seg(OFF_GB, LANE))

    def merge_gates():
        gm_ref[...] = _sigmoid(seg(OFF_GM, 2 * D_MODEL))

    if not transposed:
        gb_ref[...] = gb
        merge_gates()
        return

    row = _iota((tm, LANE), 0)
    onehot = jnp.where(lane - HEAD_DIM == _div(t * tm + row, SLC_BLOCK), 1.0, 0.0)
    low = lane < HEAD_DIM
    ones_row = jnp.where(_iota((HEAD_DIM, tm), 0) == 0, 1.0, 0.0).astype(BF16)
    vs_t = [kv_t[3 * KV_SLABS + s].astype(BF16) for s in range(KV_SLABS)]
    vw_t = [win_t[KV_SLABS + s].astype(BF16) for s in range(KV_SLABS)]
    for s in range(2 * KV_SLABS):
        cmp_ref[s] = kv_slabs[s]
    for g in range(N_KV):
        s, hi = divmod(g, 2)

        def half(v):
            return v if hi == 0 else pltpu.roll(v, HEAD_DIM, 1)

        ks_ref[0, g] = jnp.where(low, half(kv_slabs[2 * KV_SLABS + s]), onehot).astype(BF16)
        kw_ref[0, g] = jnp.where(low, half(win_slabs[s]), 0.0).astype(BF16)
        vst_ref[0, g] = jnp.concatenate([vs_t[s][hi * HEAD_DIM:(hi + 1) * HEAD_DIM], ones_row], axis=0)
        vwt_ref[0, g] = jnp.concatenate([vw_t[s][hi * HEAD_DIM:(hi + 1) * HEAD_DIM], ones_row], axis=0)
    gbt_ref[0] = gb.T
    merge_gates()


def _window_pads_kernel(kw_ref, vwt_ref):
    flag = jnp.where(_iota((WINDOW, LANE), 1) == HEAD_DIM, 1.0, 0.0).astype(BF16)
    for g in range(N_KV):
        kw_ref[0, g] = flag
        vwt_ref[0, g] = jnp.zeros((LANE, WINDOW), BF16)


def _window_pads(n_seq, rows_per_seq):
    padded = rows_per_seq + WINDOW
    return pl.pallas_call(
        _window_pads_kernel,
        grid=(n_seq,),
        out_specs=[pl.BlockSpec((1, N_KV, WINDOW, LANE), lambda b: (b, 0, 0, 0)),
                   pl.BlockSpec((1, N_KV, LANE, WINDOW), lambda b: (b, 0, 0, 0))],
        out_shape=[jax.ShapeDtypeStruct((n_seq, N_KV, padded, LANE), BF16),
                   jax.ShapeDtypeStruct((n_seq, N_KV, LANE, padded), BF16)],
        compiler_params=_params(1),
        name="window_pads",
    )()


def _project(x2d, ln_g, w_perm, cos, sin, prefix, conv_w8, carried, window_pads, *, layer, n_seq, rows_per_seq, tm,
             stride, carry, transposed, pos_per_seq):
    nt = rows_per_seq // tm
    n_rows = n_seq * rows_per_seq
    depth = w_perm.shape[0]
    row_map = lambda b, t: (b * nt + t, 0)
    pos_map = (lambda b, t: (t, 0)) if pos_per_seq else row_map
    in_specs = [
        pl.BlockSpec((tm, D_MODEL), row_map),
        _layer_spec((1, D_MODEL), layer),
        _layer_spec((D_MODEL, N_PROJ), layer),
        pl.BlockSpec((tm, LANE), pos_map),
        pl.BlockSpec((tm, LANE), pos_map),
        pl.BlockSpec((1, carry, CONV_DIM), lambda b, t: (b, 0, 0)),
        _layer_spec((SUBLANE, CONV_DIM), layer),
    ]
    rows = lambda width, dtype: (jax.ShapeDtypeStruct((n_rows, width), dtype), pl.BlockSpec((tm, width), row_map))
    if transposed:
        keep = min(WINDOW, rows_per_seq)
        assert keep % tm == 0
        first_kept = (rows_per_seq - keep) // tm
        kv_out = (jax.ShapeDtypeStruct((depth, n_seq, 4 * KV_DIM, rows_per_seq), F32),
                  pl.BlockSpec((None, None, 4 * KV_DIM, tm), lambda b, t: (layer, b, 0, t)))
        win_out = (jax.ShapeDtypeStruct((depth, n_seq, 2 * KV_DIM, keep), F32),
                   pl.BlockSpec((None, None, 2 * KV_DIM, tm),
                                lambda b, t: (layer, b, 0, jnp.maximum(t - first_kept, 0))))
    else:
        kv_out, win_out = rows(4 * KV_DIM, F32), rows(2 * KV_DIM, F32)
    outs = [
        rows(CONV_DIM, BF16),
        kv_out,
        win_out,
        rows(ATT_DIM, F32),
        rows(2 * D_MODEL, F32),
        (jax.ShapeDtypeStruct((n_seq, carry, CONV_DIM), F32),
         pl.BlockSpec((1, carry, CONV_DIM), lambda b, t: (b, 0, 0))),
    ]
    if transposed:
        tok_sub = (jax.ShapeDtypeStruct((n_seq, N_KV, rows_per_seq, LANE), BF16),
                   pl.BlockSpec((1, N_KV, tm, LANE), lambda b, t: (b, 0, t, 0)))
        tok_lane = lambda n, rows_, dtype: (jax.ShapeDtypeStruct((n_seq, n, rows_, rows_per_seq), dtype),
                                            pl.BlockSpec((1, n, rows_, tm), lambda b, t: (b, 0, 0, t)))
        assert WINDOW % tm == 0
        padded = rows_per_seq + WINDOW
        win_sub = (jax.ShapeDtypeStruct((n_seq, N_KV, padded, LANE), BF16),
                   pl.BlockSpec((1, N_KV, tm, LANE), lambda b, t: (b, 0, t + WINDOW // tm, 0)))
        win_lane = (jax.ShapeDtypeStruct((n_seq, N_KV, LANE, padded), BF16),
                    pl.BlockSpec((1, N_KV, LANE, tm), lambda b, t: (b, 0, 0, t + WINDOW // tm)))
        outs += [tok_lane(N_HEADS, HEAD_DIM, BF16),
                 tok_sub, tok_lane(N_KV, LANE, BF16),
                 win_sub, win_lane,
                 (jax.ShapeDtypeStruct((n_seq, LANE, rows_per_seq), F32),
                  pl.BlockSpec((1, LANE, tm), lambda b, t: (b, 0, t))),
                 (jax.ShapeDtypeStruct((2 * KV_SLABS, n_rows, LANE), F32),
                  pl.BlockSpec((2 * KV_SLABS, tm, LANE), lambda b, t: (0, b * nt + t, 0)))]
    else:
        outs += [rows(ATT_DIM, BF16), rows(LANE, F32)]
    args = [x2d, ln_g, w_perm, cos, sin, prefix, conv_w8]
    aliases = {}
    if carried is not None:
        assert transposed
        aliases = {len(args): 1, len(args) + 1: 2}
        args += list(carried)
    if window_pads is not None:
        aliases.update({len(args): 9, len(args) + 1: 10})
        args += list(window_pads)
    in_specs += [pl.BlockSpec(memory_space=pl.ANY)] * len(aliases)
    return pl.pallas_call(
        functools.partial(_proj_kernel, tm=tm, stride=stride, carry=carry, transposed=transposed,
                          n_carried=len(aliases)),
        grid=(n_seq, nt),
        in_specs=in_specs,
        out_specs=[o[1] for o in outs],
        out_shape=[o[0] for o in outs],
        scratch_shapes=[pltpu.VMEM((carry + tm, CONV_DIM), F32)],
        input_output_aliases=aliases,
        compiler_params=_params(2),
        name="project_prompt" if transposed else "project_sample",
    )(*args)


def _merge_kernel(x_ref, a_ref, gm_ref, p_ref, woa_ref, wob_ref, wo_ref, wple_ref, wpg_ref, fg_ref, y_ref, *rest,
                  y_needs_gate, final):
    br_a = jnp.dot(a_ref[...], woa_ref[...], preferred_element_type=F32)
    if y_needs_gate:
        zb_ref, o_ref = rest
        y = (zb_ref[...] * y_ref[...]).astype(BF16)
    else:
        (o_ref,) = rest
        y = y_ref[...]
    br_b = jnp.dot(y, wob_ref[...], preferred_element_type=F32)
    gm = gm_ref[...]
    mix = gm[:, :D_MODEL] * br_a + gm[:, D_MODEL:] * br_b
    x1 = x_ref[...] + jnp.dot(mix.astype(BF16), wo_ref[...], preferred_element_type=F32)
    gate = _sigmoid(jnp.dot(x1.astype(BF16), wpg_ref[...], preferred_element_type=F32))
    x2 = x1 + gate * jnp.dot(p_ref[...].astype(BF16), wple_ref[...], preferred_element_type=F32)
    if final:
        ms = jnp.mean(x2 * x2, axis=-1, keepdims=True)
        x2 = x2 * lax.rsqrt(ms + RMS_EPS) * fg_ref[...]
    o_ref[...] = x2


def _merge(x2d, a_act, y, zb_act, gmix, p2d, w_out_a, w_out_b, w_o, w_ple, w_pg, final_g, *, layer, tm, final):
    n_rows = x2d.shape[0]
    row = lambda width: pl.BlockSpec((tm, width), lambda t: (t, 0))
    y_args = (y,) if zb_act is None else (y, zb_act)
    return pl.pallas_call(
        functools.partial(_merge_kernel, y_needs_gate=zb_act is not None, final=final),
        grid=(n_rows // tm,),
        in_specs=[row(D_MODEL), row(CONV_DIM), row(2 * D_MODEL),
                  pl.BlockSpec((None, tm, PLE_DIM), lambda t: (layer, t, 0)),
                  _layer_spec((CONV_DIM, D_MODEL), layer), _layer_spec((ATT_DIM, D_MODEL), layer),
                  _layer_spec((D_MODEL, D_MODEL), layer), _layer_spec((PLE_DIM, D_MODEL), layer),
                  _layer_spec((D_MODEL, D_MODEL), layer), _const_spec((1, D_MODEL))] + [row(ATT_DIM)] * len(y_args),
        out_specs=row(D_MODEL),
        out_shape=jax.ShapeDtypeStruct((n_rows, D_MODEL), F32),
        compiler_params=_params(1),
        name="merge_prompt" if zb_act is None else "merge_sample",
    )(x2d, a_act, gmix, p2d, w_out_a, w_out_b, w_o, w_ple, w_pg, final_g, *y_args)


def _compress_kernel(pt_ref, *refs, n_pages, prompt):
    pages = refs[:n_pages]
    pos_ref, w1_ref, w2_ref, ko_ref, vo_ref, rows_scr, x_scr, carry_scr = refs[n_pages:]
    st = pl.program_id(1)
    per_page = PAGE_SIZE // CMP_STRIDE
    n = n_pages * per_page
    low = _iota((n, LANE), 1) < HEAD_DIM

    @pl.when(st == 0)
    def _():
        carry_scr[...] = jnp.zeros_like(carry_scr)

    row0 = _iota((n, CMP_HID), 0) == 0
    def relayout(slot):
        for j in range(n_pages):
            for v in range(KV_SLABS):
                if prompt:
                    piece = pages[j][slot * KV_SLABS + v]
                else:
                    piece = pages[j][slot, v * LANE:(v + 1) * LANE, :].T
                rows_scr[slot, v, j * PAGE_SIZE:(j + 1) * PAGE_SIZE, :] = piece
        for pair in range(CMP_STRIDE // 2):
            for v in range(KV_SLABS):
                a = rows_scr[slot, v, pl.ds(2 * pair, n, stride=CMP_STRIDE), :]
                b = rows_scr[slot, v, pl.ds(2 * pair + 1, n, stride=CMP_STRIDE), :]
                x_scr[slot, 2 * v * n:(2 * v + 1) * n, pair * LANE:(pair + 1) * LANE] = jnp.where(
                    low, a, pltpu.roll(b, HEAD_DIM, 1))
                x_scr[slot, (2 * v + 1) * n:(2 * v + 2) * n, pair * LANE:(pair + 1) * LANE] = jnp.where(
                    low, pltpu.roll(a, HEAD_DIM, 1), b)
    for slot, out_ref in enumerate((ko_ref, vo_ref)):
        relayout(slot)
        pos = pos_ref[slot]
        f = s = None
        for k0 in range(0, x_scr.shape[2], CMP_DEPTH):
            ks = slice(k0, k0 + CMP_DEPTH)
            xs = x_scr[slot, :, ks]
            f_part = jnp.dot((xs + pos[0:1, ks]).astype(BF16), w1_ref[slot, 0, ks, :], preferred_element_type=F32)
            s_part = jnp.dot((xs + pos[1:2, ks]).astype(BF16), w1_ref[slot, 1, ks, :], preferred_element_type=F32)
            f = f_part if f is None else f + f_part
            s = s_part if s is None else s + s_part
        acc = None
        for g in range(N_KV):
            fg = f[g * n:(g + 1) * n]
            prev = carry_scr[slot, g, SUBLANE - 1:SUBLANE, :]
            shifted = jnp.where(row0, prev, pltpu.roll(fg, 1, 0))
            carry_scr[slot, g] = fg[n - SUBLANE:n]
            pre = shifted + s[g * n:(g + 1) * n]
            hid = (pre * _sigmoid(pre)).astype(BF16)
            if prompt and slot == 0:
                out_ref[0, g] = jnp.dot(hid, w2_ref[0, 0, :, :LANE], preferred_element_type=F32).astype(BF16)
            elif prompt:
                out_ref[0, g] = lax.dot_general(w2_ref[1, 0, :LANE, :], hid, NT_DIMS,
                                                preferred_element_type=F32).astype(BF16)
            else:
                part = jnp.dot(hid, w2_ref[slot, g], preferred_element_type=F32)
                acc = part if acc is None else acc + part
        if not prompt:
            out_ref[0] = acc.astype(BF16)


def _compress(pages_arr, page_table, page_index, pos2, w1, w2, *, layer, n_pages, prompt):
    n_seq, pages_per_seq = page_table.shape
    n_steps = pages_per_seq // n_pages
    n = n_pages * (PAGE_SIZE // CMP_STRIDE)
    n_chunk = pages_per_seq * (PAGE_SIZE // CMP_STRIDE)
    flat = CMP_STRIDE * HEAD_DIM

    def page_spec(j):
        if prompt:
            return pl.BlockSpec((2 * KV_SLABS, PAGE_SIZE, LANE),
                                lambda b, s, pt: (0, page_index(pt[b, s * n_pages + j]), 0))
        return pl.BlockSpec((2, KV_DIM, PAGE_SIZE), lambda b, s, pt: (page_index(pt[b, s * n_pages + j]), 0, 0))

    if prompt:
        out_sds = [jax.ShapeDtypeStruct((n_seq, N_KV, n_chunk, LANE), BF16),
                   jax.ShapeDtypeStruct((n_seq, N_KV, LANE, n_chunk), BF16)]
        out_specs = [pl.BlockSpec((1, N_KV, n, LANE), lambda b, s, pt: (b, 0, s, 0)),
                     pl.BlockSpec((1, N_KV, LANE, n), lambda b, s, pt: (b, 0, 0, s))]
    else:
        out_sds = [jax.ShapeDtypeStruct((n_seq, n_chunk, KV_DIM), BF16)] * 2
        out_specs = [pl.BlockSpec((1, n, KV_DIM), lambda b, s, pt: (b, s, 0))] * 2
    grid_spec = pltpu.PrefetchScalarGridSpec(
        num_scalar_prefetch=1,
        grid=(n_seq, n_steps),
        in_specs=[page_spec(j) for j in range(n_pages)] + [_layer_spec(a.shape[1:], layer) for a in (pos2, w1, w2)],
        out_specs=out_specs,
        scratch_shapes=[pltpu.VMEM((2, KV_SLABS, n_pages * PAGE_SIZE, LANE), F32),
                        pltpu.VMEM((2, N_KV * n, flat), F32),
                        pltpu.VMEM((2, N_KV, SUBLANE, CMP_HID), F32)],
    )
    return pl.pallas_call(
        functools.partial(_compress_kernel, n_pages=n_pages, prompt=prompt),
        grid_spec=grid_spec,
        out_shape=out_sds,
        compiler_params=_params(2),
        name="compress_prompt" if prompt else "compress_sample",
    )(page_table, *([pages_arr] * n_pages), pos2, w1, w2)


def _select_bias(score_ref, cnt_ref, n_real, n_live, n_sel):
    score = score_ref[...]
    n_rows = -(-n_real // SUBLANE) * SUBLANE
    sub = _iota((SUBLANE, score.shape[1]), 0)
    cnt_ref[...] = jnp.zeros(score.shape, F32)

    def group(i0):
        cnt = cnt_ref[0:n_rows, :]
        for i in range(i0, min(i0 + SUBLANE, n_real)):
            si = score_ref[i:i + 1, :]
            pieces = []
            for r0 in range(0, n_rows, SUBLANE):
                sv = score[r0:r0 + SUBLANE]
                if r0 > i:
                    pieces.append(jnp.where(si >= sv, 1.0, 0.0))
                elif r0 + SUBLANE - 1 < i:
                    pieces.append(jnp.where(si > sv, 1.0, 0.0))
                else:
                    pieces.append(jnp.where(sub > i - r0, jnp.where(si >= sv, 1.0, 0.0), jnp.where(si > sv, 1.0, 0.0)))
            cnt = cnt + jnp.concatenate(pieces, axis=0)
        cnt_ref[0:n_rows, :] = cnt

    for i0 in range(0, n_real, SUBLANE):
        if n_live is None:
            group(i0)
        else:
            pl.when(i0 < n_live)(functools.partial(group, i0))
    return jnp.where((cnt_ref[...] < n_sel) & (score >= 0.0), 0.0, NEG_INF)


def _attn_prompt_kernel(qt_ref, gbt_ref, zb_ref, tri_ref, kc_ref, vct_ref, ks_ref, vst_ref, kw_ref, vwt_ref, o_ref,
                        score_scr, cnt_scr, m_scr, acc_scr, mx_scr, s_even_scr, s_odd_scr, *, tq, n_cmp, n_blk):
    qi = pl.program_id(2)
    s0 = qi * tq
    rows = GROUP * tq
    cw = min(rows, CHAIN_COLS)
    col_groups = [slice(c, c + cw) for c in range(0, rows, cw)]
    qpos = s0 + _mod(_iota((1, rows), 1), tq)

    def stack_q(lower):
        return jnp.concatenate([jnp.concatenate([qt_ref[0, r], lower], axis=0) for r in range(GROUP)], axis=1)

    q0 = stack_q(jnp.where(_iota((LANE - HEAD_DIM, tq), 0) == 0, NEG_INF, 0.0).astype(BF16))

    s_c = jnp.dot(kc_ref[0, 0], q0, preferred_element_type=F32)

    def window_scores(cs, first, n_keys):
        keys = kw_ref[0, 0, pl.ds(pl.multiple_of(s0 + first, tq), n_keys), :]
        return jnp.dot(keys, q0[:, cs], preferred_element_type=F32)

    w_parts = []
    for cs in col_groups:
        parts = [window_scores(cs, 0, tq) + tri_ref[0, :, cs]]
        if WINDOW > tq:
            parts.append(window_scores(cs, tq, WINDOW - tq))
        parts.append(window_scores(cs, WINDOW, tq) + tri_ref[1, :, cs])
        w_parts.append(parts)

    m_idx = _iota((n_cmp, 1), 0)
    mask_c = (m_idx >= 1) & (CMP_STRIDE * m_idx + CMP_STRIDE - 1 <= qpos)
    s_c = jnp.where(mask_c, s_c, NEG_INF)
    e_c = jnp.where(mask_c, jnp.exp2(s_c - jnp.max(s_c, axis=0, keepdims=True)), 0.0)
    p_c = e_c * (1.0 / jnp.maximum(jnp.sum(e_c, axis=0, keepdims=True), 1e-30))
    o_c = jnp.dot(vct_ref[0, 0], p_c.astype(BF16), preferred_element_type=F32)

    p_sum = p_c[:, 0:tq]
    for r in range(1, GROUP):
        p_sum = p_sum + p_c[:, r * tq:(r + 1) * tq]
    p_hi = p_sum.astype(BF16)
    p_lo = (p_sum - p_hi.astype(F32)).astype(BF16)
    ov_t = _overlap_t(MAX_BLOCKS, n_cmp)
    imp = jnp.dot(ov_t, p_hi, preferred_element_type=F32) + jnp.dot(ov_t, p_lo, preferred_element_type=F32)

    w_out = []
    for cs, parts in zip(col_groups, w_parts):
        m_w = functools.reduce(jnp.maximum, [jnp.max(s, axis=0, keepdims=True) for s in parts])
        p_w = jnp.concatenate([jnp.exp2(s - m_w).astype(BF16) for s in parts], axis=0)
        acc_w = jnp.dot(vwt_ref[0, 0, :, pl.ds(pl.multiple_of(s0, tq), WINDOW + tq)], p_w, preferred_element_type=F32)
        w_out.append(acc_w[:HEAD_DIM] / jnp.maximum(acc_w[HEAD_DIM:HEAD_DIM + 1], 1e-30))
    o_w = jnp.concatenate(w_out, axis=1)

    blk = _iota((MAX_BLOCKS, tq), 0)
    cur = _div(s0 + _iota((MAX_BLOCKS, tq), 1), SLC_BLOCK)
    forced = (blk == 0) | (blk == cur) | (blk == cur - 1)
    score = jnp.where(forced, FORCE_SCORE, imp)
    score_scr[...] = jnp.where(blk <= cur, score, -1.0)
    n_live = jnp.minimum(n_blk, (s0 + tq) // SLC_BLOCK)
    bias_t = _select_bias(score_scr, cnt_scr, n_blk, n_live, min(TOP_N, n_blk))
    qx = stack_q(bias_t.astype(BF16))

    m_scr[...] = jnp.full(m_scr.shape, NEG_INF, F32)
    acc_scr[...] = jnp.zeros(acc_scr.shape, F32)
    s_bufs = (s_even_scr, s_odd_scr)

    def scores(k0, slot, causal):
        keys = ks_ref[0, 0, pl.ds(pl.multiple_of(k0, tq), tq), :]
        prods = [jnp.dot(keys, qx[:, cs], preferred_element_type=F32) for cs in col_groups]
        for cs, s in zip(col_groups, prods):
            if causal:
                s = s + tri_ref[1, :, cs]
            s_bufs[slot][:, cs] = s
            mx_scr[slot, 0:1, cs] = jnp.max(s, axis=0, keepdims=True)

    def accumulate(k0, slot):
        vals_t = vst_ref[0, 0, :, pl.ds(pl.multiple_of(k0, tq), tq)]
        for cs in col_groups:
            m_old = m_scr[0:1, cs]
            m_new = jnp.maximum(m_old, mx_scr[slot, 0:1, cs])
            p = jnp.exp2(s_bufs[slot][:, cs] - m_new).astype(BF16)
            acc_scr[:, cs] = jnp.exp2(m_old - m_new) * acc_scr[:, cs] + jnp.dot(vals_t, p, preferred_element_type=F32)
            m_scr[0:1, cs] = m_new

    n_pairs = qi // 2
    scores(s0, 0, True)

    def pair(j, _):
        scores(2 * j * tq, 1, False)
        accumulate(jnp.where(j == 0, s0, (2 * j - 1) * tq), 0)
        scores((2 * j + 1) * tq, 0, False)
        accumulate(2 * j * tq, 1)
        return 0

    lax.fori_loop(0, n_pairs, pair, 0)
    pending = jnp.where(n_pairs == 0, s0, (2 * n_pairs - 1) * tq)

    @pl.when(qi - 2 * n_pairs == 1)
    def _():
        scores(s0 - tq, 1, False)
        accumulate(pending, 0)
        accumulate(s0 - tq, 1)

    @pl.when(qi - 2 * n_pairs == 0)
    def _():
        accumulate(pending, 0)

    o_s = acc_scr[:HEAD_DIM, :] / jnp.maximum(acc_scr[HEAD_DIM:HEAD_DIM + 1, :], 1e-30)

    def gate(r, k):
        return gbt_ref[0, pl.ds(3 * (GROUP * pl.program_id(1) + r) + k, 1), :]

    zb = zb_ref[0]
    heads = []
    for r in range(GROUP):
        rs = slice(r * tq, (r + 1) * tq)
        heads.append(gate(r, 0) * o_c[:HEAD_DIM, rs] + gate(r, 1) * o_s[:, rs] + gate(r, 2) * o_w[:, rs])
    for pair in range(GROUP // 2):
        slab = jnp.concatenate([heads[2 * pair], heads[2 * pair + 1]], axis=0).T
        o_ref[0, :, pair * LANE:(pair + 1) * LANE] = (zb[:, pair * LANE:(pair + 1) * LANE] * slab).astype(BF16)


def _attn_prompt(q_t, gb_t, zb_act, kc_ext, vc_t, ks_ext, vs_t, kw_pad, vw_pad, *, tq):
    n_b, seq_len = zb_act.shape[0], zb_act.shape[1]
    n_cmp = kc_ext.shape[2]
    n_blk = seq_len // SLC_BLOCK
    head_w = GROUP * HEAD_DIM
    rows = GROUP * tq
    assert WINDOW % tq == 0 and rows % min(rows, CHAIN_COLS) == 0
    off = _iota((tq, rows), 0) - _mod(_iota((tq, rows), 1), tq)
    tri = jnp.stack([jnp.where(off > 0, 0.0, NEG_INF), jnp.where(off <= 0, 0.0, NEG_INF)]).astype(F32)
    tok_sub = lambda n: pl.BlockSpec((1, 1, n, LANE), lambda b, g, i: (b, g, 0, 0))
    tok_lane = lambda n: pl.BlockSpec((1, 1, LANE, n), lambda b, g, i: (b, g, 0, 0))
    return pl.pallas_call(
        functools.partial(_attn_prompt_kernel, tq=tq, n_cmp=n_cmp, n_blk=n_blk),
        grid=(n_b, N_KV, seq_len // tq),
        in_specs=[pl.BlockSpec((1, GROUP, HEAD_DIM, tq), lambda b, g, i: (b, g, 0, i)),
                  pl.BlockSpec((1, LANE, tq), lambda b, g, i: (b, 0, i)),
                  pl.BlockSpec((1, tq, head_w), lambda b, g, i: (b, i, g)),
                  _const_spec((2, tq, rows)),
                  tok_sub(n_cmp), tok_lane(n_cmp), tok_sub(seq_len), tok_lane(seq_len),
                  tok_sub(seq_len + WINDOW), tok_lane(seq_len + WINDOW)],
        out_specs=pl.BlockSpec((1, tq, head_w), lambda b, g, i: (b, i, g)),
        out_shape=jax.ShapeDtypeStruct((n_b, seq_len, ATT_DIM), BF16),
        scratch_shapes=[pltpu.VMEM((MAX_BLOCKS, tq), F32), pltpu.VMEM((MAX_BLOCKS, tq), F32),
                        pltpu.VMEM((SUBLANE, rows), F32),
                        pltpu.VMEM((LANE, rows), F32), pltpu.VMEM((2, SUBLANE, rows), F32),
                        pltpu.VMEM((tq, rows), F32), pltpu.VMEM((tq, rows), F32)],
        compiler_params=_params(3),
        name="attention_prompt",
    )(q_t, gb_t, zb_act, tri, kc_ext, vc_t, ks_ext, vs_t, kw_pad, vw_pad)


def _attn_sample_kernel(pt_ref, *refs, n_pages, past_len, dec_len, n_blk_pad):
    pages = refs[:n_pages]
    (qbd_ref, qbdt_ref, kc_ref, vc_ref, newkv_ref, win_ref, newwin_ref, gate_ref, o_ref,
     score_scr, cnt_scr, bias_scr, m_scr, l_scr, acc_scr, oc_scr) = refs[n_pages:]
    st = pl.program_id(1)
    n_steps = pl.num_programs(1)
    cols = LANE
    tslots = cols // N_HEADS
    n_cmp = kc_ref.shape[1]
    n_past_blk = past_len // SLC_BLOCK
    per_page = PAGE_SIZE // SLC_BLOCK
    qbdt = qbdt_ref[0]
    tok_c = _mod(_iota((cols, 1), 0), tslots)

    @pl.when(st == 0)
    def _():
        qpos = past_len + _mod(_iota((1, cols), 1), tslots)
        s_c = jnp.dot(kc_ref[0], qbd_ref[0], preferred_element_type=F32)
        m_idx = _iota((n_cmp, 1), 0)
        mask_c = (m_idx >= 1) & (CMP_STRIDE * m_idx + CMP_STRIDE - 1 <= qpos)
        s_c = jnp.where(mask_c, s_c, NEG_INF)
        e_c = jnp.where(mask_c, jnp.exp(s_c - jnp.max(s_c, axis=0, keepdims=True)), 0.0)
        p_c = e_c / jnp.maximum(jnp.sum(e_c, axis=0, keepdims=True), 1e-30)
        p_hi = p_c.astype(BF16)
        p_lo = (p_c - p_hi.astype(F32)).astype(BF16)
        oc_scr[...] = jnp.dot(p_c.T.astype(BF16), vc_ref[0], preferred_element_type=F32)
        ov_t = _overlap_t(n_blk_pad, n_cmp)
        imp1 = jnp.dot(ov_t, p_hi, preferred_element_type=F32) + jnp.dot(ov_t, p_lo, preferred_element_type=F32)
        ca = _iota((cols, cols), 0)
        cb = _iota((cols, cols), 1)
        per_group = GROUP * tslots
        same = jnp.where((_div(ca, per_group) == _div(cb, per_group)) & (_mod(ca, tslots) == _mod(cb, tslots)),
                         1.0, 0.0).astype(BF16)
        i_hi = imp1.astype(BF16)
        i_mid = (imp1 - i_hi.astype(F32)).astype(BF16)
        i_lo = (imp1 - i_hi.astype(F32) - i_mid.astype(F32)).astype(BF16)
        imp = (jnp.dot(i_hi, same, preferred_element_type=F32) + jnp.dot(i_mid, same, preferred_element_type=F32)
               + jnp.dot(i_lo, same, preferred_element_type=F32))
        blk = _iota((n_blk_pad, cols), 0)
        cur = _div(qpos, SLC_BLOCK)
        forced = (blk == 0) | (blk == cur) | (blk == cur - 1)
        score = jnp.where(forced, FORCE_SCORE, imp)
        score_scr[...] = jnp.where(blk <= cur, score, -1.0)
        n_blk = n_past_blk + 1
        bias_scr[...] = _select_bias(score_scr, cnt_scr, n_blk, None, min(TOP_N, n_blk)).T
        m_scr[...] = jnp.full(m_scr.shape, NEG_INF, F32)
        l_scr[...] = jnp.zeros(l_scr.shape, F32)
        acc_scr[...] = jnp.zeros(acc_scr.shape, F32)

    def update(s, pv_fn):
        m_old = m_scr[...]
        m_new = jnp.maximum(m_old, jnp.max(s, axis=1, keepdims=True))
        alpha = jnp.exp(m_old - m_new)
        p = jnp.exp(s - m_new)
        l_scr[...] = alpha * l_scr[...] + jnp.sum(p, axis=1, keepdims=True)
        acc_scr[...] = alpha * acc_scr[...] + pv_fn(p.astype(BF16))
        m_scr[...] = m_new

    bias_b = bias_scr[...].astype(BF16)
    blk0 = st * (n_pages * per_page)
    key_blk = blk0 + _div(_iota((n_blk_pad, n_pages * PAGE_SIZE), 1), SLC_BLOCK)
    expand = jnp.where(_iota((n_blk_pad, n_pages * PAGE_SIZE), 0) == key_blk, 1.0, 0.0).astype(BF16)
    keys_t = jnp.concatenate([pages[j][0].astype(BF16) for j in range(n_pages)], axis=1)
    vals_t = jnp.concatenate([pages[j][1].astype(BF16) for j in range(n_pages)], axis=1)
    s = (jnp.dot(qbdt, keys_t, preferred_element_type=F32) + jnp.dot(bias_b, expand, preferred_element_type=F32))
    update(s, lambda p: lax.dot_general(p, vals_t, NT_DIMS, preferred_element_type=F32))

    @pl.when(st == n_steps - 1)
    def _():
        def pad_rows(a):
            return jnp.concatenate([a, jnp.zeros((LANE - a.shape[0], a.shape[1]), a.dtype)], axis=0)

        key_i = _iota((1, LANE), 1)
        new = pad_rows(newkv_ref[0])
        s_n = lax.dot_general(qbdt, new[:, 2 * KV_DIM:3 * KV_DIM].astype(BF16), NT_DIMS, preferred_element_type=F32)
        s_n = s_n + bias_scr[:, n_past_blk:n_past_blk + 1]
        s_n = jnp.where((key_i <= tok_c) & (key_i < dec_len), s_n, NEG_INF)
        v_n = new[:, 3 * KV_DIM:].astype(BF16)
        update(s_n, lambda p: jnp.dot(p, v_n, preferred_element_type=F32))
        o_s = acc_scr[...] / jnp.maximum(l_scr[...], 1e-30)

        w_buf = win_ref.shape[2]
        neww = pad_rows(newwin_ref[0])
        s_w = jnp.concatenate(
            [jnp.dot(qbdt, win_ref[0].astype(BF16), preferred_element_type=F32),
             lax.dot_general(qbdt, neww[:, :KV_DIM].astype(BF16), NT_DIMS, preferred_element_type=F32)], axis=1)
        idx = _iota((1, w_buf + LANE), 1)
        kpos = past_len - w_buf + idx
        dist = past_len + tok_c - kpos
        mask_w = (dist >= 0) & (dist < WINDOW) & (idx < w_buf + dec_len) & (kpos >= 0)
        s_w = jnp.where(mask_w, s_w, NEG_INF)
        e_w = jnp.where(mask_w, jnp.exp(s_w - jnp.max(s_w, axis=1, keepdims=True)), 0.0)
        p_w = (e_w / jnp.maximum(jnp.sum(e_w, axis=1, keepdims=True), 1e-30)).astype(BF16)
        o_w = (lax.dot_general(p_w[:, :w_buf], win_ref[1].astype(BF16), NT_DIMS, preferred_element_type=F32)
               + jnp.dot(p_w[:, w_buf:], neww[:, KV_DIM:].astype(BF16), preferred_element_type=F32))
        gate = gate_ref[0]
        o_ref[0] = gate[:, 0:1] * oc_scr[...] + gate[:, 1:2] * o_s + gate[:, 2:3] * o_w


def _attn_sample(cache_t, page_table, page_index, win_t, win_index, qbd, qbdt, kc_flat, vc_flat, newkv8, newwin8,
                 gates, *, n_pages, past_len, dec_len):
    n_seq, pages_per_seq = page_table.shape
    n_steps = pages_per_seq // n_pages
    n_cmp = kc_flat.shape[1]
    n_blk_pad = -(-(past_len // SLC_BLOCK + 1) // LANE) * LANE
    w_buf = win_t.shape[2]

    def page_spec(j):
        return pl.BlockSpec((2, KV_DIM, PAGE_SIZE), lambda b, s, pt: (page_index(pt[b, s * n_pages + j]), 0, 0))

    per_seq = lambda shape: pl.BlockSpec((1,) + shape, lambda b, s, pt: (b,) + (0,) * len(shape))
    grid_spec = pltpu.PrefetchScalarGridSpec(
        num_scalar_prefetch=1,
        grid=(n_seq, n_steps),
        in_specs=[page_spec(j) for j in range(n_pages)] + [
            per_seq((KV_DIM, LANE)), per_seq((LANE, KV_DIM)), per_seq((n_cmp, KV_DIM)), per_seq((n_cmp, KV_DIM)),
            per_seq((SUBLANE, 4 * KV_DIM)),
            pl.BlockSpec((2, KV_DIM, w_buf), lambda b, s, pt: (win_index(b), 0, 0)),
            per_seq((SUBLANE, 2 * KV_DIM)), per_seq((LANE, LANE))],
        out_specs=per_seq((LANE, KV_DIM)),
        scratch_shapes=[pltpu.VMEM((n_blk_pad, LANE), F32), pltpu.VMEM((n_blk_pad, LANE), F32),
                        pltpu.VMEM((LANE, n_blk_pad), F32),
                        pltpu.VMEM((LANE, 1), F32), pltpu.VMEM((LANE, 1), F32),
                        pltpu.VMEM((LANE, KV_DIM), F32), pltpu.VMEM((LANE, KV_DIM), F32)],
    )
    return pl.pallas_call(
        functools.partial(_attn_sample_kernel, n_pages=n_pages, past_len=past_len, dec_len=dec_len,
                          n_blk_pad=n_blk_pad),
        grid_spec=grid_spec,
        out_shape=jax.ShapeDtypeStruct((n_seq, LANE, KV_DIM), F32),
        compiler_params=_params(2),
        name="attention_sample",
    )(page_table, *([cache_t] * n_pages), qbd, qbdt, kc_flat, vc_flat, newkv8, win_t, newwin8, gates)


def _permute_w_in(w):
    lead = w.shape[:-1]
    c = np.cumsum([0, 4 * CONV_DIM, ATT_DIM, 4 * KV_DIM, 2 * KV_DIM, 3 * N_HEADS, ATT_DIM, 2 * D_MODEL])
    wa, wq, wkv, wwin, wgb, wzb, wgm = (w[..., c[i]:c[i + 1]] for i in range(7))
    wgb = jnp.pad(wgb, ((0, 0),) * len(lead) + ((0, LANE - 3 * N_HEADS),))
    return jnp.concatenate([wa, wq, wkv, wwin, wzb, wgm, wgb], axis=-1).astype(BF16)


def _rope_tables(pos):
    half = HEAD_DIM // 2
    inv = jnp.power(ROPE_THETA, -jnp.arange(half, dtype=F32) / half)
    ang = pos.astype(F32)[:, None] * inv[None, :]
    cos, sin = jnp.cos(ang), jnp.sin(ang)
    reps = LANE // HEAD_DIM
    return jnp.tile(jnp.concatenate([cos, cos], axis=1), (1, reps)), jnp.tile(jnp.concatenate([-sin, sin], axis=1), (1, reps))


def _compress_weights(pos_k, w1_k, w2_k, pos_v, w1_v, w2_v, grouped):
    depth = pos_k.shape[0]
    flat = CMP_STRIDE * HEAD_DIM
    pos2 = jnp.stack([jnp.pad(p.reshape(depth, 2, flat), ((0, 0), (0, SUBLANE - 2), (0, 0))) for p in (pos_k, pos_v)],
                     axis=1)
    w1 = jnp.stack([w.reshape(depth, 2, flat, CMP_HID) for w in (w1_k, w1_v)], axis=1).astype(BF16)
    if grouped:
        w2 = jnp.stack([jnp.pad(w2_k, ((0, 0), (0, 0), (0, CMP_HID - HEAD_DIM)))[:, None],
                        jnp.pad(w2_v.transpose(0, 2, 1), ((0, 0), (0, CMP_HID - HEAD_DIM), (0, 0)))[:, None]], axis=1)
    else:
        w2 = jnp.stack([jnp.stack([jnp.pad(w, ((0, 0), (0, 0), (g * HEAD_DIM, KV_DIM - (g + 1) * HEAD_DIM)))
                                   for g in range(N_KV)], axis=1) for w in (w2_k, w2_v)], axis=1)
    return pos2, w1, w2.astype(BF16)


def kernel(x_prompt, x_sample, cache_kv, state_win, state_conv, page_table, p_prompt, p_sample, ln_g, w_in, conv_w,
           cmp_pos_k, cmp_w1_k, cmp_w2_k, cmp_pos_v, cmp_w1_v, cmp_w2_v, w_out_a, w_out_b, w_o, w_ple, w_ple_gate,
           final_g):
    depth = ln_g.shape[0]
    n_b, seq_len, _ = x_prompt.shape
    n_dec, dec_len, _ = x_sample.shape
    pages_per_seq = page_table.shape[1]
    past_len = pages_per_seq * PAGE_SIZE
    n_pool = cache_kv.shape[1]
    w_buf = state_win.shape[2]

    tm_p = min(256, seq_len)
    tq = min(256, seq_len)
    tm_merge = min(512, seq_len)
    pages_p = min(16, seq_len // PAGE_SIZE)
    pages_s = min(16, pages_per_seq)
    rows_s = n_dec * dec_len
    carry_p = SUBLANE
    carry_s = (CONV_W - 1) * n_dec
    assert seq_len % PAGE_SIZE == 0 and seq_len // SLC_BLOCK <= MAX_BLOCKS and seq_len >= WINDOW + tq
    assert CONV_W - 1 <= dec_len <= SUBLANE and rows_s % SUBLANE == 0
    assert w_buf == min(WINDOW, past_len) and N_HEADS * SUBLANE == LANE

    cos_p, sin_p = _rope_tables(jnp.arange(seq_len, dtype=jnp.int32))
    pos_s = past_len + jnp.repeat(jnp.arange(dec_len, dtype=jnp.int32), n_dec)
    cos_s, sin_s = _rope_tables(pos_s)
    prompt_pages = jnp.arange(n_b * (seq_len // PAGE_SIZE), dtype=jnp.int32).reshape(n_b, seq_len // PAGE_SIZE)
    eye_g = jnp.eye(N_KV, dtype=F32)
    tpad = SUBLANE - dec_len
    cache_t = cache_kv.transpose(0, 1, 3, 4, 5, 2).reshape(depth * n_pool * 4, KV_DIM, PAGE_SIZE)
    win_t = state_win.transpose(0, 1, 3, 4, 5, 2).reshape(depth * n_dec * 2, KV_DIM, w_buf)

    xp = x_prompt.reshape(n_b * seq_len, D_MODEL)
    xs = x_sample.transpose(1, 0, 2).reshape(rows_s, D_MODEL)
    kv_s, win_new_s, conv_p, conv_s = [], [], [], []
    carried = None
    w_perm = _permute_w_in(w_in)
    g_rows = ln_g.reshape(depth, 1, D_MODEL)
    cw8 = jnp.pad(conv_w, ((0, 0), (0, SUBLANE - CONV_W), (0, 0)))
    merge_w = (w_out_a.astype(BF16), w_out_b.astype(BF16), w_o.astype(BF16), w_ple.astype(BF16),
               w_ple_gate.astype(BF16), final_g.reshape(1, D_MODEL))
    cmp_args = (cmp_pos_k, cmp_w1_k, cmp_w2_k, cmp_pos_v, cmp_w1_v, cmp_w2_v)
    cmp_w_prompt = _compress_weights(*cmp_args, True)
    cmp_w_sample = _compress_weights(*cmp_args, False)
    zero_prefix = jnp.zeros((n_b, carry_p, CONV_DIM), F32)
    p_sample_rows = p_sample.transpose(0, 2, 1, 3).reshape(depth, rows_s, PLE_DIM)
    for i in range(depth):
        last = i == depth - 1

        (a_act, newkv_all, win_all, zb_act, gmix, tail, q_t, ks_ext, vs_t, kw_pad, vw_pad, gb_t, cmp_rows) = _project(
            xp, g_rows, w_perm, cos_p, sin_p, zero_prefix, cw8, carried, _window_pads(n_b, seq_len), layer=i,
            n_seq=n_b, rows_per_seq=seq_len, tm=tm_p, stride=1, carry=carry_p, transposed=True, pos_per_seq=True)
        carried = (newkv_all, win_all)
        kc_ext, vc_t = _compress(cmp_rows, prompt_pages, lambda p: p, *cmp_w_prompt, layer=i, n_pages=pages_p,
                                 prompt=True)
        y_act = _attn_prompt(q_t, gb_t, zb_act.reshape(n_b, seq_len, ATT_DIM), kc_ext, vc_t, ks_ext, vs_t, kw_pad,
                             vw_pad, tq=tq)
        xp = _merge(xp, a_act, y_act.reshape(n_b * seq_len, ATT_DIM), None, gmix,
                    p_prompt.reshape(depth, n_b * seq_len, PLE_DIM), *merge_w, layer=i, tm=tm_merge, final=last)
        conv_p.append(tail[:, carry_p - (CONV_W - 1):])

        prefix_s = state_conv[i].transpose(1, 0, 2).reshape(1, carry_s, CONV_DIM)
        (a_act, newkv, win, zb_act, gmix, tail, qp, gb) = _project(
            xs, g_rows, w_perm, cos_s, sin_s, prefix_s, cw8, None, None, layer=i,
            n_seq=1, rows_per_seq=rows_s, tm=rows_s, stride=n_dec, carry=carry_s, transposed=False, pos_per_seq=False)
        base = i * n_pool
        kc_flat, vc_flat = _compress(cache_t, page_table, lambda p: (base + p) * 2, *cmp_w_sample, layer=i,
                                     n_pages=pages_s, prompt=False)
        q5 = jnp.pad(qp.reshape(dec_len, n_dec, N_KV, GROUP, HEAD_DIM).astype(F32),
                     ((0, tpad), (0, 0), (0, 0), (0, 0), (0, 0)))
        qbd = q5.transpose(1, 4, 2, 3, 0)[:, None] * eye_g[None, :, None, :, None, None]
        qbd = qbd.reshape(n_dec, KV_DIM, LANE).astype(BF16)
        qbdt = q5.transpose(1, 2, 3, 0, 4)[:, :, :, :, None] * eye_g[None, :, None, None, :, None]
        qbdt = qbdt.reshape(n_dec, LANE, KV_DIM).astype(BF16)
        gates = gb[:, :3 * N_HEADS].reshape(dec_len, n_dec, N_KV, GROUP, 3)
        gates = jnp.pad(gates.transpose(1, 2, 3, 0, 4), ((0, 0), (0, 0), (0, 0), (0, tpad), (0, LANE - 3)))
        gates = gates.reshape(n_dec, LANE, LANE)
        newkv_b = newkv.reshape(dec_len, n_dec, 4 * KV_DIM).transpose(1, 0, 2)
        newwin_b = win.reshape(dec_len, n_dec, 2 * KV_DIM).transpose(1, 0, 2)
        pad_rows = lambda a: jnp.pad(a, ((0, 0), (0, tpad), (0, 0)))
        o_all = _attn_sample(cache_t, page_table, lambda p: (base + p) * 2 + 1, win_t, lambda b: i * n_dec + b, qbd,
                             qbdt, kc_flat, vc_flat, pad_rows(newkv_b), pad_rows(newwin_b), gates, n_pages=pages_s,
                             past_len=past_len, dec_len=dec_len)
        o6 = o_all.reshape(n_dec, N_KV, GROUP, SUBLANE, N_KV, HEAD_DIM)
        yb = jnp.stack([o6[:, g, :, :, g] for g in range(N_KV)], axis=1)
        yb = yb.transpose(3, 0, 1, 2, 4)[:dec_len].reshape(rows_s, ATT_DIM)
        xs = _merge(xs, a_act, yb, zb_act, gmix, p_sample_rows, *merge_w, layer=i, tm=rows_s, final=last)
        kv_s.append(newkv_b.reshape(n_dec, dec_len, 4, N_KV, HEAD_DIM))
        win_new_s.append(newwin_b.reshape(n_dec, dec_len, 2, N_KV, HEAD_DIM))
        conv_s.append(tail.reshape(CONV_W - 1, n_dec, CONV_DIM).transpose(1, 0, 2))

    y_prompt = xp.reshape(n_b, seq_len, D_MODEL)
    y_sample = xs.reshape(dec_len, n_dec, D_MODEL).transpose(1, 0, 2)
    newkv_all, win_all = carried
    kv_p = newkv_all.reshape(depth, n_b, 4, N_KV, HEAD_DIM, seq_len).transpose(0, 1, 5, 2, 3, 4)
    win_p = win_all.reshape(depth, n_b, 2, N_KV, HEAD_DIM, min(WINDOW, seq_len)).transpose(0, 1, 5, 2, 3, 4)
    win_s = jnp.concatenate([state_win, jnp.stack(win_new_s)], axis=2)[:, :, -w_buf:]
    return (y_prompt, y_sample, kv_p, jnp.stack(kv_s), win_p, win_s, jnp.stack(conv_p), jnp.stack(conv_s))
```

```python
import functools

import numpy as np
import jax
import jax.numpy as jnp
from jax import lax
from jax.experimental import pallas as pl
from jax.experimental.pallas import tpu as pltpu

D_MODEL = 1024
CONV_DIM = D_MODEL
CONV_W = 3
N_HEADS = 16
HEAD_DIM = 64
N_KV = 4
GROUP = N_HEADS // N_KV
ATT_DIM = N_HEADS * HEAD_DIM
KV_DIM = N_KV * HEAD_DIM
PAGE_SIZE = 128
CMP_STRIDE = 16
CMP_BLOCK = 2 * CMP_STRIDE
CMP_HID = 4 * HEAD_DIM
SLC_BLOCK = 64
TOP_N = 16
WINDOW = 512
PLE_DIM = 256
ROPE_THETA = 10000.0
RMS_EPS = 1e-6
NEG_INF = -1e30
FORCE_SCORE = 1e6

LANE = 128
SUBLANE = 8
VMEM_LIMIT = 56 * 1024 * 1024
KV_SLABS = KV_DIM // LANE
CMP_DEPTH = 2 * LANE
CHAIN_COLS = 4 * LANE

OFF_A = 0
OFF_Q = 4 * CONV_DIM
OFF_KV = OFF_Q + ATT_DIM
OFF_WIN = OFF_KV + 4 * KV_DIM
OFF_ZB = OFF_WIN + 2 * KV_DIM
OFF_GM = OFF_ZB + ATT_DIM
OFF_GB = OFF_GM + 2 * D_MODEL
N_PROJ = OFF_GB + LANE
MAX_BLOCKS = LANE - HEAD_DIM

F32 = jnp.float32
BF16 = jnp.bfloat16
NT_DIMS = (((1,), (1,)), ((), ()))


def _params(n_axes):
    return pltpu.CompilerParams(dimension_semantics=("arbitrary",) * n_axes, vmem_limit_bytes=VMEM_LIMIT)


def _const_spec(shape):
    zeros = (0,) * len(shape)
    return pl.BlockSpec(shape, lambda *_: zeros, pipeline_mode=pl.Buffered(1))


def _layer_spec(shape, layer):
    index = (layer,) + (0,) * len(shape)
    return pl.BlockSpec((None,) + tuple(shape), lambda *_: index, pipeline_mode=pl.Buffered(1))


def _sigmoid(x):
    return 1.0 / (1.0 + jnp.exp(-x))


def _div(x, d):
    assert d & (d - 1) == 0
    return jnp.right_shift(x, d.bit_length() - 1)


def _mod(x, d):
    assert d & (d - 1) == 0
    return jnp.bitwise_and(x, d - 1)


def _iota(shape, dim):
    return lax.broadcasted_iota(jnp.int32, shape, dim)


def _rope_slab(x, cos, sin, lane):
    swapped = jnp.where(_mod(lane, HEAD_DIM) < HEAD_DIM // 2, pltpu.roll(x, LANE - HEAD_DIM // 2, 1),
                        pltpu.roll(x, HEAD_DIM // 2, 1))
    return x * cos + swapped * sin


def _overlap_t(n_blk, n_cmp):
    bj = _iota((n_blk, n_cmp), 0)
    bm = _iota((n_blk, n_cmp), 1)
    ratio = SLC_BLOCK // CMP_STRIDE
    return jnp.where((bm >= ratio * bj) & (bm <= ratio * bj + ratio) & (bm >= 1), 1.0, 0.0).astype(BF16)


def _proj_kernel(x_ref, g_ref, w_ref, cos_ref, sin_ref, pre_ref, cw_ref, *refs, tm, stride, carry, transposed,
                 n_carried):
    a_ref, kv_ref, win_ref, zb_ref, gm_ref, tail_ref, *rest = refs[n_carried:]
    if transposed:
        qt_ref, ks_ref, vst_ref, kw_ref, vwt_ref, gbt_ref, cmp_ref, u_scr = rest
    else:
        q_ref, gb_ref, u_scr = rest
    t = pl.program_id(1)
    x = x_ref[...]
    ms = jnp.mean(x * x, axis=-1, keepdims=True)
    h = (x * lax.rsqrt(ms + RMS_EPS) * g_ref[...]).astype(BF16)

    def seg(off, width):
        return jnp.dot(h, w_ref[:, off:off + width], preferred_element_type=F32)

    @pl.when(t == 0)
    def _():
        u_scr[0:carry, :] = pre_ref[0]

    u = seg(OFF_A + CONV_DIM, CONV_DIM) * seg(OFF_A + 2 * CONV_DIM, CONV_DIM)
    u_scr[carry:carry + tm, :] = u
    cw = cw_ref[...]
    conv = cw[0:1] * u_scr[carry - 2 * stride:carry - 2 * stride + tm, :]
    conv = conv + cw[1:2] * u_scr[carry - stride:carry - stride + tm, :]
    conv = conv + cw[2:3] * u
    ba = seg(OFF_A, CONV_DIM)
    za = seg(OFF_A + 3 * CONV_DIM, CONV_DIM)
    a_ref[...] = ((za * _sigmoid(za)) * (ba * conv)).astype(BF16)
    tail = u_scr[tm:tm + carry, :]
    tail_ref[0] = tail
    u_scr[0:carry, :] = tail

    lane = _iota((tm, LANE), 1)
    cos = cos_ref[...]
    sin = sin_ref[...]
    scale = HEAD_DIM ** -0.5 * (float(np.log2(np.e)) if transposed else 1.0)

    zq = seg(OFF_Q, ATT_DIM)
    for s in range(ATT_DIM // LANE):
        qs = _rope_slab(zq[:, s * LANE:(s + 1) * LANE], cos, sin, lane) * scale
        if transposed:
            qst = qs.T.astype(BF16)
            qt_ref[0, 2 * s] = qst[:HEAD_DIM]
            qt_ref[0, 2 * s + 1] = qst[HEAD_DIM:]
        else:
            q_ref[:, s * LANE:(s + 1) * LANE] = qs.astype(BF16)

    zkv = seg(OFF_KV, 4 * KV_DIM)
    zwin = seg(OFF_WIN, 2 * KV_DIM)
    kv_slabs, kv_t = [], []
    for s in range(4 * KV_SLABS):
        v = zkv[:, s * LANE:(s + 1) * LANE]
        if (s // KV_SLABS) % 2 == 0:
            v = _rope_slab(v, cos, sin, lane)
        if transposed:
            kv_t.append(v.T)
            kv_ref[s * LANE:(s + 1) * LANE, :] = kv_t[-1]
        else:
            kv_ref[:, s * LANE:(s + 1) * LANE] = v
        kv_slabs.append(v)
    win_slabs, win_t = [], []
    for s in range(2 * KV_SLABS):
        v = zwin[:, s * LANE:(s + 1) * LANE]
        if s // KV_SLABS == 0:
            v = _rope_slab(v, cos, sin, lane)
        if transposed:
            win_t.append(v.T)
            win_ref[s * LANE:(s + 1) * LANE, :] = win_t[-1]
        else:
            win_ref[:, s * LANE:(s + 1) * LANE] = v
        win_slabs.append(v)

    zb = seg(OFF_ZB, ATT_DIM)
    zb_ref[...] = zb * _sigmoid(zb)
    gb = _sigmoid(seg(OFF_GB, LANE))

    def merge_gates():
        gm_ref[...] = _sigmoid(seg(OFF_GM, 2 * D_MODEL))

    if not transposed:
        gb_ref[...] = gb
        merge_gates()
        return

    row = _iota((tm, LANE), 0)
    onehot = jnp.where(lane - HEAD_DIM == _div(t * tm + row, SLC_BLOCK), 1.0, 0.0)
    low = lane < HEAD_DIM
    ones_row = jnp.where(_iota((HEAD_DIM, tm), 0) == 0, 1.0, 0.0).astype(BF16)
    vs_t = [kv_t[3 * KV_SLABS + s].astype(BF16) for s in range(KV_SLABS)]
    vw_t = [win_t[KV_SLABS + s].astype(BF16) for s in range(KV_SLABS)]
    for s in range(2 * KV_SLABS):
        cmp_ref[s] = kv_slabs[s]
    for g in range(N_KV):
        s, hi = divmod(g, 2)

        def half(v):
            return v if hi == 0 else pltpu.roll(v, HEAD_DIM, 1)

        ks_ref[0, g] = jnp.where(low, half(kv_slabs[2 * KV_SLABS + s]), onehot).astype(BF16)
        kw_ref[0, g] = jnp.where(low, half(win_slabs[s]), 0.0).astype(BF16)
        vst_ref[0, g] = jnp.concatenate([vs_t[s][hi * HEAD_DIM:(hi + 1) * HEAD_DIM], ones_row], axis=0)
        vwt_ref[0, g] = jnp.concatenate([vw_t[s][hi * HEAD_DIM:(hi + 1) * HEAD_DIM], ones_row], axis=0)
    gbt_ref[0] = gb.T
    merge_gates()


def _window_pads_kernel(kw_ref, vwt_ref):
    flag = jnp.where(_iota((WINDOW, LANE), 1) == HEAD_DIM, 1.0, 0.0).astype(BF16)
    for g in range(N_KV):
        kw_ref[0, g] = flag
        vwt_ref[0, g] = jnp.zeros((LANE, WINDOW), BF16)


def _window_pads(n_seq, rows_per_seq):
    padded = rows_per_seq + WINDOW
    return pl.pallas_call(
        _window_pads_kernel,
        grid=(n_seq,),
        out_specs=[pl.BlockSpec((1, N_KV, WINDOW, LANE), lambda b: (b, 0, 0, 0)),
                   pl.BlockSpec((1, N_KV, LANE, WINDOW), lambda b: (b, 0, 0, 0))],
        out_shape=[jax.ShapeDtypeStruct((n_seq, N_KV, padded, LANE), BF16),
                   jax.ShapeDtypeStruct((n_seq, N_KV, LANE, padded), BF16)],
        compiler_params=_params(1),
        name="window_pads",
    )()


def _project(x2d, ln_g, w_perm, cos, sin, prefix, conv_w8, carried, window_pads, *, layer, n_seq, rows_per_seq, tm,
             stride, carry, transposed, pos_per_seq):
    nt = rows_per_seq // tm
    n_rows = n_seq * rows_per_seq
    depth = w_perm.shape[0]
    row_map = lambda b, t: (b * nt + t, 0)
    pos_map = (lambda b, t: (t, 0)) if pos_per_seq else row_map
    in_specs = [
        pl.BlockSpec((tm, D_MODEL), row_map),
        _layer_spec((1, D_MODEL), layer),
        _layer_spec((D_MODEL, N_PROJ), layer),
        pl.BlockSpec((tm, LANE), pos_map),
        pl.BlockSpec((tm, LANE), pos_map),
        pl.BlockSpec((1, carry, CONV_DIM), lambda b, t: (b, 0, 0)),
        _layer_spec((SUBLANE, CONV_DIM), layer),
    ]
    rows = lambda width, dtype: (jax.ShapeDtypeStruct((n_rows, width), dtype), pl.BlockSpec((tm, width), row_map))
    if transposed:
        keep = min(WINDOW, rows_per_seq)
        assert keep % tm == 0
        first_kept = (rows_per_seq - keep) // tm
        kv_out = (jax.ShapeDtypeStruct((depth, n_seq, 4 * KV_DIM, rows_per_seq), F32),
                  pl.BlockSpec((None, None, 4 * KV_DIM, tm), lambda b, t: (layer, b, 0, t)))
        win_out = (jax.ShapeDtypeStruct((depth, n_seq, 2 * KV_DIM, keep), F32),
                   pl.BlockSpec((None, None, 2 * KV_DIM, tm),
                                lambda b, t: (layer, b, 0, jnp.maximum(t - first_kept, 0))))
    else:
        kv_out, win_out = rows(4 * KV_DIM, F32), rows(2 * KV_DIM, F32)
    outs = [
        rows(CONV_DIM, BF16),
        kv_out,
        win_out,
        rows(ATT_DIM, F32),
        rows(2 * D_MODEL, F32),
        (jax.ShapeDtypeStruct((n_seq, carry, CONV_DIM), F32),
         pl.BlockSpec((1, carry, CONV_DIM), lambda b, t: (b, 0, 0))),
    ]
    if transposed:
        tok_sub = (jax.ShapeDtypeStruct((n_seq, N_KV, rows_per_seq, LANE), BF16),
                   pl.BlockSpec((1, N_KV, tm, LANE), lambda b, t: (b, 0, t, 0)))
        tok_lane = lambda n, rows_, dtype: (jax.ShapeDtypeStruct((n_seq, n, rows_, rows_per_seq), dtype),
                                            pl.BlockSpec((1, n, rows_, tm), lambda b, t: (b, 0, 0, t)))
        assert WINDOW % tm == 0
        padded = rows_per_seq + WINDOW
        win_sub = (jax.ShapeDtypeStruct((n_seq, N_KV, padded, LANE), BF16),
                   pl.BlockSpec((1, N_KV, tm, LANE), lambda b, t: (b, 0, t + WINDOW // tm, 0)))
        win_lane = (jax.ShapeDtypeStruct((n_seq, N_KV, LANE, padded), BF16),
                    pl.BlockSpec((1, N_KV, LANE, tm), lambda b, t: (b, 0, 0, t + WINDOW // tm)))
        outs += [tok_lane(N_HEADS, HEAD_DIM, BF16),
                 tok_sub, tok_lane(N_KV, LANE, BF16),
                 win_sub, win_lane,
                 (jax.ShapeDtypeStruct((n_seq, LANE, rows_per_seq), F32),
                  pl.BlockSpec((1, LANE, tm), lambda b, t: (b, 0, t))),
                 (jax.ShapeDtypeStruct((2 * KV_SLABS, n_rows, LANE), F32),
                  pl.BlockSpec((2 * KV_SLABS, tm, LANE), lambda b, t: (0, b * nt + t, 0)))]
    else:
        outs += [rows(ATT_DIM, BF16), rows(LANE, F32)]
    args = [x2d, ln_g, w_perm, cos, sin, prefix, conv_w8]
    aliases = {}
    if carried is not None:
        assert transposed
        aliases = {len(args): 1, len(args) + 1: 2}
        args += list(carried)
    if window_pads is not None:
        aliases.update({len(args): 9, len(args) + 1: 10})
        args += list(window_pads)
    in_specs += [pl.BlockSpec(memory_space=pl.ANY)] * len(aliases)
    return pl.pallas_call(
        functools.partial(_proj_kernel, tm=tm, stride=stride, carry=carry, transposed=transposed,
                          n_carried=len(aliases)),
        grid=(n_seq, nt),
        in_specs=in_specs,
        out_specs=[o[1] for o in outs],
        out_shape=[o[0] for o in outs],
        scratch_shapes=[pltpu.VMEM((carry + tm, CONV_DIM), F32)],
        input_output_aliases=aliases,
        compiler_params=_params(2),
        name="project_prompt" if transposed else "project_sample",
    )(*args)


def _merge_kernel(x_ref, a_ref, gm_ref, p_ref, woa_ref, wob_ref, wo_ref, wple_ref, wpg_ref, fg_ref, y_ref, *rest,
                  y_needs_gate, final):
    br_a = jnp.dot(a_ref[...], woa_ref[...], preferred_element_type=F32)
    if y_needs_gate:
        zb_ref, o_ref = rest
        y = (zb_ref[...] * y_ref[...]).astype(BF16)
    else:
        (o_ref,) = rest
        y = y_ref[...]
    br_b = jnp.dot(y, wob_ref[...], preferred_element_type=F32)
    gm = gm_ref[...]
    mix = gm[:, :D_MODEL] * br_a + gm[:, D_MODEL:] * br_b
    x1 = x_ref[...] + jnp.dot(mix.astype(BF16), wo_ref[...], preferred_element_type=F32)
    gate = _sigmoid(jnp.dot(x1.astype(BF16), wpg_ref[...], preferred_element_type=F32))
    x2 = x1 + gate * jnp.dot(p_ref[...].astype(BF16), wple_ref[...], preferred_element_type=F32)
    if final:
        ms = jnp.mean(x2 * x2, axis=-1, keepdims=True)
        x2 = x2 * lax.rsqrt(ms + RMS_EPS) * fg_ref[...]
    o_ref[...] = x2


def _merge(x2d, a_act, y, zb_act, gmix, p2d, w_out_a, w_out_b, w_o, w_ple, w_pg, final_g, *, layer, tm, final):
    n_rows = x2d.shape[0]
    row = lambda width: pl.BlockSpec((tm, width), lambda t: (t, 0))
    y_args = (y,) if zb_act is None else (y, zb_act)
    return pl.pallas_call(
        functools.partial(_merge_kernel, y_needs_gate=zb_act is not None, final=final),
        grid=(n_rows // tm,),
        in_specs=[row(D_MODEL), row(CONV_DIM), row(2 * D_MODEL),
                  pl.BlockSpec((None, tm, PLE_DIM), lambda t: (layer, t, 0)),
                  _layer_spec((CONV_DIM, D_MODEL), layer), _layer_spec((ATT_DIM, D_MODEL), layer),
                  _layer_spec((D_MODEL, D_MODEL), layer), _layer_spec((PLE_DIM, D_MODEL), layer),
                  _layer_spec((D_MODEL, D_MODEL), layer), _const_spec((1, D_MODEL))] + [row(ATT_DIM)] * len(y_args),
        out_specs=row(D_MODEL),
        out_shape=jax.ShapeDtypeStruct((n_rows, D_MODEL), F32),
        compiler_params=_params(1),
        name="merge_prompt" if zb_act is None else "merge_sample",
    )(x2d, a_act, gmix, p2d, w_out_a, w_out_b, w_o, w_ple, w_pg, final_g, *y_args)


def _compress_kernel(pt_ref, *refs, n_pages, prompt):
    pages = refs[:n_pages]
    pos_ref, w1_ref, w2_ref, ko_ref, vo_ref, rows_scr, x_scr, carry_scr = refs[n_pages:]
    st = pl.program_id(1)
    per_page = PAGE_SIZE // CMP_STRIDE
    n = n_pages * per_page
    low = _iota((n, LANE), 1) < HEAD_DIM

    @pl.when(st == 0)
    def _():
        carry_scr[...] = jnp.zeros_like(carry_scr)

    row0 = _iota((n, CMP_HID), 0) == 0
    def relayout(slot):
        for j in range(n_pages):
            for v in range(KV_SLABS):
                if prompt:
                    piece = pages[j][slot * KV_SLABS + v]
                else:
                    piece = pages[j][slot, v * LANE:(v + 1) * LANE, :].T
                rows_scr[slot, v, j * PAGE_SIZE:(j + 1) * PAGE_SIZE, :] = piece
        for pair in range(CMP_STRIDE // 2):
            for v in range(KV_SLABS):
                a = rows_scr[slot, v, pl.ds(2 * pair, n, stride=CMP_STRIDE), :]
                b = rows_scr[slot, v, pl.ds(2 * pair + 1, n, stride=CMP_STRIDE), :]
                x_scr[slot, 2 * v * n:(2 * v + 1) * n, pair * LANE:(pair + 1) * LANE] = jnp.where(
                    low, a, pltpu.roll(b, HEAD_DIM, 1))
                x_scr[slot, (2 * v + 1) * n:(2 * v + 2) * n, pair * LANE:(pair + 1) * LANE] = jnp.where(
                    low, pltpu.roll(a, HEAD_DIM, 1), b)
    for slot, out_ref in enumerate((ko_ref, vo_ref)):
        relayout(slot)
        pos = pos_ref[slot]
        f = s = None
        for k0 in range(0, x_scr.shape[2], CMP_DEPTH):
            ks = slice(k0, k0 + CMP_DEPTH)
            xs = x_scr[slot, :, ks]
            f_part = jnp.dot((xs + pos[0:1, ks]).astype(BF16), w1_ref[slot, 0, ks, :], preferred_element_type=F32)
            s_part = jnp.dot((xs + pos[1:2, ks]).astype(BF16), w1_ref[slot, 1, ks, :], preferred_element_type=F32)
            f = f_part if f is None else f + f_part
            s = s_part if s is None else s + s_part
        acc = None
        for g in range(N_KV):
            fg = f[g * n:(g + 1) * n]
            prev = carry_scr[slot, g, SUBLANE - 1:SUBLANE, :]
            shifted = jnp.where(row0, prev, pltpu.roll(fg, 1, 0))
            carry_scr[slot, g] = fg[n - SUBLANE:n]
            pre = shifted + s[g * n:(g + 1) * n]
            hid = (pre * _sigmoid(pre)).astype(BF16)
            if prompt and slot == 0:
                out_ref[0, g] = jnp.dot(hid, w2_ref[0, 0, :, :LANE], preferred_element_type=F32).astype(BF16)
            elif prompt:
                out_ref[0, g] = lax.dot_general(w2_ref[1, 0, :LANE, :], hid, NT_DIMS,
                                                preferred_element_type=F32).astype(BF16)
            else:
                part = jnp.dot(hid, w2_ref[slot, g], preferred_element_type=F32)
                acc = part if acc is None else acc + part
        if not prompt:
            out_ref[0] = acc.astype(BF16)


def _compress(pages_arr, page_table, page_index, pos2, w1, w2, *, layer, n_pages, prompt):
    n_seq, pages_per_seq = page_table.shape
    n_steps = pages_per_seq // n_pages
    n = n_pages * (PAGE_SIZE // CMP_STRIDE)
    n_chunk = pages_per_seq * (PAGE_SIZE // CMP_STRIDE)
    flat = CMP_STRIDE * HEAD_DIM

    def page_spec(j):
        if prompt:
            return pl.BlockSpec((2 * KV_SLABS, PAGE_SIZE, LANE),
                                lambda b, s, pt: (0, page_index(pt[b, s * n_pages + j]), 0))
        return pl.BlockSpec((2, KV_DIM, PAGE_SIZE), lambda b, s, pt: (page_index(pt[b, s * n_pages + j]), 0, 0))

    if prompt:
        out_sds = [jax.ShapeDtypeStruct((n_seq, N_KV, n_chunk, LANE), BF16),
                   jax.ShapeDtypeStruct((n_seq, N_KV, LANE, n_chunk), BF16)]
        out_specs = [pl.BlockSpec((1, N_KV, n, LANE), lambda b, s, pt: (b, 0, s, 0)),
                     pl.BlockSpec((1, N_KV, LANE, n), lambda b, s, pt: (b, 0, 0, s))]
    else:
        out_sds = [jax.ShapeDtypeStruct((n_seq, n_chunk, KV_DIM), BF16)] * 2
        out_specs = [pl.BlockSpec((1, n, KV_DIM), lambda b, s, pt: (b, s, 0))] * 2
    grid_spec = pltpu.PrefetchScalarGridSpec(
        num_scalar_prefetch=1,
        grid=(n_seq, n_steps),
        in_specs=[page_spec(j) for j in range(n_pages)] + [_layer_spec(a.shape[1:], layer) for a in (pos2, w1, w2)],
        out_specs=out_specs,
        scratch_shapes=[pltpu.VMEM((2, KV_SLABS, n_pages * PAGE_SIZE, LANE), F32),
                        pltpu.VMEM((2, N_KV * n, flat), F32),
                        pltpu.VMEM((2, N_KV, SUBLANE, CMP_HID), F32)],
    )
    return pl.pallas_call(
        functools.partial(_compress_kernel, n_pages=n_pages, prompt=prompt),
        grid_spec=grid_spec,
        out_shape=out_sds,
        compiler_params=_params(2),
        name="compress_prompt" if prompt else "compress_sample",
    )(page_table, *([pages_arr] * n_pages), pos2, w1, w2)


def _select_bias(score_ref, cnt_ref, n_real, n_live, n_sel):
    score = score_ref[...]
    n_rows = -(-n_real // SUBLANE) * SUBLANE
    sub = _iota((SUBLANE, score.shape[1]), 0)
    cnt_ref[...] = jnp.zeros(score.shape, F32)

    def group(i0):
        cnt = cnt_ref[0:n_rows, :]
        for i in range(i0, min(i0 + SUBLANE, n_real)):
            si = score_ref[i:i + 1, :]
            pieces = []
            for r0 in range(0, n_rows, SUBLANE):
                sv = score[r0:r0 + SUBLANE]
                if r0 > i:
                    pieces.append(jnp.where(si >= sv, 1.0, 0.0))
                elif r0 + SUBLANE - 1 < i:
                    pieces.append(jnp.where(si > sv, 1.0, 0.0))
                else:
                    pieces.append(jnp.where(sub > i - r0, jnp.where(si >= sv, 1.0, 0.0), jnp.where(si > sv, 1.0, 0.0)))
            cnt = cnt + jnp.concatenate(pieces, axis=0)
        cnt_ref[0:n_rows, :] = cnt

    for i0 in range(0, n_real, SUBLANE):
        if n_live is None:
            group(i0)
        else:
            pl.when(i0 < n_live)(functools.partial(group, i0))
    return jnp.where((cnt_ref[...] < n_sel) & (score >= 0.0), 0.0, NEG_INF)


def _attn_prompt_kernel(qt_ref, gbt_ref, zb_ref, tri_ref, kc_ref, vct_ref, ks_ref, vst_ref, kw_ref, vwt_ref, o_ref,
                        score_scr, cnt_scr, m_scr, acc_scr, mx_scr, s_even_scr, s_odd_scr, *, tq, n_cmp, n_blk):
    qi = pl.program_id(2)
    s0 = qi * tq
    rows = GROUP * tq
    cw = min(rows, CHAIN_COLS)
    col_groups = [slice(c, c + cw) for c in range(0, rows, cw)]
    qpos = s0 + _mod(_iota((1, rows), 1), tq)

    def stack_q(lower):
        return jnp.concatenate([jnp.concatenate([qt_ref[0, r], lower], axis=0) for r in range(GROUP)], axis=1)

    q0 = stack_q(jnp.where(_iota((LANE - HEAD_DIM, tq), 0) == 0, NEG_INF, 0.0).astype(BF16))

    s_c = jnp.dot(kc_ref[0, 0], q0, preferred_element_type=F32)

    def window_scores(cs, first, n_keys):
        keys = kw_ref[0, 0, pl.ds(pl.multiple_of(s0 + first, tq), n_keys), :]
        return jnp.dot(keys, q0[:, cs], preferred_element_type=F32)

    w_parts = []
    for cs in col_groups:
        parts = [window_scores(cs, 0, tq) + tri_ref[0, :, cs]]
        if WINDOW > tq:
            parts.append(window_scores(cs, tq, WINDOW - tq))
        parts.append(window_scores(cs, WINDOW, tq) + tri_ref[1, :, cs])
        w_parts.append(parts)

    m_idx = _iota((n_cmp, 1), 0)
    mask_c = (m_idx >= 1) & (CMP_STRIDE * m_idx + CMP_STRIDE - 1 <= qpos)
    s_c = jnp.where(mask_c, s_c, NEG_INF)
    e_c = jnp.where(mask_c, jnp.exp2(s_c - jnp.max(s_c, axis=0, keepdims=True)), 0.0)
    p_c = e_c * (1.0 / jnp.maximum(jnp.sum(e_c, axis=0, keepdims=True), 1e-30))
    o_c = jnp.dot(vct_ref[0, 0], p_c.astype(BF16), preferred_element_type=F32)

    p_sum = p_c[:, 0:tq]
    for r in range(1, GROUP):
        p_sum = p_sum + p_c[:, r * tq:(r + 1) * tq]
    p_hi = p_sum.astype(BF16)
    p_lo = (p_sum - p_hi.astype(F32)).astype(BF16)
    ov_t = _overlap_t(MAX_BLOCKS, n_cmp)
    imp = jnp.dot(ov_t, p_hi, preferred_element_type=F32) + jnp.dot(ov_t, p_lo, preferred_element_type=F32)

    w_out = []
    for cs, parts in zip(col_groups, w_parts):
        m_w = functools.reduce(jnp.maximum, [jnp.max(s, axis=0, keepdims=True) for s in parts])
        p_w = jnp.concatenate([jnp.exp2(s - m_w).astype(BF16) for s in parts], axis=0)
        acc_w = jnp.dot(vwt_ref[0, 0, :, pl.ds(pl.multiple_of(s0, tq), WINDOW + tq)], p_w, preferred_element_type=F32)
        w_out.append(acc_w[:HEAD_DIM] / jnp.maximum(acc_w[HEAD_DIM:HEAD_DIM + 1], 1e-30))
    o_w = jnp.concatenate(w_out, axis=1)

    blk = _iota((MAX_BLOCKS, tq), 0)
    cur = _div(s0 + _iota((MAX_BLOCKS, tq), 1), SLC_BLOCK)
    forced = (blk == 0) | (blk == cur) | (blk == cur - 1)
    score = jnp.where(forced, FORCE_SCORE, imp)
    score_scr[...] = jnp.where(blk <= cur, score, -1.0)
    n_live = jnp.minimum(n_blk, (s0 + tq) // SLC_BLOCK)
    bias_t = _select_bias(score_scr, cnt_scr, n_blk, n_live, min(TOP_N, n_blk))
    qx = stack_q(bias_t.astype(BF16))

    m_scr[...] = jnp.full(m_scr.shape, NEG_INF, F32)
    acc_scr[...] = jnp.zeros(acc_scr.shape, F32)
    s_bufs = (s_even_scr, s_odd_scr)

    def scores(k0, slot, causal):
        keys = ks_ref[0, 0, pl.ds(pl.multiple_of(k0, tq), tq), :]
        prods = [jnp.dot(keys, qx[:, cs], preferred_element_type=F32) for cs in col_groups]
        for cs, s in zip(col_groups, prods):
            if causal:
                s = s + tri_ref[1, :, cs]
            s_bufs[slot][:, cs] = s
            mx_scr[slot, 0:1, cs] = jnp.max(s, axis=0, keepdims=True)

    def accumulate(k0, slot):
        vals_t = vst_ref[0, 0, :, pl.ds(pl.multiple_of(k0, tq), tq)]
        for cs in col_groups:
            m_old = m_scr[0:1, cs]
            m_new = jnp.maximum(m_old, mx_scr[slot, 0:1, cs])
            p = jnp.exp2(s_bufs[slot][:, cs] - m_new).astype(BF16)
            acc_scr[:, cs] = jnp.exp2(m_old - m_new) * acc_scr[:, cs] + jnp.dot(vals_t, p, preferred_element_type=F32)
            m_scr[0:1, cs] = m_new

    n_pairs = qi // 2
    scores(s0, 0, True)

    def pair(j, _):
        scores(2 * j * tq, 1, False)
        accumulate(jnp.where(j == 0, s0, (2 * j - 1) * tq), 0)
        scores((2 * j + 1) * tq, 0, False)
        accumulate(2 * j * tq, 1)
        return 0

    lax.fori_loop(0, n_pairs, pair, 0)
    pending = jnp.where(n_pairs == 0, s0, (2 * n_pairs - 1) * tq)

    @pl.when(qi - 2 * n_pairs == 1)
    def _():
        scores(s0 - tq, 1, False)
        accumulate(pending, 0)
        accumulate(s0 - tq, 1)

    @pl.when(qi - 2 * n_pairs == 0)
    def _():
        accumulate(pending, 0)

    o_s = acc_scr[:HEAD_DIM, :] / jnp.maximum(acc_scr[HEAD_DIM:HEAD_DIM + 1, :], 1e-30)

    def gate(r, k):
        return gbt_ref[0, pl.ds(3 * (GROUP * pl.program_id(1) + r) + k, 1), :]

    zb = zb_ref[0]
    heads = []
    for r in range(GROUP):
        rs = slice(r * tq, (r + 1) * tq)
        heads.append(gate(r, 0) * o_c[:HEAD_DIM, rs] + gate(r, 1) * o_s[:, rs] + gate(r, 2) * o_w[:, rs])
    for pair in range(GROUP // 2):
        slab = jnp.concatenate([heads[2 * pair], heads[2 * pair + 1]], axis=0).T
        o_ref[0, :, pair * LANE:(pair + 1) * LANE] = (zb[:, pair * LANE:(pair + 1) * LANE] * slab).astype(BF16)


def _attn_prompt(q_t, gb_t, zb_act, kc_ext, vc_t, ks_ext, vs_t, kw_pad, vw_pad, *, tq):
    n_b, seq_len = zb_act.shape[0], zb_act.shape[1]
    n_cmp = kc_ext.shape[2]
    n_blk = seq_len // SLC_BLOCK
    head_w = GROUP * HEAD_DIM
    rows = GROUP * tq
    assert WINDOW % tq == 0 and rows % min(rows, CHAIN_COLS) == 0
    off = _iota((tq, rows), 0) - _mod(_iota((tq, rows), 1), tq)
    tri = jnp.stack([jnp.where(off > 0, 0.0, NEG_INF), jnp.where(off <= 0, 0.0, NEG_INF)]).astype(F32)
    tok_sub = lambda n: pl.BlockSpec((1, 1, n, LANE), lambda b, g, i: (b, g, 0, 0))
    tok_lane = lambda n: pl.BlockSpec((1, 1, LANE, n), lambda b, g, i: (b, g, 0, 0))
    return pl.pallas_call(
        functools.partial(_attn_prompt_kernel, tq=tq, n_cmp=n_cmp, n_blk=n_blk),
        grid=(n_b, N_KV, seq_len // tq),
        in_specs=[pl.BlockSpec((1, GROUP, HEAD_DIM, tq), lambda b, g, i: (b, g, 0, i)),
                  pl.BlockSpec((1, LANE, tq), lambda b, g, i: (b, 0, i)),
                  pl.BlockSpec((1, tq, head_w), lambda b, g, i: (b, i, g)),
                  _const_spec((2, tq, rows)),
                  tok_sub(n_cmp), tok_lane(n_cmp), tok_sub(seq_len), tok_lane(seq_len),
                  tok_sub(seq_len + WINDOW), tok_lane(seq_len + WINDOW)],
        out_specs=pl.BlockSpec((1, tq, head_w), lambda b, g, i: (b, i, g)),
        out_shape=jax.ShapeDtypeStruct((n_b, seq_len, ATT_DIM), BF16),
        scratch_shapes=[pltpu.VMEM((MAX_BLOCKS, tq), F32), pltpu.VMEM((MAX_BLOCKS, tq), F32),
                        pltpu.VMEM((SUBLANE, rows), F32),
                        pltpu.VMEM((LANE, rows), F32), pltpu.VMEM((2, SUBLANE, rows), F32),
                        pltpu.VMEM((tq, rows), F32), pltpu.VMEM((tq, rows), F32)],
        compiler_params=_params(3),
        name="attention_prompt",
    )(q_t, gb_t, zb_act, tri, kc_ext, vc_t, ks_ext, vs_t, kw_pad, vw_pad)


def _attn_sample_kernel(pt_ref, *refs, n_pages, past_len, dec_len, n_blk_pad):
    pages = refs[:n_pages]
    (qbd_ref, qbdt_ref, kc_ref, vc_ref, newkv_ref, win_ref, newwin_ref, gate_ref, o_ref,
     score_scr, cnt_scr, bias_scr, m_scr, l_scr, acc_scr, oc_scr) = refs[n_pages:]
    st = pl.program_id(1)
    n_steps = pl.num_programs(1)
    cols = LANE
    tslots = cols // N_HEADS
    n_cmp = kc_ref.shape[1]
    n_past_blk = past_len // SLC_BLOCK
    per_page = PAGE_SIZE // SLC_BLOCK
    qbdt = qbdt_ref[0]
    tok_c = _mod(_iota((cols, 1), 0), tslots)

    @pl.when(st == 0)
    def _():
        qpos = past_len + _mod(_iota((1, cols), 1), tslots)
        s_c = jnp.dot(kc_ref[0], qbd_ref[0], preferred_element_type=F32)
        m_idx = _iota((n_cmp, 1), 0)
        mask_c = (m_idx >= 1) & (CMP_STRIDE * m_idx + CMP_STRIDE - 1 <= qpos)
        s_c = jnp.where(mask_c, s_c, NEG_INF)
        e_c = jnp.where(mask_c, jnp.exp(s_c - jnp.max(s_c, axis=0, keepdims=True)), 0.0)
        p_c = e_c / jnp.maximum(jnp.sum(e_c, axis=0, keepdims=True), 1e-30)
        p_hi = p_c.astype(BF16)
        p_lo = (p_c - p_hi.astype(F32)).astype(BF16)
        oc_scr[...] = jnp.dot(p_c.T.astype(BF16), vc_ref[0], preferred_element_type=F32)
        ov_t = _overlap_t(n_blk_pad, n_cmp)
        imp1 = jnp.dot(ov_t, p_hi, preferred_element_type=F32) + jnp.dot(ov_t, p_lo, preferred_element_type=F32)
        ca = _iota((cols, cols), 0)
        cb = _iota((cols, cols), 1)
        per_group = GROUP * tslots
        same = jnp.where((_div(ca, per_group) == _div(cb, per_group)) & (_mod(ca, tslots) == _mod(cb, tslots)),
                         1.0, 0.0).astype(BF16)
        i_hi = imp1.astype(BF16)
        i_mid = (imp1 - i_hi.astype(F32)).astype(BF16)
        i_lo = (imp1 - i_hi.astype(F32) - i_mid.astype(F32)).astype(BF16)
        imp = (jnp.dot(i_hi, same, preferred_element_type=F32) + jnp.dot(i_mid, same, preferred_element_type=F32)
               + jnp.dot(i_lo, same, preferred_element_type=F32))
        blk = _iota((n_blk_pad, cols), 0)
        cur = _div(qpos, SLC_BLOCK)
        forced = (blk == 0) | (blk == cur) | (blk == cur - 1)
        score = jnp.where(forced, FORCE_SCORE, imp)
        score_scr[...] = jnp.where(blk <= cur, score, -1.0)
        n_blk = n_past_blk + 1
        bias_scr[...] = _select_bias(score_scr, cnt_scr, n_blk, None, min(TOP_N, n_blk)).T
        m_scr[...] = jnp.full(m_scr.shape, NEG_INF, F32)
        l_scr[...] = jnp.zeros(l_scr.shape, F32)
        acc_scr[...] = jnp.zeros(acc_scr.shape, F32)

    def update(s, pv_fn):
        m_old = m_scr[...]
        m_new = jnp.maximum(m_old, jnp.max(s, axis=1, keepdims=True))
        alpha = jnp.exp(m_old - m_new)
        p = jnp.exp(s - m_new)
        l_scr[...] = alpha * l_scr[...] + jnp.sum(p, axis=1, keepdims=True)
        acc_scr[...] = alpha * acc_scr[...] + pv_fn(p.astype(BF16))
        m_scr[...] = m_new

    bias_b = bias_scr[...].astype(BF16)
    blk0 = st * (n_pages * per_page)
    key_blk = blk0 + _div(_iota((n_blk_pad, n_pages * PAGE_SIZE), 1), SLC_BLOCK)
    expand = jnp.where(_iota((n_blk_pad, n_pages * PAGE_SIZE), 0) == key_blk, 1.0, 0.0).astype(BF16)
    keys_t = jnp.concatenate([pages[j][0].astype(BF16) for j in range(n_pages)], axis=1)
    vals_t = jnp.concatenate([pages[j][1].astype(BF16) for j in range(n_pages)], axis=1)
    s = (jnp.dot(qbdt, keys_t, preferred_element_type=F32) + jnp.dot(bias_b, expand, preferred_element_type=F32))
    update(s, lambda p: lax.dot_general(p, vals_t, NT_DIMS, preferred_element_type=F32))

    @pl.when(st == n_steps - 1)
    def _():
        def pad_rows(a):
            return jnp.concatenate([a, jnp.zeros((LANE - a.shape[0], a.shape[1]), a.dtype)], axis=0)

        key_i = _iota((1, LANE), 1)
        new = pad_rows(newkv_ref[0])
        s_n = lax.dot_general(qbdt, new[:, 2 * KV_DIM:3 * KV_DIM].astype(BF16), NT_DIMS, preferred_element_type=F32)
        s_n = s_n + bias_scr[:, n_past_blk:n_past_blk + 1]
        s_n = jnp.where((key_i <= tok_c) & (key_i < dec_len), s_n, NEG_INF)
        v_n = new[:, 3 * KV_DIM:].astype(BF16)
        update(s_n, lambda p: jnp.dot(p, v_n, preferred_element_type=F32))
        o_s = acc_scr[...] / jnp.maximum(l_scr[...], 1e-30)

        w_buf = win_ref.shape[2]
        neww = pad_rows(newwin_ref[0])
        s_w = jnp.concatenate(
            [jnp.dot(qbdt, win_ref[0].astype(BF16), preferred_element_type=F32),
             lax.dot_general(qbdt, neww[:, :KV_DIM].astype(BF16), NT_DIMS, preferred_element_type=F32)], axis=1)
        idx = _iota((1, w_buf + LANE), 1)
        kpos = past_len - w_buf + idx
        dist = past_len + tok_c - kpos
        mask_w = (dist >= 0) & (dist < WINDOW) & (idx < w_buf + dec_len) & (kpos >= 0)
        s_w = jnp.where(mask_w, s_w, NEG_INF)
        e_w = jnp.where(mask_w, jnp.exp(s_w - jnp.max(s_w, axis=1, keepdims=True)), 0.0)
        p_w = (e_w / jnp.maximum(jnp.sum(e_w, axis=1, keepdims=True), 1e-30)).astype(BF16)
        o_w = (lax.dot_general(p_w[:, :w_buf], win_ref[1].astype(BF16), NT_DIMS, preferred_element_type=F32)
               + jnp.dot(p_w[:, w_buf:], neww[:, KV_DIM:].astype(BF16), preferred_element_type=F32))
        gate = gate_ref[0]
        o_ref[0] = gate[:, 0:1] * oc_scr[...] + gate[:, 1:2] * o_s + gate[:, 2:3] * o_w


def _attn_sample(cache_t, page_table, page_index, win_t, win_index, qbd, qbdt, kc_flat, vc_flat, newkv8, newwin8,
                 gates, *, n_pages, past_len, dec_len):
    n_seq, pages_per_seq = page_table.shape
    n_steps = pages_per_seq // n_pages
    n_cmp = kc_flat.shape[1]
    n_blk_pad = -(-(past_len // SLC_BLOCK + 1) // LANE) * LANE
    w_buf = win_t.shape[2]

    def page_spec(j):
        return pl.BlockSpec((2, KV_DIM, PAGE_SIZE), lambda b, s, pt: (page_index(pt[b, s * n_pages + j]), 0, 0))

    per_seq = lambda shape: pl.BlockSpec((1,) + shape, lambda b, s, pt: (b,) + (0,) * len(shape))
    grid_spec = pltpu.PrefetchScalarGridSpec(
        num_scalar_prefetch=1,
        grid=(n_seq, n_steps),
        in_specs=[page_spec(j) for j in range(n_pages)] + [
            per_seq((KV_DIM, LANE)), per_seq((LANE, KV_DIM)), per_seq((n_cmp, KV_DIM)), per_seq((n_cmp, KV_DIM)),
            per_seq((SUBLANE, 4 * KV_DIM)),
            pl.BlockSpec((2, KV_DIM, w_buf), lambda b, s, pt: (win_index(b), 0, 0)),
            per_seq((SUBLANE, 2 * KV_DIM)), per_seq((LANE, LANE))],
        out_specs=per_seq((LANE, KV_DIM)),
        scratch_shapes=[pltpu.VMEM((n_blk_pad, LANE), F32), pltpu.VMEM((n_blk_pad, LANE), F32),
                        pltpu.VMEM((LANE, n_blk_pad), F32),
                        pltpu.VMEM((LANE, 1), F32), pltpu.VMEM((LANE, 1), F32),
                        pltpu.VMEM((LANE, KV_DIM), F32), pltpu.VMEM((LANE, KV_DIM), F32)],
    )
    return pl.pallas_call(
        functools.partial(_attn_sample_kernel, n_pages=n_pages, past_len=past_len, dec_len=dec_len,
                          n_blk_pad=n_blk_pad),
        grid_spec=grid_spec,
        out_shape=jax.ShapeDtypeStruct((n_seq, LANE, KV_DIM), F32),
        compiler_params=_params(2),
        name="attention_sample",
    )(page_table, *([cache_t] * n_pages), qbd, qbdt, kc_flat, vc_flat, newkv8, win_t, newwin8, gates)


def _permute_w_in(w):
    lead = w.shape[:-1]
    c = np.cumsum([0, 4 * CONV_DIM, ATT_DIM, 4 * KV_DIM, 2 * KV_DIM, 3 * N_HEADS, ATT_DIM, 2 * D_MODEL])
    wa, wq, wkv, wwin, wgb, wzb, wgm = (w[..., c[i]:c[i + 1]] for i in range(7))
    wgb = jnp.pad(wgb, ((0, 0),) * len(lead) + ((0, LANE - 3 * N_HEADS),))
    return jnp.concatenate([wa, wq, wkv, wwin, wzb, wgm, wgb], axis=-1).astype(BF16)


def _rope_tables(pos):
    half = HEAD_DIM // 2
    inv = jnp.power(ROPE_THETA, -jnp.arange(half, dtype=F32) / half)
    ang = pos.astype(F32)[:, None] * inv[None, :]
    cos, sin = jnp.cos(ang), jnp.sin(ang)
    reps = LANE // HEAD_DIM
    return jnp.tile(jnp.concatenate([cos, cos], axis=1), (1, reps)), jnp.tile(jnp.concatenate([-sin, sin], axis=1), (1, reps))


def _compress_weights(pos_k, w1_k, w2_k, pos_v, w1_v, w2_v, grouped):
    depth = pos_k.shape[0]
    flat = CMP_STRIDE * HEAD_DIM
    pos2 = jnp.stack([jnp.pad(p.reshape(depth, 2, flat), ((0, 0), (0, SUBLANE - 2), (0, 0))) for p in (pos_k, pos_v)],
                     axis=1)
    w1 = jnp.stack([w.reshape(depth, 2, flat, CMP_HID) for w in (w1_k, w1_v)], axis=1).astype(BF16)
    if grouped:
        w2 = jnp.stack([jnp.pad(w2_k, ((0, 0), (0, 0), (0, CMP_HID - HEAD_DIM)))[:, None],
                        jnp.pad(w2_v.transpose(0, 2, 1), ((0, 0), (0, CMP_HID - HEAD_DIM), (0, 0)))[:, None]], axis=1)
    else:
        w2 = jnp.stack([jnp.stack([jnp.pad(w, ((0, 0), (0, 0), (g * HEAD_DIM, KV_DIM - (g + 1) * HEAD_DIM)))
                                   for g in range(N_KV)], axis=1) for w in (w2_k, w2_v)], axis=1)
    return pos2, w1, w2.astype(BF16)


def kernel(x_prompt, x_sample, cache_kv, state_win, state_conv, page_table, p_prompt, p_sample, ln_g, w_in, conv_w,
           cmp_pos_k, cmp_w1_k, cmp_w2_k, cmp_pos_v, cmp_w1_v, cmp_w2_v, w_out_a, w_out_b, w_o, w_ple, w_ple_gate,
           final_g):
    depth = ln_g.shape[0]
    n_b, seq_len, _ = x_prompt.shape
    n_dec, dec_len, _ = x_sample.shape
    pages_per_seq = page_table.shape[1]
    past_len = pages_per_seq * PAGE_SIZE
    n_pool = cache_kv.shape[1]
    w_buf = state_win.shape[2]

    tm_p = min(256, seq_len)
    tq = min(256, seq_len)
    tm_merge = min(512, seq_len)
    pages_p = min(16, seq_len // PAGE_SIZE)
    pages_s = min(32, pages_per_seq)
    rows_s = n_dec * dec_len
    carry_p = SUBLANE
    carry_s = (CONV_W - 1) * n_dec
    assert seq_len % PAGE_SIZE == 0 and seq_len // SLC_BLOCK <= MAX_BLOCKS and seq_len >= WINDOW + tq
    assert CONV_W - 1 <= dec_len <= SUBLANE and rows_s % SUBLANE == 0
    assert w_buf == min(WINDOW, past_len) and N_HEADS * SUBLANE == LANE

    cos_p, sin_p = _rope_tables(jnp.arange(seq_len, dtype=jnp.int32))
    pos_s = past_len + jnp.repeat(jnp.arange(dec_len, dtype=jnp.int32), n_dec)
    cos_s, sin_s = _rope_tables(pos_s)
    prompt_pages = jnp.arange(n_b * (seq_len // PAGE_SIZE), dtype=jnp.int32).reshape(n_b, seq_len // PAGE_SIZE)
    eye_g = jnp.eye(N_KV, dtype=F32)
    tpad = SUBLANE - dec_len
    cache_t = cache_kv.transpose(0, 1, 3, 4, 5, 2).reshape(depth * n_pool * 4, KV_DIM, PAGE_SIZE)
    win_t = state_win.transpose(0, 1, 3, 4, 5, 2).reshape(depth * n_dec * 2, KV_DIM, w_buf)

    xp = x_prompt.reshape(n_b * seq_len, D_MODEL)
    xs = x_sample.transpose(1, 0, 2).reshape(rows_s, D_MODEL)
    kv_s, win_new_s, conv_p, conv_s = [], [], [], []
    carried = None
    w_perm = _permute_w_in(w_in)
    g_rows = ln_g.reshape(depth, 1, D_MODEL)
    cw8 = jnp.pad(conv_w, ((0, 0), (0, SUBLANE - CONV_W), (0, 0)))
    merge_w = (w_out_a.astype(BF16), w_out_b.astype(BF16), w_o.astype(BF16), w_ple.astype(BF16),
               w_ple_gate.astype(BF16), final_g.reshape(1, D_MODEL))
    cmp_args = (cmp_pos_k, cmp_w1_k, cmp_w2_k, cmp_pos_v, cmp_w1_v, cmp_w2_v)
    cmp_w_prompt = _compress_weights(*cmp_args, True)
    cmp_w_sample = _compress_weights(*cmp_args, False)
    zero_prefix = jnp.zeros((n_b, carry_p, CONV_DIM), F32)
    p_sample_rows = p_sample.transpose(0, 2, 1, 3).reshape(depth, rows_s, PLE_DIM)
    for i in range(depth):
        last = i == depth - 1

        (a_act, newkv_all, win_all, zb_act, gmix, tail, q_t, ks_ext, vs_t, kw_pad, vw_pad, gb_t, cmp_rows) = _project(
            xp, g_rows, w_perm, cos_p, sin_p, zero_prefix, cw8, carried, _window_pads(n_b, seq_len), layer=i,
            n_seq=n_b, rows_per_seq=seq_len, tm=tm_p, stride=1, carry=carry_p, transposed=True, pos_per_seq=True)
        carried = (newkv_all, win_all)
        kc_ext, vc_t = _compress(cmp_rows, prompt_pages, lambda p: p, *cmp_w_prompt, layer=i, n_pages=pages_p,
                                 prompt=True)
        y_act = _attn_prompt(q_t, gb_t, zb_act.reshape(n_b, seq_len, ATT_DIM), kc_ext, vc_t, ks_ext, vs_t, kw_pad,
                             vw_pad, tq=tq)
        xp = _merge(xp, a_act, y_act.reshape(n_b * seq_len, ATT_DIM), None, gmix,
                    p_prompt.reshape(depth, n_b * seq_len, PLE_DIM), *merge_w, layer=i, tm=tm_merge, final=last)
        conv_p.append(tail[:, carry_p - (CONV_W - 1):])

        prefix_s = state_conv[i].transpose(1, 0, 2).reshape(1, carry_s, CONV_DIM)
        (a_act, newkv, win, zb_act, gmix, tail, qp, gb) = _project(
            xs, g_rows, w_perm, cos_s, sin_s, prefix_s, cw8, None, None, layer=i,
            n_seq=1, rows_per_seq=rows_s, tm=rows_s, stride=n_dec, carry=carry_s, transposed=False, pos_per_seq=False)
        base = i * n_pool
        kc_flat, vc_flat = _compress(cache_t, page_table, lambda p: (base + p) * 2, *cmp_w_sample, layer=i,
                                     n_pages=pages_s, prompt=False)
        q5 = jnp.pad(qp.reshape(dec_len, n_dec, N_KV, GROUP, HEAD_DIM).astype(F32),
                     ((0, tpad), (0, 0), (0, 0), (0, 0), (0, 0)))
        qbd = q5.transpose(1, 4, 2, 3, 0)[:, None] * eye_g[None, :, None, :, None, None]
        qbd = qbd.reshape(n_dec, KV_DIM, LANE).astype(BF16)
        qbdt = q5.transpose(1, 2, 3, 0, 4)[:, :, :, :, None] * eye_g[None, :, None, None, :, None]
        qbdt = qbdt.reshape(n_dec, LANE, KV_DIM).astype(BF16)
        gates = gb[:, :3 * N_HEADS].reshape(dec_len, n_dec, N_KV, GROUP, 3)
        gates = jnp.pad(gates.transpose(1, 2, 3, 0, 4), ((0, 0), (0, 0), (0, 0), (0, tpad), (0, LANE - 3)))
        gates = gates.reshape(n_dec, LANE, LANE)
        newkv_b = newkv.reshape(dec_len, n_dec, 4 * KV_DIM).transpose(1, 0, 2)
        newwin_b = win.reshape(dec_len, n_dec, 2 * KV_DIM).transpose(1, 0, 2)
        pad_rows = lambda a: jnp.pad(a, ((0, 0), (0, tpad), (0, 0)))
        o_all = _attn_sample(cache_t, page_table, lambda p: (base + p) * 2 + 1, win_t, lambda b: i * n_dec + b, qbd,
                             qbdt, kc_flat, vc_flat, pad_rows(newkv_b), pad_rows(newwin_b), gates, n_pages=pages_s,
                             past_len=past_len, dec_len=dec_len)
        o6 = o_all.reshape(n_dec, N_KV, GROUP, SUBLANE, N_KV, HEAD_DIM)
        yb = jnp.stack([o6[:, g, :, :, g] for g in range(N_KV)], axis=1)
        yb = yb.transpose(3, 0, 1, 2, 4)[:dec_len].reshape(rows_s, ATT_DIM)
        xs = _merge(xs, a_act, yb, zb_act, gmix, p_sample_rows, *merge_w, layer=i, tm=rows_s, final=last)
        kv_s.append(newkv_b.reshape(n_dec, dec_len, 4, N_KV, HEAD_DIM))
        win_new_s.append(newwin_b.reshape(n_dec, dec_len, 2, N_KV, HEAD_DIM))
        conv_s.append(tail.reshape(CONV_W - 1, n_dec, CONV_DIM).transpose(1, 0, 2))

    y_prompt = xp.reshape(n_b, seq_len, D_MODEL)
    y_sample = xs.reshape(dec_len, n_dec, D_MODEL).transpose(1, 0, 2)
    newkv_all, win_all = carried
    kv_p = newkv_all.reshape(depth, n_b, 4, N_KV, HEAD_DIM, seq_len).transpose(0, 1, 5, 2, 3, 4)
    win_p = win_all.reshape(depth, n_b, 2, N_KV, HEAD_DIM, min(WINDOW, seq_len)).transpose(0, 1, 5, 2, 3, 4)
    win_s = jnp.concatenate([state_win, jnp.stack(win_new_s)], axis=2)[:, :, -w_buf:]
    return (y_prompt, y_sample, kv_p, jnp.stack(kv_s), win_p, win_s, jnp.stack(conv_p), jnp.stack(conv_s))
```

```python
import functools

import numpy as np
import jax
import jax.numpy as jnp
from jax import lax
from jax.experimental import pallas as pl
from jax.experimental.pallas import tpu as pltpu

D_MODEL = 1024
CONV_DIM = D_MODEL
CONV_W = 3
N_HEADS = 16
HEAD_DIM = 64
N_KV = 4
GROUP = N_HEADS // N_KV
ATT_DIM = N_HEADS * HEAD_DIM
KV_DIM = N_KV * HEAD_DIM
PAGE_SIZE = 128
CMP_STRIDE = 16
CMP_BLOCK = 2 * CMP_STRIDE
CMP_HID = 4 * HEAD_DIM
SLC_BLOCK = 64
TOP_N = 16
WINDOW = 512
PLE_DIM = 256
ROPE_THETA = 10000.0
RMS_EPS = 1e-6
NEG_INF = -1e30
FORCE_SCORE = 1e6

LANE = 128
SUBLANE = 8
VMEM_LIMIT = 56 * 1024 * 1024
KV_SLABS = KV_DIM // LANE
CMP_DEPTH = 2 * LANE
CHAIN_COLS = 4 * LANE

OFF_A = 0
OFF_Q = 4 * CONV_DIM
OFF_KV = OFF_Q + ATT_DIM
OFF_WIN = OFF_KV + 4 * KV_DIM
OFF_ZB = OFF_WIN + 2 * KV_DIM
OFF_GM = OFF_ZB + ATT_DIM
OFF_GB = OFF_GM + 2 * D_MODEL
N_PROJ = OFF_GB + LANE
MAX_BLOCKS = LANE - HEAD_DIM

F32 = jnp.float32
BF16 = jnp.bfloat16
NT_DIMS = (((1,), (1,)), ((), ()))


def _params(n_axes):
    return pltpu.CompilerParams(dimension_semantics=("arbitrary",) * n_axes, vmem_limit_bytes=VMEM_LIMIT)


def _const_spec(shape):
    zeros = (0,) * len(shape)
    return pl.BlockSpec(shape, lambda *_: zeros, pipeline_mode=pl.Buffered(1))


def _layer_spec(shape, layer):
    index = (layer,) + (0,) * len(shape)
    return pl.BlockSpec((None,) + tuple(shape), lambda *_: index, pipeline_mode=pl.Buffered(1))


def _sigmoid(x):
    return 1.0 / (1.0 + jnp.exp(-x))


def _div(x, d):
    assert d & (d - 1) == 0
    return jnp.right_shift(x, d.bit_length() - 1)


def _mod(x, d):
    assert d & (d - 1) == 0
    return jnp.bitwise_and(x, d - 1)


def _iota(shape, dim):
    return lax.broadcasted_iota(jnp.int32, shape, dim)


def _rope_slab(x, cos, sin, lane):
    swapped = jnp.where(_mod(lane, HEAD_DIM) < HEAD_DIM // 2, pltpu.roll(x, LANE - HEAD_DIM // 2, 1),
                        pltpu.roll(x, HEAD_DIM // 2, 1))
    return x * cos + swapped * sin


def _overlap_t(n_blk, n_cmp):
    bj = _iota((n_blk, n_cmp), 0)
    bm = _iota((n_blk, n_cmp), 1)
    ratio = SLC_BLOCK // CMP_STRIDE
    return jnp.where((bm >= ratio * bj) & (bm <= ratio * bj + ratio) & (bm >= 1), 1.0, 0.0).astype(BF16)


def _proj_kernel(x_ref, g_ref, w_ref, cos_ref, sin_ref, pre_ref, cw_ref, *refs, tm, stride, carry, transposed,
                 n_carried):
    a_ref, kv_ref, win_ref, zb_ref, gm_ref, tail_ref, *rest = refs[n_carried:]
    if transposed:
        qt_ref, ks_ref, vst_ref, kw_ref, vwt_ref, gbt_ref, cmp_ref, u_scr = rest
    else:
        q_ref, gb_ref, u_scr = rest
    t = pl.program_id(1)
    x = x_ref[...]
    ms = jnp.mean(x * x, axis=-1, keepdims=True)
    h = (x * lax.rsqrt(ms + RMS_EPS) * g_ref[...]).astype(BF16)

    def seg(off, width):
        return jnp.dot(h, w_ref[:, off:off + width], preferred_element_type=F32)

    @pl.when(t == 0)
    def _():
        u_scr[0:carry, :] = pre_ref[0]

    u = seg(OFF_A + CONV_DIM, CONV_DIM) * seg(OFF_A + 2 * CONV_DIM, CONV_DIM)
    u_scr[carry:carry + tm, :] = u
    cw = cw_ref[...]
    conv = cw[0:1] * u_scr[carry - 2 * stride:carry - 2 * stride + tm, :]
    conv = conv + cw[1:2] * u_scr[carry - stride:carry - stride + tm, :]
    conv = conv + cw[2:3] * u
    ba = seg(OFF_A, CONV_DIM)
    za = seg(OFF_A + 3 * CONV_DIM, CONV_DIM)
    a_ref[...] = ((za * _sigmoid(za)) * (ba * conv)).astype(BF16)
    tail = u_scr[tm:tm + carry, :]
    tail_ref[0] = tail
    u_scr[0:carry, :] = tail

    lane = _iota((tm, LANE), 1)
    cos = cos_ref[...]
    sin = sin_ref[...]
    scale = HEAD_DIM ** -0.5 * (float(np.log2(np.e)) if transposed else 1.0)

    zq = seg(OFF_Q, ATT_DIM)
    for s in range(ATT_DIM // LANE):
        qs = _rope_slab(zq[:, s * LANE:(s + 1) * LANE], cos, sin, lane) * scale
        if transposed:
            qst = qs.T.astype(BF16)
            qt_ref[0, 2 * s] = qst[:HEAD_DIM]
            qt_ref[0, 2 * s + 1] = qst[HEAD_DIM:]
        else:
            q_ref[:, s * LANE:(s + 1) * LANE] = qs.astype(BF16)

    zkv = seg(OFF_KV, 4 * KV_DIM)
    zwin = seg(OFF_WIN, 2 * KV_DIM)
    kv_slabs, kv_t = [], []
    for s in range(4 * KV_SLABS):
        v = zkv[:, s * LANE:(s + 1) * LANE]
        if (s // KV_SLABS) % 2 == 0:
            v = _rope_slab(v, cos, sin, lane)
        if transposed:
            kv_t.append(v.T)
            kv_ref[s * LANE:(s + 1) * LANE, :] = kv_t[-1]
        else:
            kv_ref[:, s * LANE:(s + 1) * LANE] = v
        kv_slabs.append(v)
    win_slabs, win_t = [], []
    for s in range(2 * KV_SLABS):
        v = zwin[:, s * LANE:(s + 1) * LANE]
        if s // KV_SLABS == 0:
            v = _rope_slab(v, cos, sin, lane)
        if transposed:
            win_t.append(v.T)
            win_ref[s * LANE:(s + 1) * LANE, :] = win_t[-1]
        else:
            win_ref[:, s * LANE:(s + 1) * LANE] = v
        win_slabs.append(v)

    zb = seg(OFF_ZB, ATT_DIM)
    zb_ref[...] = zb * _sigmoid(zb)
    gb = _sigmoid(seg(OFF_GB, LANE))

    def merge_gates():
        gm_ref[...] = _sigmoid(seg(OFF_GM, 2 * D_MODEL))

    if not transposed:
        gb_ref[...] = gb
        merge_gates()
        return

    row = _iota((tm, LANE), 0)
    onehot = jnp.where(lane - HEAD_DIM == _div(t * tm + row, SLC_BLOCK), 1.0, 0.0)
    low = lane < HEAD_DIM
    ones_row = jnp.where(_iota((HEAD_DIM, tm), 0) == 0, 1.0, 0.0).astype(BF16)
    vs_t = [kv_t[3 * KV_SLABS + s].astype(BF16) for s in range(KV_SLABS)]
    vw_t = [win_t[KV_SLABS + s].astype(BF16) for s in range(KV_SLABS)]
    for s in range(2 * KV_SLABS):
        cmp_ref[s] = kv_slabs[s]
    for g in range(N_KV):
        s, hi = divmod(g, 2)

        def half(v):
            return v if hi == 0 else pltpu.roll(v, HEAD_DIM, 1)

        ks_ref[0, g] = jnp.where(low, half(kv_slabs[2 * KV_SLABS + s]), onehot).astype(BF16)
        kw_ref[0, g] = jnp.where(low, half(win_slabs[s]), 0.0).astype(BF16)
        vst_ref[0, g] = jnp.concatenate([vs_t[s][hi * HEAD_DIM:(hi + 1) * HEAD_DIM], ones_row], axis=0)
        vwt_ref[0, g] = jnp.concatenate([vw_t[s][hi * HEAD_DIM:(hi + 1) * HEAD_DIM], ones_row], axis=0)
    gbt_ref[0] = gb.T
    merge_gates()


def _window_pads_kernel(kw_ref, vwt_ref):
    flag = jnp.where(_iota((WINDOW, LANE), 1) == HEAD_DIM, 1.0, 0.0).astype(BF16)
    for g in range(N_KV):
        kw_ref[0, g] = flag
        vwt_ref[0, g] = jnp.zeros((LANE, WINDOW), BF16)


def _window_pads(n_seq, rows_per_seq):
    padded = rows_per_seq + WINDOW
    return pl.pallas_call(
        _window_pads_kernel,
        grid=(n_seq,),
        out_specs=[pl.BlockSpec((1, N_KV, WINDOW, LANE), lambda b: (b, 0, 0, 0)),
                   pl.BlockSpec((1, N_KV, LANE, WINDOW), lambda b: (b, 0, 0, 0))],
        out_shape=[jax.ShapeDtypeStruct((n_seq, N_KV, padded, LANE), BF16),
                   jax.ShapeDtypeStruct((n_seq, N_KV, LANE, padded), BF16)],
        compiler_params=_params(1),
        name="window_pads",
    )()


def _project(x2d, ln_g, w_perm, cos, sin, prefix, conv_w8, carried, window_pads, *, layer, n_seq, rows_per_seq, tm,
             stride, carry, transposed, pos_per_seq):
    nt = rows_per_seq // tm
    n_rows = n_seq * rows_per_seq
    depth = w_perm.shape[0]
    row_map = lambda b, t: (b * nt + t, 0)
    pos_map = (lambda b, t: (t, 0)) if pos_per_seq else row_map
    in_specs = [
        pl.BlockSpec((tm, D_MODEL), row_map),
        _layer_spec((1, D_MODEL), layer),
        _layer_spec((D_MODEL, N_PROJ), layer),
        pl.BlockSpec((tm, LANE), pos_map),
        pl.BlockSpec((tm, LANE), pos_map),
        pl.BlockSpec((1, carry, CONV_DIM), lambda b, t: (b, 0, 0)),
        _layer_spec((SUBLANE, CONV_DIM), layer),
    ]
    rows = lambda width, dtype: (jax.ShapeDtypeStruct((n_rows, width), dtype), pl.BlockSpec((tm, width), row_map))
    if transposed:
        keep = min(WINDOW, rows_per_seq)
        assert keep % tm == 0
        first_kept = (rows_per_seq - keep) // tm
        kv_out = (jax.ShapeDtypeStruct((depth, n_seq, 4 * KV_DIM, rows_per_seq), F32),
                  pl.BlockSpec((None, None, 4 * KV_DIM, tm), lambda b, t: (layer, b, 0, t)))
        win_out = (jax.ShapeDtypeStruct((depth, n_seq, 2 * KV_DIM, keep), F32),
                   pl.BlockSpec((None, None, 2 * KV_DIM, tm),
                                lambda b, t: (layer, b, 0, jnp.maximum(t - first_kept, 0))))
    else:
        kv_out, win_out = rows(4 * KV_DIM, F32), rows(2 * KV_DIM, F32)
    outs = [
        rows(CONV_DIM, BF16),
        kv_out,
        win_out,
        rows(ATT_DIM, F32),
        rows(2 * D_MODEL, F32),
        (jax.ShapeDtypeStruct((n_seq, carry, CONV_DIM), F32),
         pl.BlockSpec((1, carry, CONV_DIM), lambda b, t: (b, 0, 0))),
    ]
    if transposed:
        tok_sub = (jax.ShapeDtypeStruct((n_seq, N_KV, rows_per_seq, LANE), BF16),
                   pl.BlockSpec((1, N_KV, tm, LANE), lambda b, t: (b, 0, t, 0)))
        tok_lane = lambda n, rows_, dtype: (jax.ShapeDtypeStruct((n_seq, n, rows_, rows_per_seq), dtype),
                                            pl.BlockSpec((1, n, rows_, tm), lambda b, t: (b, 0, 0, t)))
        assert WINDOW % tm == 0
        padded = rows_per_seq + WINDOW
        win_sub = (jax.ShapeDtypeStruct((n_seq, N_KV, padded, LANE), BF16),
                   pl.BlockSpec((1, N_KV, tm, LANE), lambda b, t: (b, 0, t + WINDOW // tm, 0)))
        win_lane = (jax.ShapeDtypeStruct((n_seq, N_KV, LANE, padded), BF16),
                    pl.BlockSpec((1, N_KV, LANE, tm), lambda b, t: (b, 0, 0, t + WINDOW // tm)))
        outs += [tok_lane(N_HEADS, HEAD_DIM, BF16),
                 tok_sub, tok_lane(N_KV, LANE, BF16),
                 win_sub, win_lane,
                 (jax.ShapeDtypeStruct((n_seq, LANE, rows_per_seq), F32),
                  pl.BlockSpec((1, LANE, tm), lambda b, t: (b, 0, t))),
                 (jax.ShapeDtypeStruct((2 * KV_SLABS, n_rows, LANE), F32),
                  pl.BlockSpec((2 * KV_SLABS, tm, LANE), lambda b, t: (0, b * nt + t, 0)))]
    else:
        outs += [rows(ATT_DIM, BF16), rows(LANE, F32)]
    args = [x2d, ln_g, w_perm, cos, sin, prefix, conv_w8]
    aliases = {}
    if carried is not None:
        assert transposed
        aliases = {len(args): 1, len(args) + 1: 2}
        args += list(carried)
    if window_pads is not None:
        aliases.update({len(args): 9, len(args) + 1: 10})
        args += list(window_pads)
    in_specs += [pl.BlockSpec(memory_space=pl.ANY)] * len(aliases)
    return pl.pallas_call(
        functools.partial(_proj_kernel, tm=tm, stride=stride, carry=carry, transposed=transposed,
                          n_carried=len(aliases)),
        grid=(n_seq, nt),
        in_specs=in_specs,
        out_specs=[o[1] for o in outs],
        out_shape=[o[0] for o in outs],
        scratch_shapes=[pltpu.VMEM((carry + tm, CONV_DIM), F32)],
        input_output_aliases=aliases,
        compiler_params=_params(2),
        name="project_prompt" if transposed else "project_sample",
    )(*args)


def _merge_kernel(x_ref, a_ref, gm_ref, p_ref, woa_ref, wob_ref, wo_ref, wple_ref, wpg_ref, fg_ref, y_ref, *rest,
                  y_needs_gate, final):
    br_a = jnp.dot(a_ref[...], woa_ref[...], preferred_element_type=F32)
    if y_needs_gate:
        zb_ref, o_ref = rest
        y = (zb_ref[...] * y_ref[...]).astype(BF16)
    else:
        (o_ref,) = rest
        y = y_ref[...]
    br_b = jnp.dot(y, wob_ref[...], preferred_element_type=F32)
    gm = gm_ref[...]
    mix = gm[:, :D_MODEL] * br_a + gm[:, D_MODEL:] * br_b
    x1 = x_ref[...] + jnp.dot(mix.astype(BF16), wo_ref[...], preferred_element_type=F32)
    gate = _sigmoid(jnp.dot(x1.astype(BF16), wpg_ref[...], preferred_element_type=F32))
    x2 = x1 + gate * jnp.dot(p_ref[...].astype(BF16), wple_ref[...], preferred_element_type=F32)
    if final:
        ms = jnp.mean(x2 * x2, axis=-1, keepdims=True)
        x2 = x2 * lax.rsqrt(ms + RMS_EPS) * fg_ref[...]
    o_ref[...] = x2


def _merge(x2d, a_act, y, zb_act, gmix, p2d, w_out_a, w_out_b, w_o, w_ple, w_pg, final_g, *, layer, tm, final):
    n_rows = x2d.shape[0]
    row = lambda width: pl.BlockSpec((tm, width), lambda t: (t, 0))
    y_args = (y,) if zb_act is None else (y, zb_act)
    return pl.pallas_call(
        functools.partial(_merge_kernel, y_needs_gate=zb_act is not None, final=final),
        grid=(n_rows // tm,),
        in_specs=[row(D_MODEL), row(CONV_DIM), row(2 * D_MODEL),
                  pl.BlockSpec((None, tm, PLE_DIM), lambda t: (layer, t, 0)),
                  _layer_spec((CONV_DIM, D_MODEL), layer), _layer_spec((ATT_DIM, D_MODEL), layer),
                  _layer_spec((D_MODEL, D_MODEL), layer), _layer_spec((PLE_DIM, D_MODEL), layer),
                  _layer_spec((D_MODEL, D_MODEL), layer), _const_spec((1, D_MODEL))] + [row(ATT_DIM)] * len(y_args),
        out_specs=row(D_MODEL),
        out_shape=jax.ShapeDtypeStruct((n_rows, D_MODEL), F32),
        compiler_params=_params(1),
        name="merge_prompt" if zb_act is None else "merge_sample",
    )(x2d, a_act, gmix, p2d, w_out_a, w_out_b, w_o, w_ple, w_pg, final_g, *y_args)


def _compress_kernel(pt_ref, *refs, n_pages, prompt):
    pages = refs[:n_pages]
    pos_ref, w1_ref, w2_ref, ko_ref, vo_ref, rows_scr, x_scr, carry_scr = refs[n_pages:]
    st = pl.program_id(1)
    per_page = PAGE_SIZE // CMP_STRIDE
    n = n_pages * per_page
    low = _iota((n, LANE), 1) < HEAD_DIM

    @pl.when(st == 0)
    def _():
        carry_scr[...] = jnp.zeros_like(carry_scr)

    row0 = _iota((n, CMP_HID), 0) == 0
    def relayout(slot):
        for j in range(n_pages):
            for v in range(KV_SLABS):
                if prompt:
                    piece = pages[j][slot * KV_SLABS + v]
                else:
                    piece = pages[j][slot, v * LANE:(v + 1) * LANE, :].T
                rows_scr[slot, v, j * PAGE_SIZE:(j + 1) * PAGE_SIZE, :] = piece
        for pair in range(CMP_STRIDE // 2):
            for v in range(KV_SLABS):
                a = rows_scr[slot, v, pl.ds(2 * pair, n, stride=CMP_STRIDE), :]
                b = rows_scr[slot, v, pl.ds(2 * pair + 1, n, stride=CMP_STRIDE), :]
                x_scr[slot, 2 * v * n:(2 * v + 1) * n, pair * LANE:(pair + 1) * LANE] = jnp.where(
                    low, a, pltpu.roll(b, HEAD_DIM, 1))
                x_scr[slot, (2 * v + 1) * n:(2 * v + 2) * n, pair * LANE:(pair + 1) * LANE] = jnp.where(
                    low, pltpu.roll(a, HEAD_DIM, 1), b)
    for slot, out_ref in enumerate((ko_ref, vo_ref)):
        relayout(slot)
        pos = pos_ref[slot]
        f = s = None
        for k0 in range(0, x_scr.shape[2], CMP_DEPTH):
            ks = slice(k0, k0 + CMP_DEPTH)
            xs = x_scr[slot, :, ks]
            f_part = jnp.dot((xs + pos[0:1, ks]).astype(BF16), w1_ref[slot, 0, ks, :], preferred_element_type=F32)
            s_part = jnp.dot((xs + pos[1:2, ks]).astype(BF16), w1_ref[slot, 1, ks, :], preferred_element_type=F32)
            f = f_part if f is None else f + f_part
            s = s_part if s is None else s + s_part
        acc = None
        for g in range(N_KV):
            fg = f[g * n:(g + 1) * n]
            prev = carry_scr[slot, g, SUBLANE - 1:SUBLANE, :]
            shifted = jnp.where(row0, prev, pltpu.roll(fg, 1, 0))
            carry_scr[slot, g] = fg[n - SUBLANE:n]
            pre = shifted + s[g * n:(g + 1) * n]
            hid = (pre * _sigmoid(pre)).astype(BF16)
            if prompt and slot == 0:
                out_ref[0, g] = jnp.dot(hid, w2_ref[0, 0, :, :LANE], preferred_element_type=F32).astype(BF16)
            elif prompt:
                out_ref[0, g] = lax.dot_general(w2_ref[1, 0, :LANE, :], hid, NT_DIMS,
                                                preferred_element_type=F32).astype(BF16)
            else:
                part = jnp.dot(hid, w2_ref[slot, g], preferred_element_type=F32)
                acc = part if acc is None else acc + part
        if not prompt:
            out_ref[0] = acc.astype(BF16)


def _compress(pages_arr, page_table, page_index, pos2, w1, w2, *, layer, n_pages, prompt):
    n_seq, pages_per_seq = page_table.shape
    n_steps = pages_per_seq // n_pages
    n = n_pages * (PAGE_SIZE // CMP_STRIDE)
    n_chunk = pages_per_seq * (PAGE_SIZE // CMP_STRIDE)
    flat = CMP_STRIDE * HEAD_DIM

    def page_spec(j):
        if prompt:
            return pl.BlockSpec((2 * KV_SLABS, PAGE_SIZE, LANE),
                                lambda b, s, pt: (0, page_index(pt[b, s * n_pages + j]), 0))
        return pl.BlockSpec((2, KV_DIM, PAGE_SIZE), lambda b, s, pt: (page_index(pt[b, s * n_pages + j]), 0, 0))

    if prompt:
        out_sds = [jax.ShapeDtypeStruct((n_seq, N_KV, n_chunk, LANE), BF16),
                   jax.ShapeDtypeStruct((n_seq, N_KV, LANE, n_chunk), BF16)]
        out_specs = [pl.BlockSpec((1, N_KV, n, LANE), lambda b, s, pt: (b, 0, s, 0)),
                     pl.BlockSpec((1, N_KV, LANE, n), lambda b, s, pt: (b, 0, 0, s))]
    else:
        out_sds = [jax.ShapeDtypeStruct((n_seq, n_chunk, KV_DIM), BF16)] * 2
        out_specs = [pl.BlockSpec((1, n, KV_DIM), lambda b, s, pt: (b, s, 0))] * 2
    grid_spec = pltpu.PrefetchScalarGridSpec(
        num_scalar_prefetch=1,
        grid=(n_seq, n_steps),
        in_specs=[page_spec(j) for j in range(n_pages)] + [_layer_spec(a.shape[1:], layer) for a in (pos2, w1, w2)],
        out_specs=out_specs,
        scratch_shapes=[pltpu.VMEM((2, KV_SLABS, n_pages * PAGE_SIZE, LANE), F32),
                        pltpu.VMEM((2, N_KV * n, flat), F32),
                        pltpu.VMEM((2, N_KV, SUBLANE, CMP_HID), F32)],
    )
    return pl.pallas_call(
        functools.partial(_compress_kernel, n_pages=n_pages, prompt=prompt),
        grid_spec=grid_spec,
        out_shape=out_sds,
        compiler_params=_params(2),
        name="compress_prompt" if prompt else "compress_sample",
    )(page_table, *([pages_arr] * n_pages), pos2, w1, w2)


def _select_bias(score_ref, cnt_ref, n_real, n_live, n_sel):
    score = score_ref[...]
    n_rows = -(-n_real // SUBLANE) * SUBLANE
    sub = _iota((SUBLANE, score.shape[1]), 0)
    cnt_ref[...] = jnp.zeros(score.shape, F32)

    def group(i0):
        cnt = cnt_ref[0:n_rows, :]
        for i in range(i0, min(i0 + SUBLANE, n_real)):
            si = score_ref[i:i + 1, :]
            pieces = []
            for r0 in range(0, n_rows, SUBLANE):
                sv = score[r0:r0 + SUBLANE]
                if r0 > i:
                    pieces.append(jnp.where(si >= sv, 1.0, 0.0))
                elif r0 + SUBLANE - 1 < i:
                    pieces.append(jnp.where(si > sv, 1.0, 0.0))
                else:
                    pieces.append(jnp.where(sub > i - r0, jnp.where(si >= sv, 1.0, 0.0), jnp.where(si > sv, 1.0, 0.0)))
            cnt = cnt + jnp.concatenate(pieces, axis=0)
        cnt_ref[0:n_rows, :] = cnt

    for i0 in range(0, n_real, SUBLANE):
        if n_live is None:
            group(i0)
        else:
            pl.when(i0 < n_live)(functools.partial(group, i0))
    return jnp.where((cnt_ref[...] < n_sel) & (score >= 0.0), 0.0, NEG_INF)


def _attn_prompt_kernel(qt_ref, gbt_ref, zb_ref, tri_ref, kc_ref, vct_ref, ks_ref, vst_ref, kw_ref, vwt_ref, o_ref,
                        score_scr, cnt_scr, m_scr, acc_scr, mx_scr, s_even_scr, s_odd_scr, *, tq, n_cmp, n_blk):
    qi = pl.program_id(2)
    s0 = qi * tq
    rows = GROUP * tq
    cw = min(rows, CHAIN_COLS)
    col_groups = [slice(c, c + cw) for c in range(0, rows, cw)]
    qpos = s0 + _mod(_iota((1, rows), 1), tq)

    def stack_q(lower):
        return jnp.concatenate([jnp.concatenate([qt_ref[0, r], lower], axis=0) for r in range(GROUP)], axis=1)

    q0 = stack_q(jnp.where(_iota((LANE - HEAD_DIM, tq), 0) == 0, NEG_INF, 0.0).astype(BF16))

    s_c = jnp.dot(kc_ref[0, 0], q0, preferred_element_type=F32)

    def window_scores(cs, first, n_keys):
        keys = kw_ref[0, 0, pl.ds(pl.multiple_of(s0 + first, tq), n_keys), :]
        return jnp.dot(keys, q0[:, cs], preferred_element_type=F32)

    w_parts = []
    for cs in col_groups:
        parts = [window_scores(cs, 0, tq) + tri_ref[0, :, cs]]
        if WINDOW > tq:
            parts.append(window_scores(cs, tq, WINDOW - tq))
        parts.append(window_scores(cs, WINDOW, tq) + tri_ref[1, :, cs])
        w_parts.append(parts)

    m_idx = _iota((n_cmp, 1), 0)
    mask_c = (m_idx >= 1) & (CMP_STRIDE * m_idx + CMP_STRIDE - 1 <= qpos)
    s_c = jnp.where(mask_c, s_c, NEG_INF)
    e_c = jnp.where(mask_c, jnp.exp2(s_c - jnp.max(s_c, axis=0, keepdims=True)), 0.0)
    p_c = e_c * (1.0 / jnp.maximum(jnp.sum(e_c, axis=0, keepdims=True), 1e-30))
    o_c = jnp.dot(vct_ref[0, 0], p_c.astype(BF16), preferred_element_type=F32)

    p_sum = p_c[:, 0:tq]
    for r in range(1, GROUP):
        p_sum = p_sum + p_c[:, r * tq:(r + 1) * tq]
    p_hi = p_sum.astype(BF16)
    p_lo = (p_sum - p_hi.astype(F32)).astype(BF16)
    ov_t = _overlap_t(MAX_BLOCKS, n_cmp)
    imp = jnp.dot(ov_t, p_hi, preferred_element_type=F32) + jnp.dot(ov_t, p_lo, preferred_element_type=F32)

    w_out = []
    for cs, parts in zip(col_groups, w_parts):
        m_w = functools.reduce(jnp.maximum, [jnp.max(s, axis=0, keepdims=True) for s in parts])
        p_w = jnp.concatenate([jnp.exp2(s - m_w).astype(BF16) for s in parts], axis=0)
        acc_w = jnp.dot(vwt_ref[0, 0, :, pl.ds(pl.multiple_of(s0, tq), WINDOW + tq)], p_w, preferred_element_type=F32)
        w_out.append(acc_w[:HEAD_DIM] / jnp.maximum(acc_w[HEAD_DIM:HEAD_DIM + 1], 1e-30))
    o_w = jnp.concatenate(w_out, axis=1)

    blk = _iota((MAX_BLOCKS, tq), 0)
    cur = _div(s0 + _iota((MAX_BLOCKS, tq), 1), SLC_BLOCK)
    forced = (blk == 0) | (blk == cur) | (blk == cur - 1)
    score = jnp.where(forced, FORCE_SCORE, imp)
    score_scr[...] = jnp.where(blk <= cur, score, -1.0)
    n_live = jnp.minimum(n_blk, (s0 + tq) // SLC_BLOCK)
    bias_t = _select_bias(score_scr, cnt_scr, n_blk, n_live, min(TOP_N, n_blk))
    qx = stack_q(bias_t.astype(BF16))

    m_scr[...] = jnp.full(m_scr.shape, NEG_INF, F32)
    acc_scr[...] = jnp.zeros(acc_scr.shape, F32)
    s_bufs = (s_even_scr, s_odd_scr)

    def scores(k0, slot, causal):
        keys = ks_ref[0, 0, pl.ds(pl.multiple_of(k0, tq), tq), :]
        prods = [jnp.dot(keys, qx[:, cs], preferred_element_type=F32) for cs in col_groups]
        for cs, s in zip(col_groups, prods):
            if causal:
                s = s + tri_ref[1, :, cs]
            s_bufs[slot][:, cs] = s
            mx_scr[slot, 0:1, cs] = jnp.max(s, axis=0, keepdims=True)

    def accumulate(k0, slot):
        vals_t = vst_ref[0, 0, :, pl.ds(pl.multiple_of(k0, tq), tq)]
        for cs in col_groups:
            m_old = m_scr[0:1, cs]
            m_new = jnp.maximum(m_old, mx_scr[slot, 0:1, cs])
            p = jnp.exp2(s_bufs[slot][:, cs] - m_new).astype(BF16)
            acc_scr[:, cs] = jnp.exp2(m_old - m_new) * acc_scr[:, cs] + jnp.dot(vals_t, p, preferred_element_type=F32)
            m_scr[0:1, cs] = m_new

    n_pairs = qi // 2
    scores(s0, 0, True)

    def pair(j, _):
        scores(2 * j * tq, 1, False)
        accumulate(jnp.where(j == 0, s0, (2 * j - 1) * tq), 0)
        scores((2 * j + 1) * tq, 0, False)
        accumulate(2 * j * tq, 1)
        return 0

    lax.fori_loop(0, n_pairs, pair, 0)
    pending = jnp.where(n_pairs == 0, s0, (2 * n_pairs - 1) * tq)

    @pl.when(qi - 2 * n_pairs == 1)
    def _():
        scores(s0 - tq, 1, False)
        accumulate(pending, 0)
        accumulate(s0 - tq, 1)

    @pl.when(qi - 2 * n_pairs == 0)
    def _():
        accumulate(pending, 0)

    o_s = acc_scr[:HEAD_DIM, :] / jnp.maximum(acc_scr[HEAD_DIM:HEAD_DIM + 1, :], 1e-30)

    def gate(r, k):
        return gbt_ref[0, pl.ds(3 * (GROUP * pl.program_id(1) + r) + k, 1), :]

    zb = zb_ref[0]
    heads = []
    for r in range(GROUP):
        rs = slice(r * tq, (r + 1) * tq)
        heads.append(gate(r, 0) * o_c[:HEAD_DIM, rs] + gate(r, 1) * o_s[:, rs] + gate(r, 2) * o_w[:, rs])
    for pair in range(GROUP // 2):
        slab = jnp.concatenate([heads[2 * pair], heads[2 * pair + 1]], axis=0).T
        o_ref[0, :, pair * LANE:(pair + 1) * LANE] = (zb[:, pair * LANE:(pair + 1) * LANE] * slab).astype(BF16)


def _attn_prompt(q_t, gb_t, zb_act, kc_ext, vc_t, ks_ext, vs_t, kw_pad, vw_pad, *, tq):
    n_b, seq_len = zb_act.shape[0], zb_act.shape[1]
    n_cmp = kc_ext.shape[2]
    n_blk = seq_len // SLC_BLOCK
    head_w = GROUP * HEAD_DIM
    rows = GROUP * tq
    assert WINDOW % tq == 0 and rows % min(rows, CHAIN_COLS) == 0
    off = _iota((tq, rows), 0) - _mod(_iota((tq, rows), 1), tq)
    tri = jnp.stack([jnp.where(off > 0, 0.0, NEG_INF), jnp.where(off <= 0, 0.0, NEG_INF)]).astype(F32)
    tok_sub = lambda n: pl.BlockSpec((1, 1, n, LANE), lambda b, g, i: (b, g, 0, 0))
    tok_lane = lambda n: pl.BlockSpec((1, 1, LANE, n), lambda b, g, i: (b, g, 0, 0))
    return pl.pallas_call(
        functools.partial(_attn_prompt_kernel, tq=tq, n_cmp=n_cmp, n_blk=n_blk),
        grid=(n_b, N_KV, seq_len // tq),
        in_specs=[pl.BlockSpec((1, GROUP, HEAD_DIM, tq), lambda b, g, i: (b, g, 0, i)),
                  pl.BlockSpec((1, LANE, tq), lambda b, g, i: (b, 0, i)),
                  pl.BlockSpec((1, tq, head_w), lambda b, g, i: (b, i, g)),
                  _const_spec((2, tq, rows)),
                  tok_sub(n_cmp), tok_lane(n_cmp), tok_sub(seq_len), tok_lane(seq_len),
                  tok_sub(seq_len + WINDOW), tok_lane(seq_len + WINDOW)],
        out_specs=pl.BlockSpec((1, tq, head_w), lambda b, g, i: (b, i, g)),
        out_shape=jax.ShapeDtypeStruct((n_b, seq_len, ATT_DIM), BF16),
        scratch_shapes=[pltpu.VMEM((MAX_BLOCKS, tq), F32), pltpu.VMEM((MAX_BLOCKS, tq), F32),
                        pltpu.VMEM((SUBLANE, rows), F32),
                        pltpu.VMEM((LANE, rows), F32), pltpu.VMEM((2, SUBLANE, rows), F32),
                        pltpu.VMEM((tq, rows), F32), pltpu.VMEM((tq, rows), F32)],
        compiler_params=_params(3),
        name="attention_prompt",
    )(q_t, gb_t, zb_act, tri, kc_ext, vc_t, ks_ext, vs_t, kw_pad, vw_pad)


def _attn_sample_kernel(pt_ref, *refs, n_pages, past_len, dec_len, n_blk_pad):
    pages = refs[:n_pages]
    (qbd_ref, qbdt_ref, kc_ref, vc_ref, newkv_ref, win_ref, newwin_ref, gate_ref, o_ref,
     score_scr, cnt_scr, bias_scr, m_scr, l_scr, acc_scr, oc_scr) = refs[n_pages:]
    st = pl.program_id(1)
    n_steps = pl.num_programs(1)
    cols = LANE
    tslots = cols // N_HEADS
    n_cmp = kc_ref.shape[1]
    n_past_blk = past_len // SLC_BLOCK
    per_page = PAGE_SIZE // SLC_BLOCK
    qbdt = qbdt_ref[0]
    tok_c = _mod(_iota((cols, 1), 0), tslots)

    @pl.when(st == 0)
    def _():
        qpos = past_len + _mod(_iota((1, cols), 1), tslots)
        s_c = jnp.dot(kc_ref[0], qbd_ref[0], preferred_element_type=F32)
        m_idx = _iota((n_cmp, 1), 0)
        mask_c = (m_idx >= 1) & (CMP_STRIDE * m_idx + CMP_STRIDE - 1 <= qpos)
        s_c = jnp.where(mask_c, s_c, NEG_INF)
        e_c = jnp.where(mask_c, jnp.exp(s_c - jnp.max(s_c, axis=0, keepdims=True)), 0.0)
        p_c = e_c / jnp.maximum(jnp.sum(e_c, axis=0, keepdims=True), 1e-30)
        p_hi = p_c.astype(BF16)
        p_lo = (p_c - p_hi.astype(F32)).astype(BF16)
        oc_scr[...] = jnp.dot(p_c.T.astype(BF16), vc_ref[0], preferred_element_type=F32)
        ov_t = _overlap_t(n_blk_pad, n_cmp)
        imp1 = jnp.dot(ov_t, p_hi, preferred_element_type=F32) + jnp.dot(ov_t, p_lo, preferred_element_type=F32)
        ca = _iota((cols, cols), 0)
        cb = _iota((cols, cols), 1)
        per_group = GROUP * tslots
        same = jnp.where((_div(ca, per_group) == _div(cb, per_group)) & (_mod(ca, tslots) == _mod(cb, tslots)),
                         1.0, 0.0).astype(BF16)
        i_hi = imp1.astype(BF16)
        i_mid = (imp1 - i_hi.astype(F32)).astype(BF16)
        i_lo = (imp1 - i_hi.astype(F32) - i_mid.astype(F32)).astype(BF16)
        imp = (jnp.dot(i_hi, same, preferred_element_type=F32) + jnp.dot(i_mid, same, preferred_element_type=F32)
               + jnp.dot(i_lo, same, preferred_element_type=F32))
        blk = _iota((n_blk_pad, cols), 0)
        cur = _div(qpos, SLC_BLOCK)
        forced = (blk == 0) | (blk == cur) | (blk == cur - 1)
        score = jnp.where(forced, FORCE_SCORE, imp)
        score_scr[...] = jnp.where(blk <= cur, score, -1.0)
        n_blk = n_past_blk + 1
        bias_scr[...] = _select_bias(score_scr, cnt_scr, n_blk, None, min(TOP_N, n_blk)).T
        m_scr[...] = jnp.full(m_scr.shape, NEG_INF, F32)
        l_scr[...] = jnp.zeros(l_scr.shape, F32)
        acc_scr[...] = jnp.zeros(acc_scr.shape, F32)

    def update(s, pv_fn):
        m_old = m_scr[...]
        m_new = jnp.maximum(m_old, jnp.max(s, axis=1, keepdims=True))
        alpha = jnp.exp(m_old - m_new)
        p = jnp.exp(s - m_new)
        l_scr[...] = alpha * l_scr[...] + jnp.sum(p, axis=1, keepdims=True)
        acc_scr[...] = alpha * acc_scr[...] + pv_fn(p.astype(BF16))
        m_scr[...] = m_new

    bias_b = bias_scr[...].astype(BF16)
    blk0 = st * (n_pages * per_page)
    key_blk = blk0 + _div(_iota((n_blk_pad, n_pages * PAGE_SIZE), 1), SLC_BLOCK)
    expand = jnp.where(_iota((n_blk_pad, n_pages * PAGE_SIZE), 0) == key_blk, 1.0, 0.0).astype(BF16)
    keys_t = jnp.concatenate([pages[j][0].astype(BF16) for j in range(n_pages)], axis=1)
    vals_t = jnp.concatenate([pages[j][1].astype(BF16) for j in range(n_pages)], axis=1)
    s = (jnp.dot(qbdt, keys_t, preferred_element_type=F32) + jnp.dot(bias_b, expand, preferred_element_type=F32))
    update(s, lambda p: lax.dot_general(p, vals_t, NT_DIMS, preferred_element_type=F32))

    @pl.when(st == n_steps - 1)
    def _():
        def pad_rows(a):
            return jnp.concatenate([a, jnp.zeros((LANE - a.shape[0], a.shape[1]), a.dtype)], axis=0)

        key_i = _iota((1, LANE), 1)
        new = pad_rows(newkv_ref[0])
        s_n = lax.dot_general(qbdt, new[:, 2 * KV_DIM:3 * KV_DIM].astype(BF16), NT_DIMS, preferred_element_type=F32)
        s_n = s_n + bias_scr[:, n_past_blk:n_past_blk + 1]
        s_n = jnp.where((key_i <= tok_c) & (key_i < dec_len), s_n, NEG_INF)
        v_n = new[:, 3 * KV_DIM:].astype(BF16)
        update(s_n, lambda p: jnp.dot(p, v_n, preferred_element_type=F32))
        o_s = acc_scr[...] / jnp.maximum(l_scr[...], 1e-30)

        w_buf = win_ref.shape[2]
        neww = pad_rows(newwin_ref[0])
        s_w = jnp.concatenate(
            [jnp.dot(qbdt, win_ref[0].astype(BF16), preferred_element_type=F32),
             lax.dot_general(qbdt, neww[:, :KV_DIM].astype(BF16), NT_DIMS, preferred_element_type=F32)], axis=1)
        idx = _iota((1, w_buf + LANE), 1)
        kpos = past_len - w_buf + idx
        dist = past_len + tok_c - kpos
        mask_w = (dist >= 0) & (dist < WINDOW) & (idx < w_buf + dec_len) & (kpos >= 0)
        s_w = jnp.where(mask_w, s_w, NEG_INF)
        e_w = jnp.where(mask_w, jnp.exp(s_w - jnp.max(s_w, axis=1, keepdims=True)), 0.0)
        p_w = (e_w / jnp.maximum(jnp.sum(e_w, axis=1, keepdims=True), 1e-30)).astype(BF16)
        o_w = (lax.dot_general(p_w[:, :w_buf], win_ref[1].astype(BF16), NT_DIMS, preferred_element_type=F32)
               + jnp.dot(p_w[:, w_buf:], neww[:, KV_DIM:].astype(BF16), preferred_element_type=F32))
        gate = gate_ref[0]
        o_ref[0] = gate[:, 0:1] * oc_scr[...] + gate[:, 1:2] * o_s + gate[:, 2:3] * o_w


def _attn_sample(cache_t, page_table, page_index, win_t, win_index, qbd, qbdt, kc_flat, vc_flat, newkv8, newwin8,
                 gates, *, n_pages, past_len, dec_len):
    n_seq, pages_per_seq = page_table.shape
    n_steps = pages_per_seq // n_pages
    n_cmp = kc_flat.shape[1]
    n_blk_pad = -(-(past_len // SLC_BLOCK + 1) // LANE) * LANE
    w_buf = win_t.shape[2]

    def page_spec(j):
        return pl.BlockSpec((2, KV_DIM, PAGE_SIZE), lambda b, s, pt: (page_index(pt[b, s * n_pages + j]), 0, 0))

    per_seq = lambda shape: pl.BlockSpec((1,) + shape, lambda b, s, pt: (b,) + (0,) * len(shape))
    grid_spec = pltpu.PrefetchScalarGridSpec(
        num_scalar_prefetch=1,
        grid=(n_seq, n_steps),
        in_specs=[page_spec(j) for j in range(n_pages)] + [
            per_seq((KV_DIM, LANE)), per_seq((LANE, KV_DIM)), per_seq((n_cmp, KV_DIM)), per_seq((n_cmp, KV_DIM)),
            per_seq((SUBLANE, 4 * KV_DIM)),
            pl.BlockSpec((2, KV_DIM, w_buf), lambda b, s, pt: (win_index(b), 0, 0)),
            per_seq((SUBLANE, 2 * KV_DIM)), per_seq((LANE, LANE))],
        out_specs=per_seq((LANE, KV_DIM)),
        scratch_shapes=[pltpu.VMEM((n_blk_pad, LANE), F32), pltpu.VMEM((n_blk_pad, LANE), F32),
                        pltpu.VMEM((LANE, n_blk_pad), F32),
                        pltpu.VMEM((LANE, 1), F32), pltpu.VMEM((LANE, 1), F32),
                        pltpu.VMEM((LANE, KV_DIM), F32), pltpu.VMEM((LANE, KV_DIM), F32)],
    )
    return pl.pallas_call(
        functools.partial(_attn_sample_kernel, n_pages=n_pages, past_len=past_len, dec_len=dec_len,
                          n_blk_pad=n_blk_pad),
        grid_spec=grid_spec,
        out_shape=jax.ShapeDtypeStruct((n_seq, LANE, KV_DIM), F32),
        compiler_params=_params(2),
        name="attention_sample",
    )(page_table, *([cache_t] * n_pages), qbd, qbdt, kc_flat, vc_flat, newkv8, win_t, newwin8, gates)


def _permute_w_in(w):
    lead = w.shape[:-1]
    c = np.cumsum([0, 4 * CONV_DIM, ATT_DIM, 4 * KV_DIM, 2 * KV_DIM, 3 * N_HEADS, ATT_DIM, 2 * D_MODEL])
    wa, wq, wkv, wwin, wgb, wzb, wgm = (w[..., c[i]:c[i + 1]] for i in range(7))
    wgb = jnp.pad(wgb, ((0, 0),) * len(lead) + ((0, LANE - 3 * N_HEADS),))
    return jnp.concatenate([wa, wq, wkv, wwin, wzb, wgm, wgb], axis=-1).astype(BF16)


def _rope_tables(pos):
    half = HEAD_DIM // 2
    inv = jnp.power(ROPE_THETA, -jnp.arange(half, dtype=F32) / half)
    ang = pos.astype(F32)[:, None] * inv[None, :]
    cos, sin = jnp.cos(ang), jnp.sin(ang)
    reps = LANE // HEAD_DIM
    return jnp.tile(jnp.concatenate([cos, cos], axis=1), (1, reps)), jnp.tile(jnp.concatenate([-sin, sin], axis=1), (1, reps))


def _compress_weights(pos_k, w1_k, w2_k, pos_v, w1_v, w2_v, grouped):
    depth = pos_k.shape[0]
    flat = CMP_STRIDE * HEAD_DIM
    pos2 = jnp.stack([jnp.pad(p.reshape(depth, 2, flat), ((0, 0), (0, SUBLANE - 2), (0, 0))) for p in (pos_k, pos_v)],
                     axis=1)
    w1 = jnp.stack([w.reshape(depth, 2, flat, CMP_HID) for w in (w1_k, w1_v)], axis=1).astype(BF16)
    if grouped:
        w2 = jnp.stack([jnp.pad(w2_k, ((0, 0), (0, 0), (0, CMP_HID - HEAD_DIM)))[:, None],
                        jnp.pad(w2_v.transpose(0, 2, 1), ((0, 0), (0, CMP_HID - HEAD_DIM), (0, 0)))[:, None]], axis=1)
    else:
        w2 = jnp.stack([jnp.stack([jnp.pad(w, ((0, 0), (0, 0), (g * HEAD_DIM, KV_DIM - (g + 1) * HEAD_DIM)))
                                   for g in range(N_KV)], axis=1) for w in (w2_k, w2_v)], axis=1)
    return pos2, w1, w2.astype(BF16)


def kernel(x_prompt, x_sample, cache_kv, state_win, state_conv, page_table, p_prompt, p_sample, ln_g, w_in, conv_w,
           cmp_pos_k, cmp_w1_k, cmp_w2_k, cmp_pos_v, cmp_w1_v, cmp_w2_v, w_out_a, w_out_b, w_o, w_ple, w_ple_gate,
           final_g):
    depth = ln_g.shape[0]
    n_b, seq_len, _ = x_prompt.shape
    n_dec, dec_len, _ = x_sample.shape
    pages_per_seq = page_table.shape[1]
    past_len = pages_per_seq * PAGE_SIZE
    n_pool = cache_kv.shape[1]
    w_buf = state_win.shape[2]

    tm_p = min(256, seq_len)
    tq = min(256, seq_len)
    tm_merge = min(512, seq_len)
    pages_p = min(16, seq_len // PAGE_SIZE)
    pages_s = min(32, pages_per_seq)
    pages_a = min(64, pages_per_seq)
    rows_s = n_dec * dec_len
    carry_p = SUBLANE
    carry_s = (CONV_W - 1) * n_dec
    assert seq_len % PAGE_SIZE == 0 and seq_len // SLC_BLOCK <= MAX_BLOCKS and seq_len >= WINDOW + tq
    assert CONV_W - 1 <= dec_len <= SUBLANE and rows_s % SUBLANE == 0
    assert w_buf == min(WINDOW, past_len) and N_HEADS * SUBLANE == LANE

    cos_p, sin_p = _rope_tables(jnp.arange(seq_len, dtype=jnp.int32))
    pos_s = past_len + jnp.repeat(jnp.arange(dec_len, dtype=jnp.int32), n_dec)
    cos_s, sin_s = _rope_tables(pos_s)
    prompt_pages = jnp.arange(n_b * (seq_len // PAGE_SIZE), dtype=jnp.int32).reshape(n_b, seq_len // PAGE_SIZE)
    eye_g = jnp.eye(N_KV, dtype=F32)
    tpad = SUBLANE - dec_len
    cache_t = cache_kv.transpose(0, 1, 3, 4, 5, 2).reshape(depth * n_pool * 4, KV_DIM, PAGE_SIZE)
    win_t = state_win.transpose(0, 1, 3, 4, 5, 2).reshape(depth * n_dec * 2, KV_DIM, w_buf)

    xp = x_prompt.reshape(n_b * seq_len, D_MODEL)
    xs = x_sample.transpose(1, 0, 2).reshape(rows_s, D_MODEL)
    kv_s, win_new_s, conv_p, conv_s = [], [], [], []
    carried = None
    w_perm = _permute_w_in(w_in)
    g_rows = ln_g.reshape(depth, 1, D_MODEL)
    cw8 = jnp.pad(conv_w, ((0, 0), (0, SUBLANE - CONV_W), (0, 0)))
    merge_w = (w_out_a.astype(BF16), w_out_b.astype(BF16), w_o.astype(BF16), w_ple.astype(BF16),
               w_ple_gate.astype(BF16), final_g.reshape(1, D_MODEL))
    cmp_args = (cmp_pos_k, cmp_w1_k, cmp_w2_k, cmp_pos_v, cmp_w1_v, cmp_w2_v)
    cmp_w_prompt = _compress_weights(*cmp_args, True)
    cmp_w_sample = _compress_weights(*cmp_args, False)
    zero_prefix = jnp.zeros((n_b, carry_p, CONV_DIM), F32)
    p_sample_rows = p_sample.transpose(0, 2, 1, 3).reshape(depth, rows_s, PLE_DIM)
    for i in range(depth):
        last = i == depth - 1

        (a_act, newkv_all, win_all, zb_act, gmix, tail, q_t, ks_ext, vs_t, kw_pad, vw_pad, gb_t, cmp_rows) = _project(
            xp, g_rows, w_perm, cos_p, sin_p, zero_prefix, cw8, carried, _window_pads(n_b, seq_len), layer=i,
            n_seq=n_b, rows_per_seq=seq_len, tm=tm_p, stride=1, carry=carry_p, transposed=True, pos_per_seq=True)
        carried = (newkv_all, win_all)
        kc_ext, vc_t = _compress(cmp_rows, prompt_pages, lambda p: p, *cmp_w_prompt, layer=i, n_pages=pages_p,
                                 prompt=True)
        y_act = _attn_prompt(q_t, gb_t, zb_act.reshape(n_b, seq_len, ATT_DIM), kc_ext, vc_t, ks_ext, vs_t, kw_pad,
                             vw_pad, tq=tq)
        xp = _merge(xp, a_act, y_act.reshape(n_b * seq_len, ATT_DIM), None, gmix,
                    p_prompt.reshape(depth, n_b * seq_len, PLE_DIM), *merge_w, layer=i, tm=tm_merge, final=last)
        conv_p.append(tail[:, carry_p - (CONV_W - 1):])

        prefix_s = state_conv[i].transpose(1, 0, 2).reshape(1, carry_s, CONV_DIM)
        (a_act, newkv, win, zb_act, gmix, tail, qp, gb) = _project(
            xs, g_rows, w_perm, cos_s, sin_s, prefix_s, cw8, None, None, layer=i,
            n_seq=1, rows_per_seq=rows_s, tm=rows_s, stride=n_dec, carry=carry_s, transposed=False, pos_per_seq=False)
        base = i * n_pool
        kc_flat, vc_flat = _compress(cache_t, page_table, lambda p: (base + p) * 2, *cmp_w_sample, layer=i,
                                     n_pages=pages_s, prompt=False)
        q5 = jnp.pad(qp.reshape(dec_len, n_dec, N_KV, GROUP, HEAD_DIM).astype(F32),
                     ((0, tpad), (0, 0), (0, 0), (0, 0), (0, 0)))
        qbd = q5.transpose(1, 4, 2, 3, 0)[:, None] * eye_g[None, :, None, :, None, None]
        qbd = qbd.reshape(n_dec, KV_DIM, LANE).astype(BF16)
        qbdt = q5.transpose(1, 2, 3, 0, 4)[:, :, :, :, None] * eye_g[None, :, None, None, :, None]
        qbdt = qbdt.reshape(n_dec, LANE, KV_DIM).astype(BF16)
        gates = gb[:, :3 * N_HEADS].reshape(dec_len, n_dec, N_KV, GROUP, 3)
        gates = jnp.pad(gates.transpose(1, 2, 3, 0, 4), ((0, 0), (0, 0), (0, 0), (0, tpad), (0, LANE - 3)))
        gates = gates.reshape(n_dec, LANE, LANE)
        newkv_b = newkv.reshape(dec_len, n_dec, 4 * KV_DIM).transpose(1, 0, 2)
        newwin_b = win.reshape(dec_len, n_dec, 2 * KV_DIM).transpose(1, 0, 2)
        pad_rows = lambda a: jnp.pad(a, ((0, 0), (0, tpad), (0, 0)))
        o_all = _attn_sample(cache_t, page_table, lambda p: (base + p) * 2 + 1, win_t, lambda b: i * n_dec + b, qbd,
                             qbdt, kc_flat, vc_flat, pad_rows(newkv_b), pad_rows(newwin_b), gates, n_pages=pages_a,
                             past_len=past_len, dec_len=dec_len)
        o6 = o_all.reshape(n_dec, N_KV, GROUP, SUBLANE, N_KV, HEAD_DIM)
        yb = jnp.stack([o6[:, g, :, :, g] for g in range(N_KV)], axis=1)
        yb = yb.transpose(3, 0, 1, 2, 4)[:dec_len].reshape(rows_s, ATT_DIM)
        xs = _merge(xs, a_act, yb, zb_act, gmix, p_sample_rows, *merge_w, layer=i, tm=rows_s, final=last)
        kv_s.append(newkv_b.reshape(n_dec, dec_len, 4, N_KV, HEAD_DIM))
        win_new_s.append(newwin_b.reshape(n_dec, dec_len, 2, N_KV, HEAD_DIM))
        conv_s.append(tail.reshape(CONV_W - 1, n_dec, CONV_DIM).transpose(1, 0, 2))

    y_prompt = xp.reshape(n_b, seq_len, D_MODEL)
    y_sample = xs.reshape(dec_len, n_dec, D_MODEL).transpose(1, 0, 2)
    newkv_all, win_all = carried
    kv_p = newkv_all.reshape(depth, n_b, 4, N_KV, HEAD_DIM, seq_len).transpose(0, 1, 5, 2, 3, 4)
    win_p = win_all.reshape(depth, n_b, 2, N_KV, HEAD_DIM, min(WINDOW, seq_len)).transpose(0, 1, 5, 2, 3, 4)
    win_s = jnp.concatenate([state_win, jnp.stack(win_new_s)], axis=2)[:, :, -w_buf:]
    return (y_prompt, y_sample, kv_p, jnp.stack(kv_s), win_p, win_s, jnp.stack(conv_p), jnp.stack(conv_s))
```
